```python
import math
import jax
import jax.numpy as jnp
from jax import lax
import numpy as np

D_MODEL = 1024
BATCH = 4
SEQ = 8192
DEPTH = 2

GRID_W = 64
CTX_LEN = 256
N_MIXERS = 2
HEAD_DIM = 64
A_HEADS = 16
A_KV_HEADS = 4
A_WINDOW = 128
A_BLOCK = 128
B_HEADS = 16
NA_ROWS = 8
NA_COLS = 16
NA_QCOLS = 16
NA_KCOLS = 32
N_EXPERTS = 16
EXPERT_FF = 2048
EC_CAPACITY = 2
ROPE_BASE = 10000.0
EPS = 1e-6
NEG_INF = -1e30
ATTN_SCALE = HEAD_DIM ** -0.5

kernel_name = 'hybrid_swa_natten_ecmoe_dit'


def rms_norm(x, g):
    xf = x.astype(jnp.float32)
    y = xf * lax.rsqrt(jnp.mean(xf * xf, axis=-1, keepdims=True) + EPS)
    return (y * g.astype(jnp.float32)).astype(x.dtype)


def modulate(h, shift, scale):
    return h * (1 + scale[:, None]) + shift[:, None]


def axial_rope_tables(n_tok):
    pos = jnp.arange(n_tok)
    row = (pos // GRID_W).astype(jnp.float32)
    col = (pos % GRID_W).astype(jnp.float32)
    n_freq = HEAD_DIM // 4
    inv = ROPE_BASE ** (-jnp.arange(n_freq, dtype=jnp.float32) / n_freq)
    ang_r = row[:, None] * inv
    ang_c = col[:, None] * inv
    return (jnp.cos(ang_r), jnp.sin(ang_r), jnp.cos(ang_c), jnp.sin(ang_c))


def axial_rope(x, tabs):
    cos_r, sin_r, cos_c, sin_c = tabs
    S = x.shape[1]

    def rot(v, cs, sn):
        shape = (1, S) + (1,) * (v.ndim - 3) + (cs.shape[-1],)
        cs = cs.reshape(shape)
        sn = sn.reshape(shape)
        v1, v2 = jnp.split(v.astype(jnp.float32), 2, axis=-1)
        return jnp.concatenate([v1 * cs - v2 * sn, v1 * sn + v2 * cs], axis=-1)

    xr, xc = jnp.split(x, 2, axis=-1)
    return jnp.concatenate([rot(xr, cos_r, sin_r), rot(xc, cos_c, sin_c)], axis=-1).astype(x.dtype)


def sink_softmax(s, sink):
    m = jnp.maximum(jnp.max(s, axis=-1, keepdims=True), sink)
    p = jnp.exp(s - m)
    return p / (jnp.sum(p, axis=-1, keepdims=True) + jnp.exp(sink - m))


def window_gqa_sink(hl, hc, w_qkv, q_gain, k_gain, sink, w_o, rope_tabs, ctx_out):
    B, S, _ = hl.shape
    G = A_HEADS // A_KV_HEADS
    nb = S // A_BLOCK
    cuts = [A_HEADS * HEAD_DIM, (A_HEADS + A_KV_HEADS) * HEAD_DIM]

    def project(h):
        n = h.shape[1]
        q, k, v = jnp.split(h @ w_qkv, cuts, axis=-1)
        q = rms_norm(q.reshape(B, n, A_KV_HEADS, G, HEAD_DIM), q_gain)
        k = rms_norm(k.reshape(B, n, A_KV_HEADS, HEAD_DIM), k_gain)
        return q, k, v.reshape(B, n, A_KV_HEADS, HEAD_DIM)

    ql, kl, vl = project(hl)
    qc, kc, vc = project(hc)
    ql = axial_rope(ql, rope_tabs)
    kl = axial_rope(kl, rope_tabs)
    sink_l = sink.astype(jnp.float32).reshape(1, A_KV_HEADS, G, 1, 1)

    pad = ((0, 0), (A_BLOCK, A_BLOCK), (0, 0), (0, 0))
    kpad = jnp.pad(kl, pad)
    vpad = jnp.pad(vl, pad)
    qb = jnp.moveaxis(ql.reshape(B, nb, A_BLOCK, A_KV_HEADS, G, HEAD_DIM), 1, 0)
    span = 3 * A_BLOCK
    rel = jnp.arange(A_BLOCK)[:, None] - (jnp.arange(span)[None, :] - A_BLOCK)
    in_window = jnp.abs(rel) <= A_WINDOW

    def block(args):
        i, q = args
        start = i * A_BLOCK
        kw = lax.dynamic_slice_in_dim(kpad, start, span, axis=1)
        vw = lax.dynamic_slice_in_dim(vpad, start, span, axis=1)
        kpos = start - A_BLOCK + jnp.arange(span)
        valid = in_window & ((kpos >= 0) & (kpos < S))[None, :]
        s_loc = jnp.einsum('bqhgd,bkhd->bhgqk', q, kw).astype(jnp.float32) * ATTN_SCALE
        s_loc = jnp.where(valid, s_loc, NEG_INF)
        s_ctx = jnp.einsum('bqhgd,bkhd->bhgqk', q, kc).astype(jnp.float32) * ATTN_SCALE
        p = sink_softmax(jnp.concatenate([s_loc, s_ctx], axis=-1), sink_l).astype(vl.dtype)
        return (jnp.einsum('bhgqk,bkhd->bqhgd', p[..., :span], vw)
                + jnp.einsum('bhgqk,bkhd->bqhgd', p[..., span:], vc))

    o = lax.map(block, (jnp.arange(nb), qb))
    yl = jnp.moveaxis(o, 0, 1).reshape(B, S, A_HEADS * HEAD_DIM) @ w_o
    if not ctx_out:
        return yl, None
    s_cc = jnp.einsum('bqhgd,bkhd->bhgqk', qc, kc).astype(jnp.float32) * ATTN_SCALE
    p_cc = sink_softmax(s_cc, sink_l).astype(vc.dtype)
    oc = jnp.einsum('bhgqk,bkhd->bqhgd', p_cc, vc).reshape(B, hc.shape[1], A_HEADS * HEAD_DIM)
    return yl, oc @ w_o


def neighbourhood_attention(hl, hc, w_qkv, q_gain, k_gain, rpb, w_o, ctx_out):
    B, S, _ = hl.shape
    R = S // GRID_W
    wr = min(NA_ROWS, R)
    nc = GRID_W // NA_QCOLS
    nk = wr * NA_KCOLS

    def project(h):
        n = h.shape[1]
        q, k, v = jnp.split(h @ w_qkv, 3, axis=-1)
        shp = (B, n, B_HEADS, HEAD_DIM)
        return rms_norm(q.reshape(shp), q_gain), rms_norm(k.reshape(shp), k_gain), v.reshape(shp)

    ql, kl, vl = project(hl)
    qc, kc, vc = project(hc)
    qg = jnp.moveaxis(ql.reshape(B, R, nc, NA_QCOLS, B_HEADS, HEAD_DIM), 1, 0)
    kg = kl.reshape(B, R, GRID_W, B_HEADS, HEAD_DIM)
    vg = vl.reshape(B, R, GRID_W, B_HEADS, HEAD_DIM)

    qcol = np.arange(nc)[:, None] * NA_QCOLS + np.arange(NA_QCOLS)[None, :]
    kstart = np.clip(np.arange(nc) * NA_QCOLS - (NA_KCOLS - NA_QCOLS) // 2, 0, GRID_W - NA_KCOLS)
    kcol = kstart[:, None] + np.arange(NA_KCOLS)[None, :]
    wstart = np.clip(qcol - NA_COLS // 2, 0, GRID_W - NA_COLS)
    kc3 = kcol[:, None, :]
    col_ok = (kc3 >= wstart[..., None]) & (kc3 < wstart[..., None] + NA_COLS)
    dcol_idx = np.clip(kc3 - qcol[..., None], -(NA_COLS - 1), NA_COLS - 1) + NA_COLS - 1
    mask = np.broadcast_to(col_ok[:, :, None, :], (nc, NA_QCOLS, wr, NA_KCOLS)).reshape(nc, NA_QCOLS, nk)
    rpb_cols = rpb.astype(jnp.float32)[:, :, dcol_idx]

    def row(args):
        r, q = args
        rs = jnp.clip(r - wr // 2, 0, R - wr)
        kr = lax.dynamic_slice_in_dim(kg, rs, wr, axis=1)[:, :, kcol]
        vr = lax.dynamic_slice_in_dim(vg, rs, wr, axis=1)[:, :, kcol]
        kr = jnp.transpose(kr, (0, 2, 1, 3, 4, 5)).reshape(B, nc, nk, B_HEADS, HEAD_DIM)
        vr = jnp.transpose(vr, (0, 2, 1, 3, 4, 5)).reshape(B, nc, nk, B_HEADS, HEAD_DIM)
        dr_idx = rs - r + jnp.arange(wr) + NA_ROWS - 1
        bias = jnp.transpose(rpb_cols[:, dr_idx], (0, 2, 3, 1, 4)).reshape(B_HEADS, nc, NA_QCOLS, nk)
        s_loc = jnp.einsum('bjqhd,bjkhd->bhjqk', q, kr).astype(jnp.float32) * ATTN_SCALE + bias
        s_loc = jnp.where(mask, s_loc, NEG_INF)
        s_ctx = jnp.einsum('bjqhd,bkhd->bhjqk', q, kc).astype(jnp.float32) * ATTN_SCALE
        p = jax.nn.softmax(jnp.concatenate([s_loc, s_ctx], axis=-1), axis=-1).astype(vl.dtype)
        return (jnp.einsum('bhjqk,bjkhd->bjqhd', p[..., :nk], vr)
                + jnp.einsum('bhjqk,bkhd->bjqhd', p[..., nk:], vc))

    o = lax.map(row, (jnp.arange(R), qg))
    yl = jnp.moveaxis(o, 0, 1).reshape(B, S, B_HEADS * HEAD_DIM) @ w_o
    if not ctx_out:
        return yl, None
    s_cc = jnp.einsum('bqhd,bkhd->bhqk', qc, kc).astype(jnp.float32) * ATTN_SCALE
    p_cc = jax.nn.softmax(s_cc, axis=-1).astype(vc.dtype)
    oc = jnp.einsum('bhqk,bkhd->bqhd', p_cc, vc).reshape(B, hc.shape[1], B_HEADS * HEAD_DIM)
    return yl, oc @ w_o


def ec_moe(x, w_router, w_gate, w_up, w_down):
    B, S, _ = x.shape
    cap = EC_CAPACITY * S // N_EXPERTS
    aff = jax.nn.softmax(jnp.einsum('bsd,de->bse', x, w_router).astype(jnp.float32), axis=-1)
    gate, idx = lax.top_k(jnp.swapaxes(aff, 1, 2), cap)
    bidx = jnp.arange(B)[:, None, None]
    xg = x[bidx, idx]
    h = jax.nn.silu(jnp.einsum('becd,edf->becf', xg, w_gate)) * jnp.einsum('becd,edf->becf', xg, w_up)
    y = jnp.einsum('becf,efd->becd', h, w_down) * gate[..., None].astype(x.dtype)
    return jnp.zeros_like(x).at[bidx, idx].add(y)


def setup_inputs(seed: int = 0) -> dict:
    key = jax.random.key(seed)
    ks = jax.random.split(key, 24)
    n_a = (DEPTH + N_MIXERS - 1) // N_MIXERS
    n_b = DEPTH // N_MIXERS
    f32 = jnp.float32

    def nrm(k, shape, scale):
        return jax.random.normal(k, shape, f32) * scale

    qkv_a = (A_HEADS + 2 * A_KV_HEADS) * HEAD_DIM
    d_a = A_HEADS * HEAD_DIM
    d_b = B_HEADS * HEAD_DIM
    return {
        'x': nrm(ks[0], (BATCH, SEQ, D_MODEL), 1.0),
        'c': nrm(ks[1], (BATCH, D_MODEL), 1.0),
        'ctx': nrm(ks[2], (BATCH, CTX_LEN, D_MODEL), 1.0),
        'c_ctx': nrm(ks[3], (D_MODEL,), 1.0),
        'w_mod': nrm(ks[4], (DEPTH, D_MODEL, 6 * D_MODEL), 0.5 * D_MODEL ** -0.5),
        'b_mod': nrm(ks[5], (DEPTH, 6 * D_MODEL), 0.02),
        'norm_mix': 1.0 + nrm(ks[6], (DEPTH, D_MODEL), 0.02),
        'norm_ffn': 1.0 + nrm(ks[7], (DEPTH, D_MODEL), 0.02),
        'a_w_qkv': nrm(ks[8], (n_a, D_MODEL, qkv_a), D_MODEL ** -0.5),
        'a_q_gain': 1.0 + nrm(ks[9], (n_a, HEAD_DIM), 0.02),
        'a_k_gain': 1.0 + nrm(ks[10], (n_a, HEAD_DIM), 0.02),
        'a_sink': nrm(ks[11], (n_a, A_HEADS), 0.5),
        'a_w_o': nrm(ks[12], (n_a, d_a, D_MODEL), d_a ** -0.5),
        'b_w_qkv': nrm(ks[13], (n_b, D_MODEL, 3 * d_b), D_MODEL ** -0.5),
        'b_q_gain': 1.0 + nrm(ks[14], (n_b, HEAD_DIM), 0.02),
        'b_k_gain': 1.0 + nrm(ks[15], (n_b, HEAD_DIM), 0.02),
        'b_rpb': nrm(ks[16], (n_b, B_HEADS, 2 * NA_ROWS - 1, 2 * NA_COLS - 1), 0.5),
        'b_w_o': nrm(ks[17], (n_b, d_b, D_MODEL), d_b ** -0.5),
        'moe_router': nrm(ks[18], (DEPTH, D_MODEL, N_EXPERTS), D_MODEL ** -0.5),
        'moe_w_gate': nrm(ks[19], (DEPTH, N_EXPERTS, D_MODEL, EXPERT_FF), D_MODEL ** -0.5),
        'moe_w_up': nrm(ks[20], (DEPTH, N_EXPERTS, D_MODEL, EXPERT_FF), D_MODEL ** -0.5),
        'moe_w_down': nrm(ks[21], (DEPTH, N_EXPERTS, EXPERT_FF, D_MODEL), EXPERT_FF ** -0.5),
    }


def reference(x, c, ctx, c_ctx, w_mod, b_mod, norm_mix, norm_ffn,
              a_w_qkv, a_q_gain, a_k_gain, a_sink, a_w_o,
              b_w_qkv, b_q_gain, b_k_gain, b_rpb, b_w_o,
              moe_router, moe_w_gate, moe_w_up, moe_w_down):
    rope_tabs = axial_rope_tables(x.shape[1])
    xl, xc = x, ctx
    for i in range(DEPTH):
        last = i == DEPTH - 1
        mod_l = jax.nn.silu(c) @ w_mod[i] + b_mod[i]
        mod_c = jax.nn.silu(c_ctx)[None] @ w_mod[i] + b_mod[i]
        sh1_l, sc1_l, g1_l, sh2_l, sc2_l, g2_l = jnp.split(mod_l, 6, axis=-1)
        sh1_c, sc1_c, g1_c, sh2_c, sc2_c, g2_c = jnp.split(mod_c, 6, axis=-1)
        hl = modulate(rms_norm(xl, norm_mix[i]), sh1_l, sc1_l)
        hc = modulate(rms_norm(xc, norm_mix[i]), sh1_c, sc1_c)
        j = i // N_MIXERS
        if i % N_MIXERS == 0:
            yl, yc = window_gqa_sink(hl, hc, a_w_qkv[j], a_q_gain[j], a_k_gain[j], a_sink[j],
                                     a_w_o[j], rope_tabs, not last)
        else:
            yl, yc = neighbourhood_attention(hl, hc, b_w_qkv[j], b_q_gain[j], b_k_gain[j], b_rpb[j],
                                             b_w_o[j], not last)
        xl = xl + g1_l[:, None] * yl
        hl = modulate(rms_norm(xl, norm_ffn[i]), sh2_l, sc2_l)
        xl = xl + g2_l[:, None] * ec_moe(hl, moe_router[i], moe_w_gate[i], moe_w_up[i], moe_w_down[i])
        if not last:
            xc = xc + g1_c[:, None] * yc
            hc = modulate(rms_norm(xc, norm_ffn[i]), sh2_c, sc2_c)
            xc = xc + g2_c[:, None] * ec_moe(hc, moe_router[i], moe_w_gate[i], moe_w_up[i], moe_w_down[i])
    return xl
```

```python
import functools

import numpy as np
import jax
import jax.numpy as jnp
from jax import lax
from jax.experimental import pallas as pl
from jax.experimental.pallas import tpu as pltpu

F32 = jnp.float32
BF16 = jnp.bfloat16

GRID_W = 64
HEAD_DIM = 64
A_HEADS = 16
A_KV_HEADS = 4
A_WINDOW = 128
A_BLOCK = 128
B_HEADS = 16
NA_ROWS = 8
NA_COLS = 16
N_EXPERTS = 16
EC_CAPACITY = 2
ROPE_BASE = 10000.0
EPS = 1e-6
NEG_INF = -1e30
ATTN_SCALE = HEAD_DIM ** -0.5
N_MIXERS = 2

NA_QROWS = 2
NA_KROWS = NA_ROWS + NA_QROWS - 1
NA_HEAD_GROUP = 4

V7X_VMEM_BYTES = 64 * 1024 * 1024
VMEM_LIMIT = 56 * 1024 * 1024

_NT = (((1,), (1,)), ((), ()))
_NN = (((1,), (0,)), ((), ()))


def _dot(a, b, dims=_NN):
    return lax.dot_general(a, b, dims, preferred_element_type=F32)


def _split(a):
    hi = a.astype(BF16)
    lo = (a - hi.astype(F32)).astype(BF16)
    return hi, lo


def _dot3(a, b, dims=_NN):
    ah, al = _split(a)
    bh, bl = _split(b)
    return _dot(ah, bh, dims) + (_dot(ah, bl, dims) + _dot(al, bh, dims))


def _params(sem, vmem=None):
    return pltpu.CompilerParams(dimension_semantics=sem, vmem_limit_bytes=vmem)


def _mod_kernel(c_ref, w_ref, b_ref, o_ref):
    c = c_ref[...]
    a = c * jax.nn.sigmoid(c)
    o_ref[0] = _dot3(a, w_ref[0]) + b_ref[0]


def _modulation(cvec, w_mod, b_mod):
    depth, d, n = w_mod.shape
    rows = cvec.shape[0]
    tn = 1024
    return pl.pallas_call(
        _mod_kernel,
        out_shape=jax.ShapeDtypeStruct((depth, rows, n), F32),
        grid=(depth, n // tn),
        in_specs=[
            pl.BlockSpec((rows, d), lambda i, j: (0, 0)),
            pl.BlockSpec((1, d, tn), lambda i, j: (i, 0, j)),
            pl.BlockSpec((1, 1, tn), lambda i, j: (i, 0, j)),
        ],
        out_specs=pl.BlockSpec((1, rows, tn), lambda i, j: (i, 0, j)),
        compiler_params=_params(("parallel", "parallel")),
        name="modulation",
    )(cvec, w_mod, b_mod.reshape(depth, 1, n))


def _rms_mod(x, g, shift, scale):
    y = x * lax.rsqrt(jnp.mean(x * x, axis=-1, keepdims=True) + EPS)
    return (y * g) * (1.0 + scale) + shift


def _proj_kernel(*refs, n_q, n_kv, rope):
    if rope:
        (x_ref, g_ref, sh_ref, sc_ref, w_ref, qg_ref, kg_ref, qgp_ref, kgp_ref, cos_ref, sin_ref,
         q_ref, k_ref, v_ref) = refs
    else:
        x_ref, g_ref, sh_ref, sc_ref, w_ref, qg_ref, kg_ref, q_ref, k_ref, v_ref = refs
    dh = HEAD_DIM
    h = _rms_mod(x_ref[0], g_ref[...], sh_ref[0], sc_ref[0]).astype(BF16)
    res = _dot(h, w_ref[...])
    n_qk = n_q + n_kv
    part0 = (n_q + 2 * n_kv) * dh

    def head(j, gain_ref, gainp_ref):
        t = res[:, j * dh:(j + 1) * dh]
        r = lax.rsqrt(jnp.mean(t * t, axis=-1, keepdims=True) + EPS)
        tn = (t * r) * gain_ref[...]
        if rope:
            tp = res[:, part0 + j * dh:part0 + (j + 1) * dh]
            tpn = (tp * r) * gainp_ref[...]
            tn = tn * cos_ref[...] + tpn * sin_ref[...]
        return tn.astype(BF16)

    for j in range(n_q):
        q_ref[0, j] = head(j, qg_ref, qgp_ref if rope else None)
    for j in range(n_kv):
        k_ref[0, j] = head(n_q + j, kg_ref, kgp_ref if rope else None)
        c0 = (n_qk + j) * dh
        v_ref[0, j] = res[:, c0:c0 + dh].astype(BF16)


def _rope_perm():
    d = np.arange(HEAD_DIM)
    return np.where((d % 32) < 16, d + 16, d - 16)


def _project(x, g, shift, scale, w, q_gain, k_gain, n_q, n_kv, rope_tabs=None):
    bx, n, d = x.shape
    dh = HEAD_DIM
    rope = rope_tabs is not None
    qg = (q_gain * ATTN_SCALE).reshape(1, dh).astype(F32)
    kg = k_gain.reshape(1, dh).astype(F32)
    wb = w.astype(BF16)
    if rope:
        perm = _rope_perm()
        cols = (np.arange(n_q + n_kv)[:, None] * dh + perm[None, :]).reshape(-1)
        wb = jnp.concatenate([wb, wb[:, cols]], axis=1)
    n_cols = wb.shape[1]
    tm = min(512, n)
    bm = shift.shape[0]
    mod_map = (lambda b, t: (b, 0, 0)) if bm == bx else (lambda b, t: (0, 0, 0))
    const2 = lambda b, t: (0, 0)
    in_specs = [
        pl.BlockSpec((1, tm, d), lambda b, t: (b, t, 0)),
        pl.BlockSpec((1, d), const2),
        pl.BlockSpec((1, 1, d), mod_map),
        pl.BlockSpec((1, 1, d), mod_map),
        pl.BlockSpec((d, n_cols), const2),
        pl.BlockSpec((1, dh), const2),
        pl.BlockSpec((1, dh), const2),
    ]
    args = [x, g.reshape(1, d), shift, scale, wb, qg, kg]
    if rope:
        cos_t, sin_t = rope_tabs
        perm = _rope_perm()
        in_specs += [pl.BlockSpec((1, dh), const2), pl.BlockSpec((1, dh), const2),
                     pl.BlockSpec((tm, dh), lambda b, t: (t, 0)), pl.BlockSpec((tm, dh), lambda b, t: (t, 0))]
        args += [qg[:, perm], kg[:, perm], cos_t, sin_t]
    out_shape = [jax.ShapeDtypeStruct((bx, n_q, n, dh), BF16),
                 jax.ShapeDtypeStruct((bx, n_kv, n, dh), BF16),
                 jax.ShapeDtypeStruct((bx, n_kv, n, dh), BF16)]
    out_specs = [pl.BlockSpec((1, n_q, tm, dh), lambda b, t: (b, 0, t, 0)),
                 pl.BlockSpec((1, n_kv, tm, dh), lambda b, t: (b, 0, t, 0)),
                 pl.BlockSpec((1, n_kv, tm, dh), lambda b, t: (b, 0, t, 0))]
    return pl.pallas_call(
        functools.partial(_proj_kernel, n_q=n_q, n_kv=n_kv, rope=rope),
        out_shape=out_shape,
        grid=(bx, n // tm),
        in_specs=in_specs,
        out_specs=out_specs,
        compiler_params=_params(("parallel", "parallel"), VMEM_LIMIT),
        name="qkv_proj_rope" if rope else "qkv_proj",
    )(*args)


def _rope_tables(n_tok):
    pos = jnp.arange(n_tok)
    row = (pos // GRID_W).astype(F32)
    col = (pos % GRID_W).astype(F32)
    n_freq = HEAD_DIM // 4
    inv = ROPE_BASE ** (-jnp.arange(n_freq, dtype=F32) / n_freq)
    ang_r = row[:, None] * inv
    ang_c = col[:, None] * inv
    cos_t = jnp.concatenate([jnp.cos(ang_r), jnp.cos(ang_r), jnp.cos(ang_c), jnp.cos(ang_c)], axis=-1)
    sin_t = jnp.concatenate([-jnp.sin(ang_r), jnp.sin(ang_r), -jnp.sin(ang_c), jnp.sin(ang_c)], axis=-1)
    return cos_t, sin_t


def _softmax_pv(s_parts, v_parts, sink=None):
    m = s_parts[0].max(axis=-1, keepdims=True)
    for s in s_parts[1:]:
        m = jnp.maximum(m, s.max(axis=-1, keepdims=True))
    if sink is not None:
        m = jnp.maximum(m, sink)
    den = jnp.exp(sink - m) if sink is not None else None
    acc = None
    for s, v in zip(s_parts, v_parts):
        p = jnp.exp(s - m)
        ps = p.sum(axis=-1, keepdims=True)
        den = ps if den is None else den + ps
        pv = _dot(p.astype(BF16), v)
        acc = pv if acc is None else acc + pv
    return acc / den


def _attn_a_kernel(q_ref, k_ref, v_ref, kc_ref, vc_ref, sink_ref, o_ref, *, seq, n_kv, group):
    blk = A_BLOCK
    span = 3 * blk
    i = pl.program_id(1)
    start = pl.multiple_of(jnp.clip((i - 1) * blk, 0, seq - span), blk)
    rows = lax.broadcasted_iota(jnp.int32, (group * blk, span), 0)
    cols = lax.broadcasted_iota(jnp.int32, (group * blk, span), 1)
    rel = (i * blk - start) + (rows & (blk - 1)) - cols
    valid = jnp.abs(rel) <= A_WINDOW
    outs = []
    for h in range(n_kv):
        q = q_ref[0, h * group:(h + 1) * group].reshape(group * blk, HEAD_DIM)
        kw = k_ref[0, h, pl.ds(start, span), :]
        vw = v_ref[0, h, pl.ds(start, span), :]
        s_loc = jnp.where(valid, _dot(q, kw, _NT), NEG_INF)
        s_ctx = _dot(q, kc_ref[0, h], _NT)
        o = _softmax_pv([s_loc, s_ctx], [vw, vc_ref[0, h]], sink_ref[h])
        for g in range(group):
            outs.append(o[g * blk:(g + 1) * blk])
    o_ref[0] = jnp.concatenate(outs, axis=-1).astype(BF16)


def _sink_cols(sink, n_kv, group, rows):
    s = sink.astype(F32).reshape(n_kv, group, 1, 1)
    return jnp.broadcast_to(s, (n_kv, group, rows, 1)).reshape(n_kv, group * rows, 1)


def _attn_a(q, k, v, kc, vc, sink):
    b, hq, s, dh = q.shape
    n_kv = k.shape[1]
    group = hq // n_kv
    nc = kc.shape[2]
    blk = A_BLOCK
    assert s % blk == 0 and s >= 3 * blk
    return pl.pallas_call(
        functools.partial(_attn_a_kernel, seq=s, n_kv=n_kv, group=group),
        out_shape=jax.ShapeDtypeStruct((b, s, hq * dh), BF16),
        grid=(b, s // blk),
        in_specs=[
            pl.BlockSpec((1, hq, blk, dh), lambda bi, i: (bi, 0, i, 0)),
            pl.BlockSpec((1, n_kv, s, dh), lambda bi, i: (bi, 0, 0, 0)),
            pl.BlockSpec((1, n_kv, s, dh), lambda bi, i: (bi, 0, 0, 0)),
            pl.BlockSpec((1, n_kv, nc, dh), lambda bi, i: (bi, 0, 0, 0)),
            pl.BlockSpec((1, n_kv, nc, dh), lambda bi, i: (bi, 0, 0, 0)),
            pl.BlockSpec((n_kv, group * blk, 1), lambda bi, i: (0, 0, 0)),
        ],
        out_specs=pl.BlockSpec((1, blk, hq * dh), lambda bi, i: (bi, i, 0)),
        compiler_params=_params(("parallel", "arbitrary"), VMEM_LIMIT),
        name="attn_window",
    )(q, k, v, kc, vc, _sink_cols(sink, n_kv, group, blk))


def _attn_ctx_kernel(q_ref, k_ref, v_ref, sink_ref, o_ref, *, n_kv, group, use_sink):
    nq = q_ref.shape[2]
    outs = []
    for h in range(n_kv):
        q = q_ref[0, h * group:(h + 1) * group].reshape(group * nq, HEAD_DIM)
        s = _dot(q, k_ref[0, h], _NT)
        o = _softmax_pv([s], [v_ref[0, h]], sink_ref[h] if use_sink else None)
        for g in range(group):
            outs.append(o[g * nq:(g + 1) * nq])
    o_ref[0] = jnp.concatenate(outs, axis=-1).astype(BF16)


def _attn_ctx(q, k, v, sink):
    b, hq, n, dh = q.shape
    n_kv = k.shape[1]
    group = hq // n_kv
    use_sink = sink is not None
    sink_c = _sink_cols(sink if use_sink else jnp.zeros((hq,), F32), n_kv, group, n)
    return pl.pallas_call(
        functools.partial(_attn_ctx_kernel, n_kv=n_kv, group=group, use_sink=use_sink),
        out_shape=jax.ShapeDtypeStruct((b, n, hq * dh), BF16),
        grid=(b,),
        in_specs=[
            pl.BlockSpec((1, hq, n, dh), lambda bi: (bi, 0, 0, 0)),
            pl.BlockSpec((1, n_kv, n, dh), lambda bi: (bi, 0, 0, 0)),
            pl.BlockSpec((1, n_kv, n, dh), lambda bi: (bi, 0, 0, 0)),
            pl.BlockSpec((n_kv, group * n, 1), lambda bi: (0, 0, 0)),
        ],
        out_specs=pl.BlockSpec((1, n, hq * dh), lambda bi: (bi, 0, 0)),
        compiler_params=_params(("parallel",)),
        name="attn_ctx",
    )(q, k, v, sink_c)


def _na_bias_tables(rpb, n_rows):
    w = GRID_W
    n_blocks = n_rows // NA_QROWS
    half = NA_ROWS // 2
    cfgs, cfg_of = {}, []
    for rb in range(n_blocks):
        r0 = rb * NA_QROWS
        rs = int(np.clip(r0 - half, 0, n_rows - NA_KROWS))
        key = (r0 - rs,) + tuple(int(np.clip(r0 + a - half, 0, n_rows - NA_ROWS)) - rs for a in range(NA_QROWS))
        cfg_of.append(cfgs.setdefault(key, len(cfgs)))
    qc = np.arange(w)
    kc = np.arange(w)
    wstart = np.clip(qc - NA_COLS // 2, 0, w - NA_COLS)
    col_ok = (kc[None, :] >= wstart[:, None]) & (kc[None, :] < wstart[:, None] + NA_COLS)
    dcol = np.clip(kc[None, :] - qc[:, None], -(NA_COLS - 1), NA_COLS - 1) + NA_COLS - 1
    dr_all, dc_all, ok_all = [], [], []
    for key in cfgs:
        off, rsr = key[0], key[1:]
        dr = np.zeros((NA_QROWS, w, NA_KROWS, w), np.int32)
        dc = np.zeros_like(dr)
        ok = np.zeros(dr.shape, bool)
        for a in range(NA_QROWS):
            assert 0 <= rsr[a] and rsr[a] + NA_ROWS <= NA_KROWS
            for c in range(NA_KROWS):
                row_ok = rsr[a] <= c < rsr[a] + NA_ROWS
                dr[a, :, c, :] = np.clip(c - off - a + NA_ROWS - 1, 0, 2 * NA_ROWS - 2)
                dc[a, :, c, :] = dcol
                ok[a, :, c, :] = col_ok & row_ok
        shape = (NA_QROWS * w, NA_KROWS * w)
        dr_all.append(dr.reshape(shape)); dc_all.append(dc.reshape(shape)); ok_all.append(ok.reshape(shape))
    dr_all, dc_all, ok_all = np.stack(dr_all), np.stack(dc_all), np.stack(ok_all)
    tab = rpb.astype(F32)[:, dr_all, dc_all]
    tab = jnp.where(ok_all[None], tab, NEG_INF)
    return jnp.asarray(np.array(cfg_of, np.int32)), jnp.transpose(tab, (1, 0, 2, 3))


def _attn_b_kernel(cfg_ref, q_ref, k_ref, v_ref, kc_ref, vc_ref, bias_ref, o_ref, *, n_rows, heads):
    del cfg_ref
    w = GRID_W
    rb = pl.program_id(2)
    rs = jnp.clip(rb * NA_QROWS - NA_ROWS // 2, 0, n_rows - NA_KROWS)
    start = pl.multiple_of(rs * w, w)
    span = NA_KROWS * w
    outs = []
    for h in range(heads):
        q = q_ref[0, h]
        kw = k_ref[0, h, pl.ds(start, span), :]
        vw = v_ref[0, h, pl.ds(start, span), :]
        s_loc = _dot(q, kw, _NT) + bias_ref[0, h]
        s_ctx = _dot(q, kc_ref[0, h], _NT)
        outs.append(_softmax_pv([s_loc, s_ctx], [vw, vc_ref[0, h]]))
    o_ref[0] = jnp.concatenate(outs, axis=-1).astype(BF16)


def _attn_b(q, k, v, kc, vc, rpb):
    b, hq, s, dh = q.shape
    nc = kc.shape[2]
    w = GRID_W
    n_rows = s // w
    assert s % w == 0 and n_rows >= NA_KROWS and n_rows % NA_QROWS == 0
    cfg_of, table = _na_bias_tables(rpb, n_rows)
    hg = NA_HEAD_GROUP
    tq = NA_QROWS * w
    span = NA_KROWS * w
    grid_spec = pltpu.PrefetchScalarGridSpec(
        num_scalar_prefetch=1,
        grid=(b, hq // hg, n_rows // NA_QROWS),
        in_specs=[
            pl.BlockSpec((1, hg, tq, dh), lambda bi, g, r, cfg: (bi, g, r, 0)),
            pl.BlockSpec((1, hg, s, dh), lambda bi, g, r, cfg: (bi, g, 0, 0)),
            pl.BlockSpec((1, hg, s, dh), lambda bi, g, r, cfg: (bi, g, 0, 0)),
            pl.BlockSpec((1, hg, nc, dh), lambda bi, g, r, cfg: (bi, g, 0, 0)),
            pl.BlockSpec((1, hg, nc, dh), lambda bi, g, r, cfg: (bi, g, 0, 0)),
            pl.BlockSpec((1, hg, tq, span), lambda bi, g, r, cfg: (cfg[r], g, 0, 0)),
        ],
        out_specs=pl.BlockSpec((1, tq, hg * dh), lambda bi, g, r, cfg: (bi, r, g)),
    )
    return pl.pallas_call(
        functools.partial(_attn_b_kernel, n_rows=n_rows, heads=hg),
        out_shape=jax.ShapeDtypeStruct((b, s, hq * dh), BF16),
        grid_spec=grid_spec,
        compiler_params=_params(("parallel", "parallel", "arbitrary"), VMEM_LIMIT),
        name="attn_neighbourhood",
    )(cfg_of, q, k, v, kc, vc, table)


def _out_kernel(o_ref, wo_ref, x_ref, g1_ref, g_ref, sh_ref, sc_ref, wr_ref, xn_ref, h_ref, aff_ref):
    y = _dot(o_ref[0], wo_ref[...])
    xn = x_ref[0] + g1_ref[0] * y
    xn_ref[0] = xn
    h = _rms_mod(xn, g_ref[...], sh_ref[0], sc_ref[0])
    h_ref[0] = h.astype(BF16)
    logits = _dot3(wr_ref[...], h, _NT)
    z = jnp.exp(logits - logits.max(axis=0, keepdims=True))
    aff_ref[0] = z / z.sum(axis=0, keepdims=True)


def _out_proj(o, w_o, x, g1, g, shift, scale, w_router):
    bx, n, d = x.shape
    da = o.shape[2]
    e = w_router.shape[1]
    tm = min(512, n)
    bm = g1.shape[0]
    mod_map = (lambda b, t: (b, 0, 0)) if bm == bx else (lambda b, t: (0, 0, 0))
    const2 = lambda b, t: (0, 0)
    mod_spec = pl.BlockSpec((1, 1, d), mod_map)
    return pl.pallas_call(
        _out_kernel,
        out_shape=[jax.ShapeDtypeStruct((bx, n, d), F32),
                   jax.ShapeDtypeStruct((bx, n, d), BF16),
                   jax.ShapeDtypeStruct((bx, e, n), F32)],
        grid=(bx, n // tm),
        in_specs=[
            pl.BlockSpec((1, tm, da), lambda b, t: (b, t, 0)),
            pl.BlockSpec((da, d), const2),
            pl.BlockSpec((1, tm, d), lambda b, t: (b, t, 0)),
            mod_spec,
            pl.BlockSpec((1, d), const2),
            mod_spec,
            mod_spec,
            pl.BlockSpec((e, d), const2),
        ],
        out_specs=[pl.BlockSpec((1, tm, d), lambda b, t: (b, t, 0)),
                   pl.BlockSpec((1, tm, d), lambda b, t: (b, t, 0)),
                   pl.BlockSpec((1, e, tm), lambda b, t: (b, 0, t))],
        compiler_params=_params(("parallel", "parallel"), VMEM_LIMIT),
        name="out_proj_router",
    )(o, w_o.astype(BF16), x, g1, g.reshape(1, d), shift, scale, w_router.T)


def _ffn_kernel(x_ref, gate_ref, wg_ref, wu_ref, wd_ref, y_ref):
    f = pl.program_id(2)
    nb, cap, d = x_ref.shape[1:]
    x = x_ref[0].reshape(nb * cap, d)
    a = _dot(x, wg_ref[0].astype(BF16))
    u = _dot(x, wu_ref[0].astype(BF16))
    h = ((a * jax.nn.sigmoid(a)) * u).astype(BF16)
    y = _dot(h, wd_ref[0].astype(BF16)).reshape(nb, cap, d)

    @pl.when(f == 0)
    def _():
        y_ref[0] = y

    @pl.when(f > 0)
    def _():
        y_ref[0] += y

    @pl.when(f == pl.num_programs(2) - 1)
    def _():
        y_ref[0] *= gate_ref[0]


def _ffn(xg, gate, w_gate, w_up, w_down):
    e, bx, cap, d = xg.shape
    ff = w_gate.shape[2]
    nb = bx if bx * cap <= 2048 else max(1, 2048 // cap)
    assert bx % nb == 0
    tf = min(512, ff)
    return pl.pallas_call(
        _ffn_kernel,
        out_shape=jax.ShapeDtypeStruct((e, bx, cap, d), F32),
        grid=(e, bx // nb, ff // tf),
        in_specs=[
            pl.BlockSpec((1, nb, cap, d), lambda ei, r, f: (ei, r, 0, 0)),
            pl.BlockSpec((1, nb, cap, 1), lambda ei, r, f: (ei, r, 0, 0)),
            pl.BlockSpec((1, d, tf), lambda ei, r, f: (ei, 0, f)),
            pl.BlockSpec((1, d, tf), lambda ei, r, f: (ei, 0, f)),
            pl.BlockSpec((1, tf, d), lambda ei, r, f: (ei, f, 0)),
        ],
        out_specs=pl.BlockSpec((1, nb, cap, d), lambda ei, r, f: (ei, r, 0, 0)),
        compiler_params=_params(("parallel", "parallel", "arbitrary"), VMEM_LIMIT),
        name="moe_ffn",
    )(xg, gate, w_gate, w_up, w_down)


def _ec_moe(h, aff, w_gate, w_up, w_down):
    bx, n, d = h.shape
    e = aff.shape[1]
    cap = EC_CAPACITY * n // e
    gate, idx = lax.top_k(aff, cap)
    idx_t = jnp.transpose(idx, (1, 0, 2))
    gate_t = jnp.transpose(gate, (1, 0, 2))[..., None]
    bidx = jnp.arange(bx)[None, :, None]
    xg = h[bidx, idx_t]
    y = _ffn(xg, gate_t, w_gate, w_up, w_down)
    return jnp.zeros((bx, n, d), F32).at[bidx, idx_t].add(y)


def kernel(x, c, ctx, c_ctx, w_mod, b_mod, norm_mix, norm_ffn, a_w_qkv, a_q_gain, a_k_gain, a_sink, a_w_o,
           b_w_qkv, b_q_gain, b_k_gain, b_rpb, b_w_o, moe_router, moe_w_gate, moe_w_up, moe_w_down):
    b, s, d = x.shape
    depth = w_mod.shape[0]
    pad = (-(b + 1)) % 8
    cvec = jnp.concatenate([c, c_ctx[None], jnp.zeros((pad, d), F32)], axis=0)
    mods = _modulation(cvec, w_mod, b_mod)
    rope_tabs = _rope_tables(s)
    xl, xc = x, ctx
    for i in range(depth):
        last = i == depth - 1
        ml = [m.reshape(b, 1, d) for m in jnp.split(mods[i, :b], 6, axis=-1)]
        mc = [m.reshape(1, 1, d) for m in jnp.split(mods[i, b:b + 1], 6, axis=-1)]
        sh1_l, sc1_l, g1_l, sh2_l, sc2_l, g2_l = ml
        sh1_c, sc1_c, g1_c, sh2_c, sc2_c, g2_c = mc
        j = i // N_MIXERS
        if i % N_MIXERS == 0:
            ql, kl, vl = _project(xl, norm_mix[i], sh1_l, sc1_l, a_w_qkv[j], a_q_gain[j], a_k_gain[j],
                                  A_HEADS, A_KV_HEADS, rope_tabs)
            qc, kc, vc = _project(xc, norm_mix[i], sh1_c, sc1_c, a_w_qkv[j], a_q_gain[j], a_k_gain[j],
                                  A_HEADS, A_KV_HEADS)
            ol = _attn_a(ql, kl, vl, kc, vc, a_sink[j])
            oc = None if last else _attn_ctx(qc, kc, vc, a_sink[j])
            w_o = a_w_o[j]
        else:
            ql, kl, vl = _project(xl, norm_mix[i], sh1_l, sc1_l, b_w_qkv[j], b_q_gain[j], b_k_gain[j],
                                  B_HEADS, B_HEADS)
            qc, kc, vc = _project(xc, norm_mix[i], sh1_c, sc1_c, b_w_qkv[j], b_q_gain[j], b_k_gain[j],
                                  B_HEADS, B_HEADS)
            ol = _attn_b(ql, kl, vl, kc, vc, b_rpb[j])
            oc = None if last else _attn_ctx(qc, kc, vc, None)
            w_o = b_w_o[j]
        xl, hl, aff_l = _out_proj(ol, w_o, xl, g1_l, norm_ffn[i], sh2_l, sc2_l, moe_router[i])
        xl = xl + g2_l * _ec_moe(hl, aff_l, moe_w_gate[i], moe_w_up[i], moe_w_down[i])
        if not last:
            xc, hc, aff_c = _out_proj(oc, w_o, xc, g1_c, norm_ffn[i], sh2_c, sc2_c, moe_router[i])
            xc = xc + g2_c * _ec_moe(hc, aff_c, moe_w_gate[i], moe_w_up[i], moe_w_down[i])
    return xl
```

```python
import functools

import numpy as np
import jax
import jax.numpy as jnp
from jax import lax
from jax.experimental import pallas as pl
from jax.experimental.pallas import tpu as pltpu

F32 = jnp.float32
BF16 = jnp.bfloat16

GRID_W = 64
HEAD_DIM = 64
A_HEADS = 16
A_KV_HEADS = 4
A_WINDOW = 128
A_BLOCK = 128
B_HEADS = 16
NA_ROWS = 8
NA_COLS = 16
N_EXPERTS = 16
EC_CAPACITY = 2
ROPE_BASE = 10000.0
EPS = 1e-6
NEG_INF = -1e30
ATTN_SCALE = HEAD_DIM ** -0.5
N_MIXERS = 2

NA_QROWS = 2
NA_KROWS = NA_ROWS + NA_QROWS
NA_HEAD_GROUP = 8

VMEM_LIMIT = 56 * 1024 * 1024

_NT = (((1,), (1,)), ((), ()))
_NN = (((1,), (0,)), ((), ()))


def _dot(a, b, dims=_NN):
    return lax.dot_general(a, b, dims, preferred_element_type=F32)


def _split(a):
    hi = a.astype(BF16)
    lo = (a - hi.astype(F32)).astype(BF16)
    return hi, lo


def _dot3(a, b, dims=_NN):
    ah, al = _split(a)
    bh, bl = _split(b)
    return _dot(ah, bh, dims) + (_dot(ah, bl, dims) + _dot(al, bh, dims))


def _params(sem, vmem=None):
    return pltpu.CompilerParams(dimension_semantics=sem, vmem_limit_bytes=vmem)


def _mod_kernel(c_ref, w_ref, b_ref, o_ref):
    c = c_ref[...]
    a = c * jax.nn.sigmoid(c)
    o_ref[0] = _dot3(a, w_ref[0]) + b_ref[0]


def _modulation(cvec, w_mod, b_mod):
    depth, d, n = w_mod.shape
    rows = cvec.shape[0]
    tn = 1024
    return pl.pallas_call(
        _mod_kernel,
        out_shape=jax.ShapeDtypeStruct((depth, rows, n), F32),
        grid=(depth, n // tn),
        in_specs=[
            pl.BlockSpec((rows, d), lambda i, j: (0, 0)),
            pl.BlockSpec((1, d, tn), lambda i, j: (i, 0, j)),
            pl.BlockSpec((1, 1, tn), lambda i, j: (i, 0, j)),
        ],
        out_specs=pl.BlockSpec((1, rows, tn), lambda i, j: (i, 0, j)),
        compiler_params=_params(("parallel", "parallel")),
        name="modulation",
    )(cvec, w_mod, b_mod.reshape(depth, 1, n))


def _rms_mod(x, g, shift, scale):
    y = x * lax.rsqrt(jnp.mean(x * x, axis=-1, keepdims=True) + EPS)
    return (y * g) * (1.0 + scale) + shift


def _proj_kernel(*refs, n_q, n_kv, rope):
    if rope:
        x_ref, g_ref, sh_ref, sc_ref, wt_ref, cq_ref, ck_ref, sq_ref, sk_ref, qt_ref, k_ref, vt_ref = refs
    else:
        x_ref, g_ref, sh_ref, sc_ref, wt_ref, cq_ref, ck_ref, qt_ref, k_ref, vt_ref = refs
        sq_ref = sk_ref = None
    dh = HEAD_DIM
    h = _rms_mod(x_ref[0], g_ref[...], sh_ref[0], sc_ref[0]).astype(BF16)
    res = _dot(wt_ref[...], h, _NT)

    def head(j, c_ref, s_ref):
        t = res[j * dh:(j + 1) * dh]
        tn = t * lax.rsqrt(jnp.mean(t * t, axis=0, keepdims=True) + EPS)
        out = tn * c_ref[...]
        if rope:
            q4 = dh // 4
            partner = jnp.concatenate([tn[q4:2 * q4], tn[:q4], tn[3 * q4:], tn[2 * q4:3 * q4]], axis=0)
            out = out + partner * s_ref[...]
        return out

    for j in range(n_q):
        qt_ref[0, j] = head(j, cq_ref, sq_ref).astype(BF16)
    ks = [head(n_q + j, ck_ref, sk_ref) for j in range(n_kv)]
    k_ref[0] = jnp.concatenate(ks, axis=0).T.astype(BF16)
    v0 = (n_q + n_kv) * dh
    for j in range(n_kv):
        vt_ref[0, j] = res[v0 + j * dh:v0 + (j + 1) * dh].astype(BF16)


def _rope_perm():
    d = np.arange(HEAD_DIM)
    return np.where((d % 32) < 16, d + 16, d - 16)


def _rope_tables(n_tok):
    pos = jnp.arange(n_tok)
    row = (pos // GRID_W).astype(F32)
    col = (pos % GRID_W).astype(F32)
    n_freq = HEAD_DIM // 4
    inv = ROPE_BASE ** (-jnp.arange(n_freq, dtype=F32) / n_freq)
    ang_r = row[:, None] * inv
    ang_c = col[:, None] * inv
    cos_t = jnp.concatenate([jnp.cos(ang_r), jnp.cos(ang_r), jnp.cos(ang_c), jnp.cos(ang_c)], axis=-1)
    sin_t = jnp.concatenate([-jnp.sin(ang_r), jnp.sin(ang_r), -jnp.sin(ang_c), jnp.sin(ang_c)], axis=-1)
    return cos_t.T, sin_t.T


def _project(x, g, shift, scale, w, q_gain, k_gain, n_q, n_kv, rope_tabs=None):
    bx, n, d = x.shape
    dh = HEAD_DIM
    rope = rope_tabs is not None
    tm = min(512, n)
    qg = (q_gain.astype(F32) * ATTN_SCALE).reshape(dh, 1)
    kg = k_gain.astype(F32).reshape(dh, 1)
    wt = w.T.astype(BF16)
    n_cols = wt.shape[0]
    bm = shift.shape[0]
    mod_map = (lambda b, t: (b, 0, 0)) if bm == bx else (lambda b, t: (0, 0, 0))
    const2 = lambda b, t: (0, 0)
    tok_map = lambda b, t: (0, t)
    in_specs = [
        pl.BlockSpec((1, tm, d), lambda b, t: (b, t, 0)),
        pl.BlockSpec((1, d), const2),
        pl.BlockSpec((1, 1, d), mod_map),
        pl.BlockSpec((1, 1, d), mod_map),
        pl.BlockSpec((n_cols, d), const2),
    ]
    args = [x, g.reshape(1, d), shift, scale, wt]
    if rope:
        cos_t, sin_t = rope_tabs
        perm = _rope_perm()
        args += [qg * cos_t, kg * cos_t, qg[perm] * sin_t, kg[perm] * sin_t]
        in_specs += [pl.BlockSpec((dh, tm), tok_map)] * 4
    else:
        args += [jnp.broadcast_to(qg, (dh, tm)), jnp.broadcast_to(kg, (dh, tm))]
        in_specs += [pl.BlockSpec((dh, tm), const2)] * 2
    out_shape = [jax.ShapeDtypeStruct((bx, n_q, dh, n), BF16),
                 jax.ShapeDtypeStruct((bx, n, n_kv * dh), BF16),
                 jax.ShapeDtypeStruct((bx, n_kv, dh, n), BF16)]
    out_specs = [pl.BlockSpec((1, n_q, dh, tm), lambda b, t: (b, 0, 0, t)),
                 pl.BlockSpec((1, tm, n_kv * dh), lambda b, t: (b, t, 0)),
                 pl.BlockSpec((1, n_kv, dh, tm), lambda b, t: (b, 0, 0, t))]
    return pl.pallas_call(
        functools.partial(_proj_kernel, n_q=n_q, n_kv=n_kv, rope=rope),
        out_shape=out_shape,
        grid=(bx, n // tm),
        in_specs=in_specs,
        out_specs=out_specs,
        compiler_params=_params(("parallel", "parallel"), VMEM_LIMIT),
        name="qkv_proj_rope" if rope else "qkv_proj",
    )(*args)


def _softmax_pv_t(scores, vts, sinks):
    ms = []
    for parts, sink in zip(scores, sinks):
        m = parts[0].max(axis=0, keepdims=True)
        for s in parts[1:]:
            m = jnp.maximum(m, s.max(axis=0, keepdims=True))
        ms.append(m if sink is None else jnp.maximum(m, sink))
    probs, dens = [], []
    for parts, m, sink in zip(scores, ms, sinks):
        ps = [jnp.exp(s - m) for s in parts]
        den = ps[0].sum(axis=0, keepdims=True)
        for p in ps[1:]:
            den = den + p.sum(axis=0, keepdims=True)
        dens.append(den if sink is None else den + jnp.exp(sink - m))
        probs.append([p.astype(BF16) for p in ps])
    outs = []
    for ps, vt_parts, den in zip(probs, vts, dens):
        acc = _dot(vt_parts[0], ps[0])
        for vt, p in zip(vt_parts[1:], ps[1:]):
            acc = acc + _dot(vt, p)
        outs.append(acc / den)
    return outs


def _attn_a_kernel(qt_ref, k_ref, vt_ref, kc_ref, vct_ref, sink_ref, o_ref, *, seq, n_kv, group):
    blk = A_BLOCK
    span = 3 * blk
    dh = HEAD_DIM
    i = pl.program_id(1)
    start = pl.multiple_of(jnp.clip((i - 1) * blk, 0, seq - span), blk)
    krow = lax.broadcasted_iota(jnp.int32, (span, group * blk), 0)
    qcol = lax.broadcasted_iota(jnp.int32, (span, group * blk), 1)
    rel = (i * blk - start) + (qcol & (blk - 1)) - krow
    valid = jnp.abs(rel) <= A_WINDOW
    scores, vts = [], []
    for h in range(n_kv):
        qt = jnp.concatenate([qt_ref[0, h * group + g] for g in range(group)], axis=1)
        kw = k_ref[0, pl.ds(start, span), h * dh:(h + 1) * dh]
        s_loc = jnp.where(valid, _dot(kw, qt), NEG_INF)
        s_ctx = _dot(kc_ref[0, :, h * dh:(h + 1) * dh], qt)
        scores.append([s_loc, s_ctx])
        vts.append([vt_ref[0, h, :, pl.ds(start, span)], vct_ref[0, h]])
    ots = _softmax_pv_t(scores, vts, [sink_ref[h] for h in range(n_kv)])
    outs = [ot[:, g * blk:(g + 1) * blk] for ot in ots for g in range(group)]
    o_ref[0] = jnp.concatenate(outs, axis=0).T.astype(BF16)


def _sink_rows(sink, n_kv, group, cols):
    s = sink.astype(F32).reshape(n_kv, 1, group, 1)
    return jnp.broadcast_to(s, (n_kv, 1, group, cols)).reshape(n_kv, 1, group * cols)


def _attn_a(qt, k, vt, kc, vct, sink):
    b, hq, dh, s = qt.shape
    n_kv = vt.shape[1]
    group = hq // n_kv
    nc = kc.shape[1]
    blk = A_BLOCK
    assert s % blk == 0 and s >= 3 * blk
    return pl.pallas_call(
        functools.partial(_attn_a_kernel, seq=s, n_kv=n_kv, group=group),
        out_shape=jax.ShapeDtypeStruct((b, s, hq * dh), BF16),
        grid=(b, s // blk),
        in_specs=[
            pl.BlockSpec((1, hq, dh, blk), lambda bi, i: (bi, 0, 0, i)),
            pl.BlockSpec((1, s, n_kv * dh), lambda bi, i: (bi, 0, 0)),
            pl.BlockSpec((1, n_kv, dh, s), lambda bi, i: (bi, 0, 0, 0)),
            pl.BlockSpec((1, nc, n_kv * dh), lambda bi, i: (bi, 0, 0)),
            pl.BlockSpec((1, n_kv, dh, nc), lambda bi, i: (bi, 0, 0, 0)),
            pl.BlockSpec((n_kv, 1, group * blk), lambda bi, i: (0, 0, 0)),
        ],
        out_specs=pl.BlockSpec((1, blk, hq * dh), lambda bi, i: (bi, i, 0)),
        compiler_params=_params(("parallel", "arbitrary"), VMEM_LIMIT),
        name="attn_window",
    )(qt, k, vt, kc, vct, _sink_rows(sink, n_kv, group, blk))


def _attn_ctx_kernel(qt_ref, k_ref, vt_ref, sink_ref, o_ref, *, n_kv, group, use_sink):
    dh = HEAD_DIM
    nq = qt_ref.shape[3]
    scores, vts = [], []
    for h in range(n_kv):
        qt = jnp.concatenate([qt_ref[0, h * group + g] for g in range(group)], axis=1)
        scores.append([_dot(k_ref[0, :, h * dh:(h + 1) * dh], qt)])
        vts.append([vt_ref[0, h]])
    ots = _softmax_pv_t(scores, vts, [sink_ref[h] if use_sink else None for h in range(n_kv)])
    outs = [ot[:, g * nq:(g + 1) * nq] for ot in ots for g in range(group)]
    o_ref[0] = jnp.concatenate(outs, axis=0).T.astype(BF16)


def _attn_ctx(qt, k, vt, sink):
    b, hq, dh, n = qt.shape
    n_kv = vt.shape[1]
    group = hq // n_kv
    use_sink = sink is not None
    sink_r = _sink_rows(sink if use_sink else jnp.zeros((hq,), F32), n_kv, group, n)
    return pl.pallas_call(
        functools.partial(_attn_ctx_kernel, n_kv=n_kv, group=group, use_sink=use_sink),
        out_shape=jax.ShapeDtypeStruct((b, n, hq * dh), BF16),
        grid=(b,),
        in_specs=[
            pl.BlockSpec((1, hq, dh, n), lambda bi: (bi, 0, 0, 0)),
            pl.BlockSpec((1, n, n_kv * dh), lambda bi: (bi, 0, 0)),
            pl.BlockSpec((1, n_kv, dh, n), lambda bi: (bi, 0, 0, 0)),
            pl.BlockSpec((n_kv, 1, group * n), lambda bi: (0, 0, 0)),
        ],
        out_specs=pl.BlockSpec((1, n, hq * dh), lambda bi: (bi, 0, 0)),
        compiler_params=_params(("parallel",)),
        name="attn_ctx",
    )(qt, k, vt, sink_r)


def _na_bias_tables(rpb, n_rows):
    w = GRID_W
    n_blocks = n_rows // NA_QROWS
    half = NA_ROWS // 2
    cfgs, cfg_of = {}, []
    for rb in range(n_blocks):
        r0 = rb * NA_QROWS
        rs = int(np.clip(r0 - half, 0, n_rows - NA_KROWS))
        key = (r0 - rs,) + tuple(int(np.clip(r0 + a - half, 0, n_rows - NA_ROWS)) - rs for a in range(NA_QROWS))
        cfg_of.append(cfgs.setdefault(key, len(cfgs)))
    kc = np.arange(w)[:, None]
    qc = np.arange(w)[None, :]
    wstart = np.clip(qc - NA_COLS // 2, 0, w - NA_COLS)
    col_ok = (kc >= wstart) & (kc < wstart + NA_COLS)
    dcol = np.clip(kc - qc, -(NA_COLS - 1), NA_COLS - 1) + NA_COLS - 1
    onehot = (dcol[None] == np.arange(2 * NA_COLS - 1)[:, None, None]).astype(np.float32)
    tiles = jnp.einsum('hrd,dkq->hrkq', rpb.astype(F32), jnp.asarray(onehot), precision=lax.Precision.HIGHEST)
    tiles = jnp.where(col_ok[None, None], tiles, NEG_INF)
    masked = jnp.full((rpb.shape[0], w, w), NEG_INF, F32)
    tabs = []
    for key in cfgs:
        off, rsr = key[0], key[1:]
        rows = []
        for c in range(NA_KROWS):
            cols = []
            for a in range(NA_QROWS):
                assert 0 <= rsr[a] and rsr[a] + NA_ROWS <= NA_KROWS
                row_ok = rsr[a] <= c < rsr[a] + NA_ROWS
                cols.append(tiles[:, c - off - a + NA_ROWS - 1] if row_ok else masked)
            rows.append(jnp.concatenate(cols, axis=2))
        tabs.append(jnp.concatenate(rows, axis=1))
    return jnp.asarray(np.array(cfg_of, np.int32)), jnp.stack(tabs)


def _attn_b_kernel(cfg_ref, qt_ref, k_ref, vt_ref, kc_ref, vct_ref, bias_ref, o_ref, *, n_rows, heads):
    del cfg_ref
    w = GRID_W
    dh = HEAD_DIM
    rb = pl.program_id(2)
    rs2 = jnp.clip(rb - NA_ROWS // (2 * NA_QROWS), 0, (n_rows - NA_KROWS) // NA_QROWS)
    start = pl.multiple_of(rs2 * (NA_QROWS * w), NA_QROWS * w)
    span = NA_KROWS * w
    scores, vts = [], []
    for h in range(heads):
        qt = qt_ref[0, h]
        kw = k_ref[0, pl.ds(start, span), h * dh:(h + 1) * dh]
        s_loc = _dot(kw, qt) + bias_ref[0, h]
        s_ctx = _dot(kc_ref[0, :, h * dh:(h + 1) * dh], qt)
        scores.append([s_loc, s_ctx])
        vts.append([vt_ref[0, h, :, pl.ds(start, span)], vct_ref[0, h]])
    outs = _softmax_pv_t(scores, vts, [None] * heads)
    o_ref[0] = jnp.concatenate(outs, axis=0).T.astype(BF16)


def _attn_b(qt, k, vt, kc, vct, rpb):
    b, hq, dh, s = qt.shape
    nc = kc.shape[1]
    w = GRID_W
    n_rows = s // w
    assert s % w == 0 and n_rows >= NA_KROWS and n_rows % NA_QROWS == 0
    cfg_of, table = _na_bias_tables(rpb, n_rows)
    hg = NA_HEAD_GROUP
    tq = NA_QROWS * w
    span = NA_KROWS * w
    grid_spec = pltpu.PrefetchScalarGridSpec(
        num_scalar_prefetch=1,
        grid=(b, hq // hg, n_rows // NA_QROWS),
        in_specs=[
            pl.BlockSpec((1, hg, dh, tq), lambda bi, g, r, cfg: (bi, g, 0, r)),
            pl.BlockSpec((1, s, hg * dh), lambda bi, g, r, cfg: (bi, 0, g)),
            pl.BlockSpec((1, hg, dh, s), lambda bi, g, r, cfg: (bi, g, 0, 0)),
            pl.BlockSpec((1, nc, hg * dh), lambda bi, g, r, cfg: (bi, 0, g)),
            pl.BlockSpec((1, hg, dh, nc), lambda bi, g, r, cfg: (bi, g, 0, 0)),
            pl.BlockSpec((1, hg, span, tq), lambda bi, g, r, cfg: (cfg[r], g, 0, 0)),
        ],
        out_specs=pl.BlockSpec((1, tq, hg * dh), lambda bi, g, r, cfg: (bi, r, g)),
    )
    return pl.pallas_call(
        functools.partial(_attn_b_kernel, n_rows=n_rows, heads=hg),
        out_shape=jax.ShapeDtypeStruct((b, s, hq * dh), BF16),
        grid_spec=grid_spec,
        compiler_params=_params(("parallel", "parallel", "arbitrary"), VMEM_LIMIT),
        name="attn_neighbourhood",
    )(cfg_of, qt, k, vt, kc, vct, table)


def _out_kernel(o_ref, wo_ref, x_ref, g1_ref, g_ref, sh_ref, sc_ref, wr_ref, xn_ref, h_ref, aff_ref):
    y = _dot(o_ref[0], wo_ref[...])
    xn = x_ref[0] + g1_ref[0] * y
    xn_ref[0] = xn
    h = _rms_mod(xn, g_ref[...], sh_ref[0], sc_ref[0])
    h_ref[0] = h.astype(BF16)
    logits = _dot3(wr_ref[...], h, _NT)
    z = jnp.exp(logits - logits.max(axis=0, keepdims=True))
    aff_ref[0] = z / z.sum(axis=0, keepdims=True)


def _out_proj(o, w_o, x, g1, g, shift, scale, w_router):
    bx, n, d = x.shape
    da = o.shape[2]
    e = w_router.shape[1]
    tm = min(512, n)
    bm = g1.shape[0]
    mod_map = (lambda b, t: (b, 0, 0)) if bm == bx else (lambda b, t: (0, 0, 0))
    const2 = lambda b, t: (0, 0)
    mod_spec = pl.BlockSpec((1, 1, d), mod_map)
    return pl.pallas_call(
        _out_kernel,
        out_shape=[jax.ShapeDtypeStruct((bx, n, d), F32),
                   jax.ShapeDtypeStruct((bx, n, d), BF16),
                   jax.ShapeDtypeStruct((bx, e, n), F32)],
        grid=(bx, n // tm),
        in_specs=[
            pl.BlockSpec((1, tm, da), lambda b, t: (b, t, 0)),
            pl.BlockSpec((da, d), const2),
            pl.BlockSpec((1, tm, d), lambda b, t: (b, t, 0)),
            mod_spec,
            pl.BlockSpec((1, d), const2),
            mod_spec,
            mod_spec,
            pl.BlockSpec((e, d), const2),
        ],
        out_specs=[pl.BlockSpec((1, tm, d), lambda b, t: (b, t, 0)),
                   pl.BlockSpec((1, tm, d), lambda b, t: (b, t, 0)),
                   pl.BlockSpec((1, e, tm), lambda b, t: (b, 0, t))],
        compiler_params=_params(("parallel", "parallel"), VMEM_LIMIT),
        name="out_proj_router",
    )(o, w_o.astype(BF16), x, g1, g.reshape(1, d), shift, scale, w_router.T)


def _ffn_kernel(x_ref, gate_ref, wg_ref, wu_ref, wd_ref, y_ref):
    f = pl.program_id(2)
    nb, cap, d = x_ref.shape[1:]
    x = x_ref[0].reshape(nb * cap, d)
    a = _dot(x, wg_ref[0].astype(BF16))
    u = _dot(x, wu_ref[0].astype(BF16))
    h = ((a * jax.nn.sigmoid(a)) * u).astype(BF16)
    y = _dot(h, wd_ref[0].astype(BF16)).reshape(nb, cap, d)

    @pl.when(f == 0)
    def _():
        y_ref[0] = y

    @pl.when(f > 0)
    def _():
        y_ref[0] += y

    @pl.when(f == pl.num_programs(2) - 1)
    def _():
        y_ref[0] *= gate_ref[0]


def _ffn(xg, gate, w_gate, w_up, w_down):
    e, bx, cap, d = xg.shape
    ff = w_gate.shape[2]
    nb = bx if bx * cap <= 2048 else max(1, 2048 // cap)
    assert bx % nb == 0
    tf = min(512, ff)
    return pl.pallas_call(
        _ffn_kernel,
        out_shape=jax.ShapeDtypeStruct((e, bx, cap, d), F32),
        grid=(e, bx // nb, ff // tf),
        in_specs=[
            pl.BlockSpec((1, nb, cap, d), lambda ei, r, f: (ei, r, 0, 0)),
            pl.BlockSpec((1, nb, cap, 1), lambda ei, r, f: (ei, r, 0, 0)),
            pl.BlockSpec((1, d, tf), lambda ei, r, f: (ei, 0, f)),
            pl.BlockSpec((1, d, tf), lambda ei, r, f: (ei, 0, f)),
            pl.BlockSpec((1, tf, d), lambda ei, r, f: (ei, f, 0)),
        ],
        out_specs=pl.BlockSpec((1, nb, cap, d), lambda ei, r, f: (ei, r, 0, 0)),
        compiler_params=_params(("parallel", "parallel", "arbitrary"), VMEM_LIMIT),
        name="moe_ffn",
    )(xg, gate, w_gate, w_up, w_down)


def _ec_moe(h, aff, w_gate, w_up, w_down):
    bx, n, d = h.shape
    e = aff.shape[1]
    cap = EC_CAPACITY * n // e
    gate, idx = lax.top_k(aff, cap)
    idx_t = jnp.transpose(idx, (1, 0, 2))
    gate_t = jnp.transpose(gate, (1, 0, 2))[..., None]
    bidx = jnp.arange(bx)[None, :, None]
    xg = h[bidx, idx_t]
    y = _ffn(xg, gate_t, w_gate, w_up, w_down)
    return jnp.zeros((bx, n, d), F32).at[bidx, idx_t].add(y)


def kernel(x, c, ctx, c_ctx, w_mod, b_mod, norm_mix, norm_ffn, a_w_qkv, a_q_gain, a_k_gain, a_sink, a_w_o,
           b_w_qkv, b_q_gain, b_k_gain, b_rpb, b_w_o, moe_router, moe_w_gate, moe_w_up, moe_w_down):
    b, s, d = x.shape
    depth = w_mod.shape[0]
    pad = (-(b + 1)) % 8
    cvec = jnp.concatenate([c, c_ctx[None], jnp.zeros((pad, d), F32)], axis=0)
    mods = _modulation(cvec, w_mod, b_mod)
    rope_tabs = _rope_tables(s)
    xl, xc = x, ctx
    for i in range(depth):
        last = i == depth - 1
        ml = [m.reshape(b, 1, d) for m in jnp.split(mods[i, :b], 6, axis=-1)]
        mc = [m.reshape(1, 1, d) for m in jnp.split(mods[i, b:b + 1], 6, axis=-1)]
        sh1_l, sc1_l, g1_l, sh2_l, sc2_l, g2_l = ml
        sh1_c, sc1_c, g1_c, sh2_c, sc2_c, g2_c = mc
        j = i // N_MIXERS
        if i % N_MIXERS == 0:
            ql, kl, vl = _project(xl, norm_mix[i], sh1_l, sc1_l, a_w_qkv[j], a_q_gain[j], a_k_gain[j],
                                  A_HEADS, A_KV_HEADS, rope_tabs)
            qc, kc, vc = _project(xc, norm_mix[i], sh1_c, sc1_c, a_w_qkv[j], a_q_gain[j], a_k_gain[j],
                                  A_HEADS, A_KV_HEADS)
            ol = _attn_a(ql, kl, vl, kc, vc, a_sink[j])
            oc = None if last else _attn_ctx(qc, kc, vc, a_sink[j])
            w_o = a_w_o[j]
        else:
            ql, kl, vl = _project(xl, norm_mix[i], sh1_l, sc1_l, b_w_qkv[j], b_q_gain[j], b_k_gain[j],
                                  B_HEADS, B_HEADS)
            qc, kc, vc = _project(xc, norm_mix[i], sh1_c, sc1_c, b_w_qkv[j], b_q_gain[j], b_k_gain[j],
                                  B_HEADS, B_HEADS)
            ol = _attn_b(ql, kl, vl, kc, vc, b_rpb[j])
            oc = None if last else _attn_ctx(qc, kc, vc, None)
            w_o = b_w_o[j]
        xl, hl, aff_l = _out_proj(ol, w_o, xl, g1_l, norm_ffn[i], sh2_l, sc2_l, moe_router[i])
        xl = xl + g2_l * _ec_moe(hl, aff_l, moe_w_gate[i], moe_w_up[i], moe_w_down[i])
        if not last:
            xc, hc, aff_c = _out_proj(oc, w_o, xc, g1_c, norm_ffn[i], sh2_c, sc2_c, moe_router[i])
            xc = xc + g2_c * _ec_moe(hc, aff_c, moe_w_gate[i], moe_w_up[i], moe_w_down[i])
    return xl
```

```python
import functools

import numpy as np
import jax
import jax.numpy as jnp
from jax import lax
from jax.experimental import pallas as pl
from jax.experimental.pallas import tpu as pltpu

F32 = jnp.float32
BF16 = jnp.bfloat16

GRID_W = 64
HEAD_DIM = 64
A_HEADS = 16
A_KV_HEADS = 4
A_WINDOW = 128
A_BLOCK = 128
B_HEADS = 16
NA_ROWS = 8
NA_COLS = 16
N_EXPERTS = 16
EC_CAPACITY = 2
ROPE_BASE = 10000.0
EPS = 1e-6
NEG_INF = -1e30
ATTN_SCALE = HEAD_DIM ** -0.5
N_MIXERS = 2

NA_QROWS = 2
NA_KROWS = NA_ROWS + NA_QROWS
NA_HEAD_GROUP = 8

VMEM_LIMIT = 56 * 1024 * 1024

_NT = (((1,), (1,)), ((), ()))
_NN = (((1,), (0,)), ((), ()))


def _dot(a, b, dims=_NN):
    return lax.dot_general(a, b, dims, preferred_element_type=F32)


def _split(a):
    hi = a.astype(BF16)
    lo = (a - hi.astype(F32)).astype(BF16)
    return hi, lo


def _dot3(a, b, dims=_NN):
    ah, al = _split(a)
    bh, bl = _split(b)
    return _dot(ah, bh, dims) + (_dot(ah, bl, dims) + _dot(al, bh, dims))


def _params(sem, vmem=None):
    return pltpu.CompilerParams(dimension_semantics=sem, vmem_limit_bytes=vmem)


def _mod_kernel(c_ref, w_ref, b_ref, o_ref):
    c = c_ref[...]
    a = c * jax.nn.sigmoid(c)
    o_ref[0] = _dot3(a, w_ref[0]) + b_ref[0]


def _modulation(cvec, w_mod, b_mod):
    depth, d, n = w_mod.shape
    rows = cvec.shape[0]
    tn = 1024
    return pl.pallas_call(
        _mod_kernel,
        out_shape=jax.ShapeDtypeStruct((depth, rows, n), F32),
        grid=(depth, n // tn),
        in_specs=[
            pl.BlockSpec((rows, d), lambda i, j: (0, 0)),
            pl.BlockSpec((1, d, tn), lambda i, j: (i, 0, j)),
            pl.BlockSpec((1, 1, tn), lambda i, j: (i, 0, j)),
        ],
        out_specs=pl.BlockSpec((1, rows, tn), lambda i, j: (i, 0, j)),
        compiler_params=_params(("parallel", "parallel")),
        name="modulation",
    )(cvec, w_mod, b_mod.reshape(depth, 1, n))


def _rms_mod(x, g, shift, scale):
    y = x * lax.rsqrt(jnp.mean(x * x, axis=-1, keepdims=True) + EPS)
    return (y * g) * (1.0 + scale) + shift


def _proj_kernel(*refs, n_q, n_kv, rope):
    if rope:
        x_ref, g_ref, sh_ref, sc_ref, wt_ref, cq_ref, ck_ref, sq_ref, sk_ref, qt_ref, k_ref, vt_ref = refs
    else:
        x_ref, g_ref, sh_ref, sc_ref, wt_ref, cq_ref, ck_ref, qt_ref, k_ref, vt_ref = refs
        sq_ref = sk_ref = None
    dh = HEAD_DIM
    h = _rms_mod(x_ref[0], g_ref[...], sh_ref[0], sc_ref[0]).astype(BF16)
    res = _dot(wt_ref[...], h, _NT)

    def head(j, c_ref, s_ref):
        t = res[j * dh:(j + 1) * dh]
        tn = t * lax.rsqrt(jnp.mean(t * t, axis=0, keepdims=True) + EPS)
        out = tn * c_ref[...]
        if rope:
            q4 = dh // 4
            partner = jnp.concatenate([tn[q4:2 * q4], tn[:q4], tn[3 * q4:], tn[2 * q4:3 * q4]], axis=0)
            out = out + partner * s_ref[...]
        return out

    for j in range(n_q):
        qt_ref[0, j] = head(j, cq_ref, sq_ref).astype(BF16)
    ks = [head(n_q + j, ck_ref, sk_ref) for j in range(n_kv)]
    k_ref[0] = jnp.concatenate(ks, axis=0).T.astype(BF16)
    v0 = (n_q + n_kv) * dh
    for j in range(n_kv):
        vt_ref[0, j] = res[v0 + j * dh:v0 + (j + 1) * dh].astype(BF16)


def _rope_perm():
    d = np.arange(HEAD_DIM)
    return np.where((d % 32) < 16, d + 16, d - 16)


def _rope_tables(n_tok):
    pos = jnp.arange(n_tok)
    row = (pos // GRID_W).astype(F32)
    col = (pos % GRID_W).astype(F32)
    n_freq = HEAD_DIM // 4
    inv = ROPE_BASE ** (-jnp.arange(n_freq, dtype=F32) / n_freq)
    ang_r = row[:, None] * inv
    ang_c = col[:, None] * inv
    cos_t = jnp.concatenate([jnp.cos(ang_r), jnp.cos(ang_r), jnp.cos(ang_c), jnp.cos(ang_c)], axis=-1)
    sin_t = jnp.concatenate([-jnp.sin(ang_r), jnp.sin(ang_r), -jnp.sin(ang_c), jnp.sin(ang_c)], axis=-1)
    return cos_t.T, sin_t.T


def _project(x, g, shift, scale, w, q_gain, k_gain, n_q, n_kv, rope_tabs=None):
    bx, n, d = x.shape
    dh = HEAD_DIM
    rope = rope_tabs is not None
    tm = min(512, n)
    qg = (q_gain.astype(F32) * ATTN_SCALE).reshape(dh, 1)
    kg = k_gain.astype(F32).reshape(dh, 1)
    wt = w.T.astype(BF16)
    n_cols = wt.shape[0]
    bm = shift.shape[0]
    mod_map = (lambda b, t: (b, 0, 0)) if bm == bx else (lambda b, t: (0, 0, 0))
    const2 = lambda b, t: (0, 0)
    tok_map = lambda b, t: (0, t)
    in_specs = [
        pl.BlockSpec((1, tm, d), lambda b, t: (b, t, 0)),
        pl.BlockSpec((1, d), const2),
        pl.BlockSpec((1, 1, d), mod_map),
        pl.BlockSpec((1, 1, d), mod_map),
        pl.BlockSpec((n_cols, d), const2),
    ]
    args = [x, g.reshape(1, d), shift, scale, wt]
    if rope:
        cos_t, sin_t = rope_tabs
        perm = _rope_perm()
        args += [qg * cos_t, kg * cos_t, qg[perm] * sin_t, kg[perm] * sin_t]
        in_specs += [pl.BlockSpec((dh, tm), tok_map)] * 4
    else:
        args += [jnp.broadcast_to(qg, (dh, tm)), jnp.broadcast_to(kg, (dh, tm))]
        in_specs += [pl.BlockSpec((dh, tm), const2)] * 2
    out_shape = [jax.ShapeDtypeStruct((bx, n_q, dh, n), BF16),
                 jax.ShapeDtypeStruct((bx, n, n_kv * dh), BF16),
                 jax.ShapeDtypeStruct((bx, n_kv, dh, n), BF16)]
    out_specs = [pl.BlockSpec((1, n_q, dh, tm), lambda b, t: (b, 0, 0, t)),
                 pl.BlockSpec((1, tm, n_kv * dh), lambda b, t: (b, t, 0)),
                 pl.BlockSpec((1, n_kv, dh, tm), lambda b, t: (b, 0, 0, t))]
    return pl.pallas_call(
        functools.partial(_proj_kernel, n_q=n_q, n_kv=n_kv, rope=rope),
        out_shape=out_shape,
        grid=(bx, n // tm),
        in_specs=in_specs,
        out_specs=out_specs,
        compiler_params=_params(("parallel", "parallel"), VMEM_LIMIT),
        name="qkv_proj_rope" if rope else "qkv_proj",
    )(*args)


def _softmax_pv_t(scores, vts, sinks):
    ms = []
    for parts, sink in zip(scores, sinks):
        m = parts[0].max(axis=0, keepdims=True)
        for s in parts[1:]:
            m = jnp.maximum(m, s.max(axis=0, keepdims=True))
        ms.append(m if sink is None else jnp.maximum(m, sink))
    probs, dens = [], []
    for parts, m, sink in zip(scores, ms, sinks):
        ps = [jnp.exp(s - m) for s in parts]
        den = ps[0].sum(axis=0, keepdims=True)
        for p in ps[1:]:
            den = den + p.sum(axis=0, keepdims=True)
        dens.append(den if sink is None else den + jnp.exp(sink - m))
        probs.append([p.astype(BF16) for p in ps])
    outs = []
    for ps, vt_parts, den in zip(probs, vts, dens):
        acc = _dot(vt_parts[0], ps[0])
        for vt, p in zip(vt_parts[1:], ps[1:]):
            acc = acc + _dot(vt, p)
        outs.append(acc / den)
    return outs


def _attn_a_kernel(qt_ref, k_ref, vt_ref, kc_ref, vct_ref, sink_ref, o_ref, *, seq, n_kv, group):
    blk = A_BLOCK
    span = 3 * blk
    dh = HEAD_DIM
    i = pl.program_id(1)
    start = pl.multiple_of(jnp.clip((i - 1) * blk, 0, seq - span), blk)
    krow = lax.broadcasted_iota(jnp.int32, (span, group * blk), 0)
    qcol = lax.broadcasted_iota(jnp.int32, (span, group * blk), 1)
    rel = (i * blk - start) + (qcol & (blk - 1)) - krow
    valid = jnp.abs(rel) <= A_WINDOW
    scores, vts = [], []
    for h in range(n_kv):
        qt = jnp.concatenate([qt_ref[0, h * group + g] for g in range(group)], axis=1)
        kw = k_ref[0, pl.ds(start, span), h * dh:(h + 1) * dh]
        s_loc = jnp.where(valid, _dot(kw, qt), NEG_INF)
        s_ctx = _dot(kc_ref[0, :, h * dh:(h + 1) * dh], qt)
        scores.append([s_loc, s_ctx])
        vts.append([vt_ref[0, h, :, pl.ds(start, span)], vct_ref[0, h]])
    ots = _softmax_pv_t(scores, vts, [sink_ref[h] for h in range(n_kv)])
    outs = [ot[:, g * blk:(g + 1) * blk] for ot in ots for g in range(group)]
    o_ref[0] = jnp.concatenate(outs, axis=0).T.astype(BF16)


def _sink_rows(sink, n_kv, group, cols):
    s = sink.astype(F32).reshape(n_kv, 1, group, 1)
    return jnp.broadcast_to(s, (n_kv, 1, group, cols)).reshape(n_kv, 1, group * cols)


def _attn_a(qt, k, vt, kc, vct, sink):
    b, hq, dh, s = qt.shape
    n_kv = vt.shape[1]
    group = hq // n_kv
    nc = kc.shape[1]
    blk = A_BLOCK
    assert s % blk == 0 and s >= 3 * blk
    return pl.pallas_call(
        functools.partial(_attn_a_kernel, seq=s, n_kv=n_kv, group=group),
        out_shape=jax.ShapeDtypeStruct((b, s, hq * dh), BF16),
        grid=(b, s // blk),
        in_specs=[
            pl.BlockSpec((1, hq, dh, blk), lambda bi, i: (bi, 0, 0, i)),
            pl.BlockSpec((1, s, n_kv * dh), lambda bi, i: (bi, 0, 0)),
            pl.BlockSpec((1, n_kv, dh, s), lambda bi, i: (bi, 0, 0, 0)),
            pl.BlockSpec((1, nc, n_kv * dh), lambda bi, i: (bi, 0, 0)),
            pl.BlockSpec((1, n_kv, dh, nc), lambda bi, i: (bi, 0, 0, 0)),
            pl.BlockSpec((n_kv, 1, group * blk), lambda bi, i: (0, 0, 0)),
        ],
        out_specs=pl.BlockSpec((1, blk, hq * dh), lambda bi, i: (bi, i, 0)),
        compiler_params=_params(("parallel", "arbitrary"), VMEM_LIMIT),
        name="attn_window",
    )(qt, k, vt, kc, vct, _sink_rows(sink, n_kv, group, blk))


def _attn_ctx_kernel(qt_ref, k_ref, vt_ref, sink_ref, o_ref, *, n_kv, group, use_sink):
    dh = HEAD_DIM
    nq = qt_ref.shape[3]
    scores, vts = [], []
    for h in range(n_kv):
        qt = jnp.concatenate([qt_ref[0, h * group + g] for g in range(group)], axis=1)
        scores.append([_dot(k_ref[0, :, h * dh:(h + 1) * dh], qt)])
        vts.append([vt_ref[0, h]])
    ots = _softmax_pv_t(scores, vts, [sink_ref[h] if use_sink else None for h in range(n_kv)])
    outs = [ot[:, g * nq:(g + 1) * nq] for ot in ots for g in range(group)]
    o_ref[0] = jnp.concatenate(outs, axis=0).T.astype(BF16)


def _attn_ctx(qt, k, vt, sink):
    b, hq, dh, n = qt.shape
    n_kv = vt.shape[1]
    group = hq // n_kv
    use_sink = sink is not None
    sink_r = _sink_rows(sink if use_sink else jnp.zeros((hq,), F32), n_kv, group, n)
    return pl.pallas_call(
        functools.partial(_attn_ctx_kernel, n_kv=n_kv, group=group, use_sink=use_sink),
        out_shape=jax.ShapeDtypeStruct((b, n, hq * dh), BF16),
        grid=(b,),
        in_specs=[
            pl.BlockSpec((1, hq, dh, n), lambda bi: (bi, 0, 0, 0)),
            pl.BlockSpec((1, n, n_kv * dh), lambda bi: (bi, 0, 0)),
            pl.BlockSpec((1, n_kv, dh, n), lambda bi: (bi, 0, 0, 0)),
            pl.BlockSpec((n_kv, 1, group * n), lambda bi: (0, 0, 0)),
        ],
        out_specs=pl.BlockSpec((1, n, hq * dh), lambda bi: (bi, 0, 0)),
        compiler_params=_params(("parallel",)),
        name="attn_ctx",
    )(qt, k, vt, sink_r)


def _na_bias_tables(rpb, n_rows):
    w = GRID_W
    n_blocks = n_rows // NA_QROWS
    half = NA_ROWS // 2
    cfgs, cfg_of = {}, []
    for rb in range(n_blocks):
        r0 = rb * NA_QROWS
        rs = int(np.clip(r0 - half, 0, n_rows - NA_KROWS))
        key = (r0 - rs,) + tuple(int(np.clip(r0 + a - half, 0, n_rows - NA_ROWS)) - rs for a in range(NA_QROWS))
        cfg_of.append(cfgs.setdefault(key, len(cfgs)))
    kc = np.arange(w)[:, None]
    qc = np.arange(w)[None, :]
    wstart = np.clip(qc - NA_COLS // 2, 0, w - NA_COLS)
    col_ok = (kc >= wstart) & (kc < wstart + NA_COLS)
    dcol = np.clip(kc - qc, -(NA_COLS - 1), NA_COLS - 1) + NA_COLS - 1
    onehot = (dcol[None] == np.arange(2 * NA_COLS - 1)[:, None, None]).astype(np.float32)
    tiles = jnp.einsum('hrd,dkq->hrkq', rpb.astype(F32), jnp.asarray(onehot), precision=lax.Precision.HIGHEST)
    tiles = jnp.where(col_ok[None, None], tiles, NEG_INF)
    masked = jnp.full((rpb.shape[0], w, w), NEG_INF, F32)
    tabs = []
    for key in cfgs:
        off, rsr = key[0], key[1:]
        rows = []
        for c in range(NA_KROWS):
            cols = []
            for a in range(NA_QROWS):
                assert 0 <= rsr[a] and rsr[a] + NA_ROWS <= NA_KROWS
                row_ok = rsr[a] <= c < rsr[a] + NA_ROWS
                cols.append(tiles[:, c - off - a + NA_ROWS - 1] if row_ok else masked)
            rows.append(jnp.concatenate(cols, axis=2))
        tabs.append(jnp.concatenate(rows, axis=1))
    return jnp.asarray(np.array(cfg_of, np.int32)), jnp.stack(tabs)


def _attn_b_kernel(cfg_ref, qt_ref, k_ref, vt_ref, kc_ref, vct_ref, bias_ref, o_ref, *, n_rows, heads):
    del cfg_ref
    w = GRID_W
    dh = HEAD_DIM
    rb = pl.program_id(2)
    rs2 = jnp.clip(rb - NA_ROWS // (2 * NA_QROWS), 0, (n_rows - NA_KROWS) // NA_QROWS)
    start = pl.multiple_of(rs2 * (NA_QROWS * w), NA_QROWS * w)
    span = NA_KROWS * w
    scores, vts = [], []
    for h in range(heads):
        qt = qt_ref[0, h]
        kw = k_ref[0, pl.ds(start, span), h * dh:(h + 1) * dh]
        s_loc = _dot(kw, qt) + bias_ref[0, h]
        s_ctx = _dot(kc_ref[0, :, h * dh:(h + 1) * dh], qt)
        scores.append([s_loc, s_ctx])
        vts.append([vt_ref[0, h, :, pl.ds(start, span)], vct_ref[0, h]])
    outs = _softmax_pv_t(scores, vts, [None] * heads)
    o_ref[0] = jnp.concatenate(outs, axis=0).T.astype(BF16)


def _attn_b(qt, k, vt, kc, vct, rpb):
    b, hq, dh, s = qt.shape
    nc = kc.shape[1]
    w = GRID_W
    n_rows = s // w
    assert s % w == 0 and n_rows >= NA_KROWS and n_rows % NA_QROWS == 0
    cfg_of, table = _na_bias_tables(rpb, n_rows)
    hg = NA_HEAD_GROUP
    tq = NA_QROWS * w
    span = NA_KROWS * w
    grid_spec = pltpu.PrefetchScalarGridSpec(
        num_scalar_prefetch=1,
        grid=(b, hq // hg, n_rows // NA_QROWS),
        in_specs=[
            pl.BlockSpec((1, hg, dh, tq), lambda bi, g, r, cfg: (bi, g, 0, r)),
            pl.BlockSpec((1, s, hg * dh), lambda bi, g, r, cfg: (bi, 0, g)),
            pl.BlockSpec((1, hg, dh, s), lambda bi, g, r, cfg: (bi, g, 0, 0)),
            pl.BlockSpec((1, nc, hg * dh), lambda bi, g, r, cfg: (bi, 0, g)),
            pl.BlockSpec((1, hg, dh, nc), lambda bi, g, r, cfg: (bi, g, 0, 0)),
            pl.BlockSpec((1, hg, span, tq), lambda bi, g, r, cfg: (cfg[r], g, 0, 0)),
        ],
        out_specs=pl.BlockSpec((1, tq, hg * dh), lambda bi, g, r, cfg: (bi, r, g)),
    )
    return pl.pallas_call(
        functools.partial(_attn_b_kernel, n_rows=n_rows, heads=hg),
        out_shape=jax.ShapeDtypeStruct((b, s, hq * dh), BF16),
        grid_spec=grid_spec,
        compiler_params=_params(("parallel", "parallel", "arbitrary"), VMEM_LIMIT),
        name="attn_neighbourhood",
    )(cfg_of, qt, k, vt, kc, vct, table)


def _out_kernel(o_ref, wo_ref, x_ref, g1_ref, g_ref, sh_ref, sc_ref, wr_ref, xn_ref, h_ref, aff_ref):
    y = _dot(o_ref[0], wo_ref[...])
    xn = x_ref[0] + g1_ref[0] * y
    xn_ref[0] = xn
    h = _rms_mod(xn, g_ref[...], sh_ref[0], sc_ref[0])
    h_ref[0] = h
    logits = _dot3(wr_ref[...], h, _NT)
    z = jnp.exp(logits - logits.max(axis=0, keepdims=True))
    aff_ref[0] = z / z.sum(axis=0, keepdims=True)


def _out_proj(o, w_o, x, g1, g, shift, scale, w_router):
    bx, n, d = x.shape
    da = o.shape[2]
    e = w_router.shape[1]
    tm = min(512, n)
    bm = g1.shape[0]
    mod_map = (lambda b, t: (b, 0, 0)) if bm == bx else (lambda b, t: (0, 0, 0))
    const2 = lambda b, t: (0, 0)
    mod_spec = pl.BlockSpec((1, 1, d), mod_map)
    return pl.pallas_call(
        _out_kernel,
        out_shape=[jax.ShapeDtypeStruct((bx, n, d), F32),
                   jax.ShapeDtypeStruct((bx, n, d), F32),
                   jax.ShapeDtypeStruct((bx, e, n), F32)],
        grid=(bx, n // tm),
        in_specs=[
            pl.BlockSpec((1, tm, da), lambda b, t: (b, t, 0)),
            pl.BlockSpec((da, d), const2),
            pl.BlockSpec((1, tm, d), lambda b, t: (b, t, 0)),
            mod_spec,
            pl.BlockSpec((1, d), const2),
            mod_spec,
            mod_spec,
            pl.BlockSpec((e, d), const2),
        ],
        out_specs=[pl.BlockSpec((1, tm, d), lambda b, t: (b, t, 0)),
                   pl.BlockSpec((1, tm, d), lambda b, t: (b, t, 0)),
                   pl.BlockSpec((1, e, tm), lambda b, t: (b, 0, t))],
        compiler_params=_params(("parallel", "parallel"), VMEM_LIMIT),
        name="out_proj_router",
    )(o, w_o.astype(BF16), x, g1, g.reshape(1, d), shift, scale, w_router.T)


LANES = 128
ROUTE_ROW_ALIGN = 16
ROUTE_BISECT_STEPS = 40
MOE_TILE_ROWS = 1024
MOE_ROW_CHUNK = 512
_TN = (((0,), (0,)), ((), ()))


def _route_kernel(aff_ref, tri_ref, lmat_ref, idx_ref, gate_ref, *, cap):
    n_exp, n_rows, lanes = aff_ref.shape[1:]
    aff = aff_ref[0]

    def reduce_tokens(x, op):
        return op(op(x, axis=1, keepdims=True), axis=2, keepdims=True)

    def count_ge(t):
        return reduce_tokens(jnp.where(aff >= t, 1.0, 0.0), jnp.sum)

    def max_below(t):
        return reduce_tokens(jnp.where(aff < t, aff, -1.0), jnp.max)

    def bisect(_, lohi):
        lo, hi = lohi
        mid = 0.5 * (lo + hi)
        ok = count_ge(mid) >= cap
        return jnp.where(ok, mid, lo), jnp.where(ok, hi, mid)

    top = reduce_tokens(aff, jnp.max)
    _, hi = lax.fori_loop(0, ROUTE_BISECT_STEPS, bisect, (jnp.zeros_like(top), 2.0 * top + 1e-30))

    def short(t):
        return jnp.max(jnp.where(count_ge(t) < cap, 1.0, 0.0)) > 0.5

    def walk(t):
        return jnp.where(count_ge(t) >= cap, t, max_below(t))

    thr = lax.while_loop(short, walk, max_below(hi))
    above = aff > thr
    equal = aff == thr
    need = cap - reduce_tokens(jnp.where(above, 1.0, 0.0), jnp.sum)

    def prefix(x3):
        x2 = x3.reshape(n_exp * n_rows, lanes)
        inrow = _dot(x2.astype(BF16), tri_ref[...])
        tot = inrow[:, lanes - 1:lanes]
        offs = _dot(lmat_ref[...], jnp.broadcast_to(tot, inrow.shape).astype(BF16))
        return inrow, offs, tot

    eq_f = jnp.where(equal, 1.0, 0.0)
    eq_in, eq_off, _ = prefix(eq_f)
    eq_rank = (eq_in + eq_off).reshape(n_exp, n_rows, lanes) - eq_f
    sel_f = jnp.where(above | (equal & (eq_rank < need)), 1.0, 0.0)
    inrow, offs, tot = prefix(sel_f)
    sel2 = sel_f.reshape(n_exp * n_rows, lanes)
    aff2 = aff.reshape(n_exp * n_rows, lanes)
    a_hi = aff2.astype(BF16)
    r1 = aff2 - a_hi.astype(F32)
    a_mid = r1.astype(BF16)
    a_lo = (r1 - a_mid.astype(F32)).astype(BF16)

    slot = lax.broadcasted_iota(jnp.int32, (1, cap), 1).astype(F32)
    row_id = lax.broadcasted_iota(jnp.int32, (n_rows, 1), 0).astype(F32)
    lane_id = lax.broadcasted_iota(jnp.int32, (lanes, 1), 0).astype(F32)
    for ei in range(n_exp):
        rows = slice(ei * n_rows, (ei + 1) * n_rows)
        off_e = offs[rows, 0:1]
        tot_e = tot[rows]
        in_row = jnp.where((off_e <= slot) & (slot < off_e + tot_e), 1.0, 0.0)
        row_base = (in_row * off_e).sum(axis=0, keepdims=True)
        row_of = (in_row * row_id).sum(axis=0, keepdims=True)
        in_row_b = in_row.astype(BF16)
        pick = lambda v: _dot(v, in_row_b, _TN)
        hit = (pick(inrow[rows].astype(BF16)) == slot + 1.0 - row_base) & (pick(sel2[rows].astype(BF16)) > 0.5)
        hit_f = jnp.where(hit, 1.0, 0.0)
        aff_rows = (pick(a_hi[rows]) + pick(a_mid[rows])) + pick(a_lo[rows])
        lane_of = (hit_f * lane_id).sum(axis=0, keepdims=True)
        idx_ref[0, ei:ei + 1, :] = (row_of * lanes + lane_of).astype(jnp.int32)
        gate_ref[0, ei:ei + 1, :] = (hit_f * aff_rows).sum(axis=0, keepdims=True)


def _route(aff, cap):
    bx, e, n = aff.shape
    unit = LANES * ROUTE_ROW_ALIGN
    n_pad = -(-n // unit) * unit
    if n_pad != n:
        aff = jnp.pad(aff, ((0, 0), (0, 0), (0, n_pad - n)), constant_values=-1.0)
    n_rows = n_pad // LANES
    tri = np.triu(np.ones((LANES, LANES), np.float32))
    r = np.arange(e * n_rows)
    lmat = ((r[:, None] // n_rows == r[None, :] // n_rows) & (r[None, :] < r[:, None])).astype(np.float32)
    return pl.pallas_call(
        functools.partial(_route_kernel, cap=cap),
        out_shape=[jax.ShapeDtypeStruct((bx, e, cap), jnp.int32), jax.ShapeDtypeStruct((bx, e, cap), F32)],
        grid=(bx,),
        in_specs=[
            pl.BlockSpec((1, e, n_rows, LANES), lambda b: (b, 0, 0, 0)),
            pl.BlockSpec((LANES, LANES), lambda b: (0, 0)),
            pl.BlockSpec((e * n_rows, e * n_rows), lambda b: (0, 0)),
        ],
        out_specs=[pl.BlockSpec((1, e, cap), lambda b: (b, 0, 0)), pl.BlockSpec((1, e, cap), lambda b: (b, 0, 0))],
        compiler_params=_params(("parallel",)),
        name="moe_route",
    )(aff.reshape(bx, e, n_rows, LANES), jnp.asarray(tri, BF16), jnp.asarray(lmat, BF16))


def _moe_kernel(idx_ref, h_hbm, gate_ref, g2_ref, wg_ref, wu_ref, wd_ref, x_hbm, out_hbm,
                xbuf, abuf, xb, gsem, asem, ssem, *, rows, cap, chunk, n_tiles):
    del x_hbm
    e, p, f = pl.program_id(0), pl.program_id(1), pl.program_id(2)
    n_steps = pl.num_programs(0) * n_tiles
    k = e * n_tiles + p
    slot = lax.rem(k, 2)
    lag = 2 if n_tiles >= 2 else 1

    def row_copies(step, src, dst, dslot, sem, gather):
        se, sp = lax.div(step, n_tiles), lax.rem(step, n_tiles)

        def body(i, carry):
            tok = idx_ref[se, sp * rows + i]
            hbm_row, buf_row = src.at[pl.ds(tok, 1)], dst.at[dslot, pl.ds(i, 1)]
            cp = (pltpu.make_async_copy(hbm_row, buf_row, sem.at[dslot]) if gather
                  else pltpu.make_async_copy(buf_row, hbm_row, sem.at[dslot]))
            cp.start()
            return carry

        lax.fori_loop(0, rows, body, 0, unroll=8)

    def wait_rows(buf, bslot, sem):
        pltpu.make_async_copy(buf.at[bslot], buf.at[bslot], sem.at[bslot]).wait()

    @pl.when(f == 0)
    def _begin():
        @pl.when(k == 0)
        def _():
            row_copies(k, h_hbm, xbuf, slot, gsem, True)

        @pl.when(k >= lag)
        def _():
            wait_rows(abuf, lax.rem(k - lag, 2), ssem)

        row_copies(k, out_hbm, abuf, slot, asem, True)
        wait_rows(xbuf, slot, gsem)

        @pl.when(k + 1 < n_steps)
        def _():
            row_copies(k + 1, h_hbm, xbuf, 1 - slot, gsem, True)

        xb[...] = xbuf[slot].astype(BF16)

    wg = wg_ref[0].astype(BF16)
    wu = wu_ref[0].astype(BF16)
    wd = wd_ref[0].astype(BF16)
    chunks_per_sample = cap // chunk

    def ffn_chunk(c, carry):
        rs = pl.ds(pl.multiple_of(c * chunk, chunk), chunk)
        x = xb[rs]
        a = _dot(x, wg)
        u = _dot(x, wu)
        hcat = ((a * jax.nn.sigmoid(a)) * u).astype(BF16)
        y = _dot(hcat, wd)
        scale = gate_ref[0, rs] * g2_ref[lax.div(c, chunks_per_sample)]

        @pl.when((f == 0) & (c == 0))
        def _():
            wait_rows(abuf, slot, asem)

        abuf[slot, rs] += y * scale
        return carry

    lax.fori_loop(0, rows // chunk, ffn_chunk, 0)

    @pl.when(f == pl.num_programs(2) - 1)
    def _finish():
        row_copies(k, out_hbm, abuf, slot, ssem, False)

        @pl.when(k == n_steps - 1)
        def _():
            if lag == 2:
                @pl.when(k >= 1)
                def _():
                    wait_rows(abuf, 1 - slot, ssem)
            wait_rows(abuf, slot, ssem)


def _ec_moe(x, h, aff, g2, w_gate, w_up, w_down):
    bx, n, d = h.shape
    e = aff.shape[1]
    ff = w_gate.shape[2]
    cap = EC_CAPACITY * n // e
    idx, gate = _route(aff, cap)
    rows_of = (idx + (jnp.arange(bx, dtype=jnp.int32) * n)[:, None, None])
    rows_of = jnp.transpose(rows_of, (1, 0, 2)).reshape(e, bx * cap)
    gate_c = jnp.transpose(gate, (1, 0, 2)).reshape(e, bx * cap, 1)
    g2 = jnp.broadcast_to(g2, (bx, 1, d))
    nb = max(1, min(bx, MOE_TILE_ROWS // cap))
    assert bx % nb == 0
    rows = nb * cap
    n_tiles = bx // nb
    chunk = min(MOE_ROW_CHUNK, cap)
    assert cap % chunk == 0
    tf = min(512, ff)
    grid_spec = pltpu.PrefetchScalarGridSpec(
        num_scalar_prefetch=1,
        grid=(e, n_tiles, ff // tf),
        in_specs=[
            pl.BlockSpec(memory_space=pl.ANY),
            pl.BlockSpec((1, rows, 1), lambda ei, p, f, idx: (ei, p, 0)),
            pl.BlockSpec((nb, 1, d), lambda ei, p, f, idx: (p, 0, 0)),
            pl.BlockSpec((1, d, tf), lambda ei, p, f, idx: (ei, 0, f)),
            pl.BlockSpec((1, d, tf), lambda ei, p, f, idx: (ei, 0, f)),
            pl.BlockSpec((1, tf, d), lambda ei, p, f, idx: (ei, f, 0)),
            pl.BlockSpec(memory_space=pl.ANY),
        ],
        out_specs=pl.BlockSpec(memory_space=pl.ANY),
        scratch_shapes=[
            pltpu.VMEM((2, rows, d), F32),
            pltpu.VMEM((2, rows, d), F32),
            pltpu.VMEM((rows, d), BF16),
            pltpu.SemaphoreType.DMA((2,)),
            pltpu.SemaphoreType.DMA((2,)),
            pltpu.SemaphoreType.DMA((2,)),
        ],
    )
    out = pl.pallas_call(
        functools.partial(_moe_kernel, rows=rows, cap=cap, chunk=chunk, n_tiles=n_tiles),
        out_shape=jax.ShapeDtypeStruct((bx * n, d), F32),
        grid_spec=grid_spec,
        input_output_aliases={7: 0},
        compiler_params=_params(("arbitrary", "arbitrary", "arbitrary"), VMEM_LIMIT),
        name="moe_ffn",
    )(rows_of, h.reshape(bx * n, d), gate_c, g2, w_gate, w_up, w_down, x.reshape(bx * n, d))
    return out.reshape(bx, n, d)


def kernel(x, c, ctx, c_ctx, w_mod, b_mod, norm_mix, norm_ffn, a_w_qkv, a_q_gain, a_k_gain, a_sink, a_w_o,
           b_w_qkv, b_q_gain, b_k_gain, b_rpb, b_w_o, moe_router, moe_w_gate, moe_w_up, moe_w_down):
    b, s, d = x.shape
    depth = w_mod.shape[0]
    pad = (-(b + 1)) % 8
    cvec = jnp.concatenate([c, c_ctx[None], jnp.zeros((pad, d), F32)], axis=0)
    mods = _modulation(cvec, w_mod, b_mod)
    rope_tabs = _rope_tables(s)
    xl, xc = x, ctx
    for i in range(depth):
        last = i == depth - 1
        ml = [m.reshape(b, 1, d) for m in jnp.split(mods[i, :b], 6, axis=-1)]
        mc = [m.reshape(1, 1, d) for m in jnp.split(mods[i, b:b + 1], 6, axis=-1)]
        sh1_l, sc1_l, g1_l, sh2_l, sc2_l, g2_l = ml
        sh1_c, sc1_c, g1_c, sh2_c, sc2_c, g2_c = mc
        j = i // N_MIXERS
        if i % N_MIXERS == 0:
            ql, kl, vl = _project(xl, norm_mix[i], sh1_l, sc1_l, a_w_qkv[j], a_q_gain[j], a_k_gain[j],
                                  A_HEADS, A_KV_HEADS, rope_tabs)
            qc, kc, vc = _project(xc, norm_mix[i], sh1_c, sc1_c, a_w_qkv[j], a_q_gain[j], a_k_gain[j],
                                  A_HEADS, A_KV_HEADS)
            ol = _attn_a(ql, kl, vl, kc, vc, a_sink[j])
            oc = None if last else _attn_ctx(qc, kc, vc, a_sink[j])
            w_o = a_w_o[j]
        else:
            ql, kl, vl = _project(xl, norm_mix[i], sh1_l, sc1_l, b_w_qkv[j], b_q_gain[j], b_k_gain[j],
                                  B_HEADS, B_HEADS)
            qc, kc, vc = _project(xc, norm_mix[i], sh1_c, sc1_c, b_w_qkv[j], b_q_gain[j], b_k_gain[j],
                                  B_HEADS, B_HEADS)
            ol = _attn_b(ql, kl, vl, kc, vc, b_rpb[j])
            oc = None if last else _attn_ctx(qc, kc, vc, None)
            w_o = b_w_o[j]
        xl, hl, aff_l = _out_proj(ol, w_o, xl, g1_l, norm_ffn[i], sh2_l, sc2_l, moe_router[i])
        xl = _ec_moe(xl, hl, aff_l, g2_l, moe_w_gate[i], moe_w_up[i], moe_w_down[i])
        if not last:
            xc, hc, aff_c = _out_proj(oc, w_o, xc, g1_c, norm_ffn[i], sh2_c, sc2_c, moe_router[i])
            xc = _ec_moe(xc, hc, aff_c, g2_c, moe_w_gate[i], moe_w_up[i], moe_w_down[i])
    return xl
```

```python
import functools

import numpy as np
import jax
import jax.numpy as jnp
from jax import lax
from jax.experimental import pallas as pl
from jax.experimental.pallas import tpu as pltpu

F32 = jnp.float32
BF16 = jnp.bfloat16

GRID_W = 64
HEAD_DIM = 64
A_HEADS = 16
A_KV_HEADS = 4
A_WINDOW = 128
A_BLOCK = 128
B_HEADS = 16
NA_ROWS = 8
NA_COLS = 16
N_EXPERTS = 16
EC_CAPACITY = 2
ROPE_BASE = 10000.0
EPS = 1e-6
NEG_INF = -1e30
ATTN_SCALE = HEAD_DIM ** -0.5
N_MIXERS = 2

NA_QROWS = 2
NA_KROWS = NA_ROWS + NA_QROWS
NA_HEAD_GROUP = 8

LANES = 128
VMEM_LIMIT = 56 * 1024 * 1024

_NT = (((1,), (1,)), ((), ()))
_NN = (((1,), (0,)), ((), ()))


def _dot(a, b, dims=_NN):
    return lax.dot_general(a, b, dims, preferred_element_type=F32)


def _split(a):
    hi = a.astype(BF16)
    lo = (a - hi.astype(F32)).astype(BF16)
    return hi, lo


def _dot3(a, b, dims=_NN):
    ah, al = _split(a)
    bh, bl = _split(b)
    return _dot(ah, bh, dims) + (_dot(ah, bl, dims) + _dot(al, bh, dims))


def _params(sem, vmem=None):
    return pltpu.CompilerParams(dimension_semantics=sem, vmem_limit_bytes=vmem)


def _mod_kernel(c_ref, w_ref, b_ref, o_ref):
    c = c_ref[...]
    a = c * jax.nn.sigmoid(c)
    o_ref[0] = _dot3(a, w_ref[0]) + b_ref[0]


def _modulation(cvec, w_mod, b_mod):
    depth, d, n = w_mod.shape
    rows = cvec.shape[0]
    tn = 1024
    return pl.pallas_call(
        _mod_kernel,
        out_shape=jax.ShapeDtypeStruct((depth, rows, n), F32),
        grid=(depth, n // tn),
        in_specs=[
            pl.BlockSpec((rows, d), lambda i, j: (0, 0)),
            pl.BlockSpec((1, d, tn), lambda i, j: (i, 0, j)),
            pl.BlockSpec((1, 1, tn), lambda i, j: (i, 0, j)),
        ],
        out_specs=pl.BlockSpec((1, rows, tn), lambda i, j: (i, 0, j)),
        compiler_params=_params(("parallel", "parallel")),
        name="modulation",
    )(cvec, w_mod, b_mod.reshape(depth, 1, n))


def _rms_mod(x, g, shift, scale):
    y = x * lax.rsqrt(jnp.mean(x * x, axis=-1, keepdims=True) + EPS)
    return (y * g) * (1.0 + scale) + shift


def _load_tokens(ref, tm):
    chunks = ref.shape[0] // tm
    return jnp.concatenate([ref[pl.ds(j, tm, stride=chunks), :] for j in range(chunks)], axis=1)


def _store_tokens(ref, val):
    tm, d = val.shape
    chunks = d // LANES
    for j in range(chunks):
        ref[pl.ds(j, tm, stride=chunks), :] = val[:, j * LANES:(j + 1) * LANES]


def _token_spec(tm, d, n_tiles, tiled):
    if tiled:
        return pl.BlockSpec((tm * (d // LANES), LANES), lambda b, t: (b * n_tiles + t, 0))
    return pl.BlockSpec((1, tm, d), lambda b, t: (b, t, 0))


def _proj_kernel(*refs, n_q, n_kv, rope, tiled):
    if rope:
        x_ref, g_ref, sh_ref, sc_ref, wt_ref, cq_ref, ck_ref, sq_ref, sk_ref, qt_ref, k_ref, vt_ref = refs
    else:
        x_ref, g_ref, sh_ref, sc_ref, wt_ref, cq_ref, ck_ref, qt_ref, k_ref, vt_ref = refs
        sq_ref = sk_ref = None
    dh = HEAD_DIM
    x = _load_tokens(x_ref, qt_ref.shape[3]) if tiled else x_ref[0]
    h = _rms_mod(x, g_ref[...], sh_ref[0], sc_ref[0]).astype(BF16)
    res = _dot(wt_ref[...], h, _NT)

    def head(j, c_ref, s_ref):
        t = res[j * dh:(j + 1) * dh]
        tn = t * lax.rsqrt(jnp.mean(t * t, axis=0, keepdims=True) + EPS)
        out = tn * c_ref[...]
        if rope:
            q4 = dh // 4
            partner = jnp.concatenate([tn[q4:2 * q4], tn[:q4], tn[3 * q4:], tn[2 * q4:3 * q4]], axis=0)
            out = out + partner * s_ref[...]
        return out

    for j in range(n_q):
        qt_ref[0, j] = head(j, cq_ref, sq_ref).astype(BF16)
    ks = [head(n_q + j, ck_ref, sk_ref) for j in range(n_kv)]
    k_ref[0] = jnp.concatenate(ks, axis=0).T.astype(BF16)
    v0 = (n_q + n_kv) * dh
    for j in range(n_kv):
        vt_ref[0, j] = res[v0 + j * dh:v0 + (j + 1) * dh].astype(BF16)


def _rope_perm():
    d = np.arange(HEAD_DIM)
    return np.where((d % 32) < 16, d + 16, d - 16)


def _rope_tables(n_tok):
    pos = jnp.arange(n_tok)
    row = (pos // GRID_W).astype(F32)
    col = (pos % GRID_W).astype(F32)
    n_freq = HEAD_DIM // 4
    inv = ROPE_BASE ** (-jnp.arange(n_freq, dtype=F32) / n_freq)
    ang_r = row[:, None] * inv
    ang_c = col[:, None] * inv
    cos_t = jnp.concatenate([jnp.cos(ang_r), jnp.cos(ang_r), jnp.cos(ang_c), jnp.cos(ang_c)], axis=-1)
    sin_t = jnp.concatenate([-jnp.sin(ang_r), jnp.sin(ang_r), -jnp.sin(ang_c), jnp.sin(ang_c)], axis=-1)
    return cos_t.T, sin_t.T


def _project(x, g, shift, scale, w, q_gain, k_gain, n_q, n_kv, rope_tabs=None, dims=None):
    tiled = dims is not None
    bx, n, d = dims if tiled else x.shape
    dh = HEAD_DIM
    rope = rope_tabs is not None
    tm = min(512, n)
    qg = (q_gain.astype(F32) * ATTN_SCALE).reshape(dh, 1)
    kg = k_gain.astype(F32).reshape(dh, 1)
    wt = w.T.astype(BF16)
    n_cols = wt.shape[0]
    bm = shift.shape[0]
    mod_map = (lambda b, t: (b, 0, 0)) if bm == bx else (lambda b, t: (0, 0, 0))
    const2 = lambda b, t: (0, 0)
    tok_map = lambda b, t: (0, t)
    in_specs = [
        _token_spec(tm, d, n // tm, tiled),
        pl.BlockSpec((1, d), const2),
        pl.BlockSpec((1, 1, d), mod_map),
        pl.BlockSpec((1, 1, d), mod_map),
        pl.BlockSpec((n_cols, d), const2),
    ]
    args = [x, g.reshape(1, d), shift, scale, wt]
    if rope:
        cos_t, sin_t = rope_tabs
        perm = _rope_perm()
        args += [qg * cos_t, kg * cos_t, qg[perm] * sin_t, kg[perm] * sin_t]
        in_specs += [pl.BlockSpec((dh, tm), tok_map)] * 4
    else:
        args += [jnp.broadcast_to(qg, (dh, tm)), jnp.broadcast_to(kg, (dh, tm))]
        in_specs += [pl.BlockSpec((dh, tm), const2)] * 2
    out_shape = [jax.ShapeDtypeStruct((bx, n_q, dh, n), BF16),
                 jax.ShapeDtypeStruct((bx, n, n_kv * dh), BF16),
                 jax.ShapeDtypeStruct((bx, n_kv, dh, n), BF16)]
    out_specs = [pl.BlockSpec((1, n_q, dh, tm), lambda b, t: (b, 0, 0, t)),
                 pl.BlockSpec((1, tm, n_kv * dh), lambda b, t: (b, t, 0)),
                 pl.BlockSpec((1, n_kv, dh, tm), lambda b, t: (b, 0, 0, t))]
    return pl.pallas_call(
        functools.partial(_proj_kernel, n_q=n_q, n_kv=n_kv, rope=rope, tiled=tiled),
        out_shape=out_shape,
        grid=(bx, n // tm),
        in_specs=in_specs,
        out_specs=out_specs,
        compiler_params=_params(("parallel", "parallel"), VMEM_LIMIT),
        name="qkv_proj_rope" if rope else "qkv_proj",
    )(*args)


def _softmax_pv_t(scores, vts, sinks):
    ms = []
    for parts, sink in zip(scores, sinks):
        m = parts[0].max(axis=0, keepdims=True)
        for s in parts[1:]:
            m = jnp.maximum(m, s.max(axis=0, keepdims=True))
        ms.append(m if sink is None else jnp.maximum(m, sink))
    probs, dens = [], []
    for parts, m, sink in zip(scores, ms, sinks):
        ps = [jnp.exp(s - m) for s in parts]
        den = ps[0].sum(axis=0, keepdims=True)
        for p in ps[1:]:
            den = den + p.sum(axis=0, keepdims=True)
        dens.append(den if sink is None else den + jnp.exp(sink - m))
        probs.append([p.astype(BF16) for p in ps])
    outs = []
    for ps, vt_parts, den in zip(probs, vts, dens):
        acc = _dot(vt_parts[0], ps[0])
        for vt, p in zip(vt_parts[1:], ps[1:]):
            acc = acc + _dot(vt, p)
        outs.append(acc / den)
    return outs


def _attn_a_kernel(qt_ref, k_ref, vt_ref, kc_ref, vct_ref, sink_ref, o_ref, *, seq, n_kv, group):
    blk = A_BLOCK
    span = 3 * blk
    dh = HEAD_DIM
    i = pl.program_id(1)
    start = pl.multiple_of(jnp.clip((i - 1) * blk, 0, seq - span), blk)
    krow = lax.broadcasted_iota(jnp.int32, (span, group * blk), 0)
    qcol = lax.broadcasted_iota(jnp.int32, (span, group * blk), 1)
    rel = (i * blk - start) + (qcol & (blk - 1)) - krow
    valid = jnp.abs(rel) <= A_WINDOW
    scores, vts = [], []
    for h in range(n_kv):
        qt = jnp.concatenate([qt_ref[0, h * group + g] for g in range(group)], axis=1)
        kw = k_ref[0, pl.ds(start, span), h * dh:(h + 1) * dh]
        s_loc = jnp.where(valid, _dot(kw, qt), NEG_INF)
        s_ctx = _dot(kc_ref[0, :, h * dh:(h + 1) * dh], qt)
        scores.append([s_loc, s_ctx])
        vts.append([vt_ref[0, h, :, pl.ds(start, span)], vct_ref[0, h]])
    ots = _softmax_pv_t(scores, vts, [sink_ref[h] for h in range(n_kv)])
    outs = [ot[:, g * blk:(g + 1) * blk] for ot in ots for g in range(group)]
    o_ref[0] = jnp.concatenate(outs, axis=0).T.astype(BF16)


def _sink_rows(sink, n_kv, group, cols):
    s = sink.astype(F32).reshape(n_kv, 1, group, 1)
    return jnp.broadcast_to(s, (n_kv, 1, group, cols)).reshape(n_kv, 1, group * cols)


def _attn_a(qt, k, vt, kc, vct, sink):
    b, hq, dh, s = qt.shape
    n_kv = vt.shape[1]
    group = hq // n_kv
    nc = kc.shape[1]
    blk = A_BLOCK
    assert s % blk == 0 and s >= 3 * blk
    return pl.pallas_call(
        functools.partial(_attn_a_kernel, seq=s, n_kv=n_kv, group=group),
        out_shape=jax.ShapeDtypeStruct((b, s, hq * dh), BF16),
        grid=(b, s // blk),
        in_specs=[
            pl.BlockSpec((1, hq, dh, blk), lambda bi, i: (bi, 0, 0, i)),
            pl.BlockSpec((1, s, n_kv * dh), lambda bi, i: (bi, 0, 0)),
            pl.BlockSpec((1, n_kv, dh, s), lambda bi, i: (bi, 0, 0, 0)),
            pl.BlockSpec((1, nc, n_kv * dh), lambda bi, i: (bi, 0, 0)),
            pl.BlockSpec((1, n_kv, dh, nc), lambda bi, i: (bi, 0, 0, 0)),
            pl.BlockSpec((n_kv, 1, group * blk), lambda bi, i: (0, 0, 0)),
        ],
        out_specs=pl.BlockSpec((1, blk, hq * dh), lambda bi, i: (bi, i, 0)),
        compiler_params=_params(("parallel", "arbitrary"), VMEM_LIMIT),
        name="attn_window",
    )(qt, k, vt, kc, vct, _sink_rows(sink, n_kv, group, blk))


def _attn_ctx_kernel(qt_ref, k_ref, vt_ref, sink_ref, o_ref, *, n_kv, group, use_sink):
    dh = HEAD_DIM
    nq = qt_ref.shape[3]
    scores, vts = [], []
    for h in range(n_kv):
        qt = jnp.concatenate([qt_ref[0, h * group + g] for g in range(group)], axis=1)
        scores.append([_dot(k_ref[0, :, h * dh:(h + 1) * dh], qt)])
        vts.append([vt_ref[0, h]])
    ots = _softmax_pv_t(scores, vts, [sink_ref[h] if use_sink else None for h in range(n_kv)])
    outs = [ot[:, g * nq:(g + 1) * nq] for ot in ots for g in range(group)]
    o_ref[0] = jnp.concatenate(outs, axis=0).T.astype(BF16)


def _attn_ctx(qt, k, vt, sink):
    b, hq, dh, n = qt.shape
    n_kv = vt.shape[1]
    group = hq // n_kv
    use_sink = sink is not None
    sink_r = _sink_rows(sink if use_sink else jnp.zeros((hq,), F32), n_kv, group, n)
    return pl.pallas_call(
        functools.partial(_attn_ctx_kernel, n_kv=n_kv, group=group, use_sink=use_sink),
        out_shape=jax.ShapeDtypeStruct((b, n, hq * dh), BF16),
        grid=(b,),
        in_specs=[
            pl.BlockSpec((1, hq, dh, n), lambda bi: (bi, 0, 0, 0)),
            pl.BlockSpec((1, n, n_kv * dh), lambda bi: (bi, 0, 0)),
            pl.BlockSpec((1, n_kv, dh, n), lambda bi: (bi, 0, 0, 0)),
            pl.BlockSpec((n_kv, 1, group * n), lambda bi: (0, 0, 0)),
        ],
        out_specs=pl.BlockSpec((1, n, hq * dh), lambda bi: (bi, 0, 0)),
        compiler_params=_params(("parallel",)),
        name="attn_ctx",
    )(qt, k, vt, sink_r)


def _na_bias_tables(rpb, n_rows):
    w = GRID_W
    n_blocks = n_rows // NA_QROWS
    half = NA_ROWS // 2
    cfgs, cfg_of = {}, []
    for rb in range(n_blocks):
        r0 = rb * NA_QROWS
        rs = int(np.clip(r0 - half, 0, n_rows - NA_KROWS))
        key = (r0 - rs,) + tuple(int(np.clip(r0 + a - half, 0, n_rows - NA_ROWS)) - rs for a in range(NA_QROWS))
        cfg_of.append(cfgs.setdefault(key, len(cfgs)))
    kc = np.arange(w)[:, None]
    qc = np.arange(w)[None, :]
    wstart = np.clip(qc - NA_COLS // 2, 0, w - NA_COLS)
    col_ok = (kc >= wstart) & (kc < wstart + NA_COLS)
    dcol = np.clip(kc - qc, -(NA_COLS - 1), NA_COLS - 1) + NA_COLS - 1
    onehot = (dcol[None] == np.arange(2 * NA_COLS - 1)[:, None, None]).astype(np.float32)
    tiles = jnp.einsum('hrd,dkq->hrkq', rpb.astype(F32), jnp.asarray(onehot), precision=lax.Precision.HIGHEST)
    tiles = jnp.where(col_ok[None, None], tiles, NEG_INF)
    masked = jnp.full((rpb.shape[0], w, w), NEG_INF, F32)
    tabs = []
    for key in cfgs:
        off, rsr = key[0], key[1:]
        rows = []
        for c in range(NA_KROWS):
            cols = []
            for a in range(NA_QROWS):
                assert 0 <= rsr[a] and rsr[a] + NA_ROWS <= NA_KROWS
                row_ok = rsr[a] <= c < rsr[a] + NA_ROWS
                cols.append(tiles[:, c - off - a + NA_ROWS - 1] if row_ok else masked)
            rows.append(jnp.concatenate(cols, axis=2))
        tabs.append(jnp.concatenate(rows, axis=1))
    return jnp.asarray(np.array(cfg_of, np.int32)), jnp.stack(tabs)


def _attn_b_kernel(cfg_ref, qt_ref, k_ref, vt_ref, kc_ref, vct_ref, bias_ref, o_ref, *, n_rows, heads):
    del cfg_ref
    w = GRID_W
    dh = HEAD_DIM
    rb = pl.program_id(2)
    rs2 = jnp.clip(rb - NA_ROWS // (2 * NA_QROWS), 0, (n_rows - NA_KROWS) // NA_QROWS)
    start = pl.multiple_of(rs2 * (NA_QROWS * w), NA_QROWS * w)
    span = NA_KROWS * w
    scores, vts = [], []
    for h in range(heads):
        qt = qt_ref[0, h]
        kw = k_ref[0, pl.ds(start, span), h * dh:(h + 1) * dh]
        s_loc = _dot(kw, qt) + bias_ref[0, h]
        s_ctx = _dot(kc_ref[0, :, h * dh:(h + 1) * dh], qt)
        scores.append([s_loc, s_ctx])
        vts.append([vt_ref[0, h, :, pl.ds(start, span)], vct_ref[0, h]])
    outs = _softmax_pv_t(scores, vts, [None] * heads)
    o_ref[0] = jnp.concatenate(outs, axis=0).T.astype(BF16)


def _attn_b(qt, k, vt, kc, vct, rpb):
    b, hq, dh, s = qt.shape
    nc = kc.shape[1]
    w = GRID_W
    n_rows = s // w
    assert s % w == 0 and n_rows >= NA_KROWS and n_rows % NA_QROWS == 0
    cfg_of, table = _na_bias_tables(rpb, n_rows)
    hg = NA_HEAD_GROUP
    tq = NA_QROWS * w
    span = NA_KROWS * w
    grid_spec = pltpu.PrefetchScalarGridSpec(
        num_scalar_prefetch=1,
        grid=(b, hq // hg, n_rows // NA_QROWS),
        in_specs=[
            pl.BlockSpec((1, hg, dh, tq), lambda bi, g, r, cfg: (bi, g, 0, r)),
            pl.BlockSpec((1, s, hg * dh), lambda bi, g, r, cfg: (bi, 0, g)),
            pl.BlockSpec((1, hg, dh, s), lambda bi, g, r, cfg: (bi, g, 0, 0)),
            pl.BlockSpec((1, nc, hg * dh), lambda bi, g, r, cfg: (bi, 0, g)),
            pl.BlockSpec((1, hg, dh, nc), lambda bi, g, r, cfg: (bi, g, 0, 0)),
            pl.BlockSpec((1, hg, span, tq), lambda bi, g, r, cfg: (cfg[r], g, 0, 0)),
        ],
        out_specs=pl.BlockSpec((1, tq, hg * dh), lambda bi, g, r, cfg: (bi, r, g)),
    )
    return pl.pallas_call(
        functools.partial(_attn_b_kernel, n_rows=n_rows, heads=hg),
        out_shape=jax.ShapeDtypeStruct((b, s, hq * dh), BF16),
        grid_spec=grid_spec,
        compiler_params=_params(("parallel", "parallel", "arbitrary"), VMEM_LIMIT),
        name="attn_neighbourhood",
    )(cfg_of, qt, k, vt, kc, vct, table)


def _out_kernel(o_ref, wo_ref, x_ref, g1_ref, g_ref, sh_ref, sc_ref, wr_ref, xn_ref, h_ref, aff_ref, *, tiled):
    y = _dot(o_ref[0], wo_ref[...])
    x = _load_tokens(x_ref, o_ref.shape[1]) if tiled else x_ref[0]
    xn = x + g1_ref[0] * y
    _store_tokens(xn_ref, xn)
    h = _rms_mod(xn, g_ref[...], sh_ref[0], sc_ref[0])
    _store_tokens(h_ref, h)
    logits = _dot3(wr_ref[...], h, _NT)
    z = jnp.exp(logits - logits.max(axis=0, keepdims=True))
    aff_ref[0] = z / z.sum(axis=0, keepdims=True)


def _out_proj(o, w_o, x, g1, g, shift, scale, w_router, dims=None):
    tiled = dims is not None
    bx, n, d = dims if tiled else x.shape
    assert d % (8 * LANES) == 0
    da = o.shape[2]
    e = w_router.shape[1]
    tm = min(512, n)
    bm = g1.shape[0]
    mod_map = (lambda b, t: (b, 0, 0)) if bm == bx else (lambda b, t: (0, 0, 0))
    const2 = lambda b, t: (0, 0)
    mod_spec = pl.BlockSpec((1, 1, d), mod_map)
    tiled_shape = jax.ShapeDtypeStruct((bx * n * (d // LANES), LANES), F32)
    return pl.pallas_call(
        functools.partial(_out_kernel, tiled=tiled),
        out_shape=[tiled_shape, tiled_shape, jax.ShapeDtypeStruct((bx, e, n), F32)],
        grid=(bx, n // tm),
        in_specs=[
            pl.BlockSpec((1, tm, da), lambda b, t: (b, t, 0)),
            pl.BlockSpec((da, d), const2),
            _token_spec(tm, d, n // tm, tiled),
            mod_spec,
            pl.BlockSpec((1, d), const2),
            mod_spec,
            mod_spec,
            pl.BlockSpec((e, d), const2),
        ],
        out_specs=[_token_spec(tm, d, n // tm, True),
                   _token_spec(tm, d, n // tm, True),
                   pl.BlockSpec((1, e, tm), lambda b, t: (b, 0, t))],
        compiler_params=_params(("parallel", "parallel"), VMEM_LIMIT),
        name="out_proj_router",
    )(o, w_o.astype(BF16), x, g1, g.reshape(1, d), shift, scale, w_router.T)


def _untile_kernel(x_ref, o_ref):
    o_ref[0] = _load_tokens(x_ref, o_ref.shape[1])


def _untile(x, dims):
    bx, n, d = dims
    tm = min(512, n)
    return pl.pallas_call(
        _untile_kernel,
        out_shape=jax.ShapeDtypeStruct(dims, F32),
        grid=(bx, n // tm),
        in_specs=[_token_spec(tm, d, n // tm, True)],
        out_specs=_token_spec(tm, d, n // tm, False),
        compiler_params=_params(("parallel", "parallel")),
        name="untile_tokens",
    )(x)


ROUTE_ROW_ALIGN = 16
ROUTE_BISECT_STEPS = 40
MOE_TILE_ROWS = 1024
MOE_ROW_CHUNK = 512
_TN = (((0,), (0,)), ((), ()))


def _route_kernel(aff_ref, tri_ref, lmat_ref, idx_ref, gate_ref, *, cap):
    n_exp, n_rows, lanes = aff_ref.shape[1:]
    aff = aff_ref[0]

    def reduce_tokens(x, op):
        return op(op(x, axis=1, keepdims=True), axis=2, keepdims=True)

    def count_ge(t):
        return reduce_tokens(jnp.where(aff >= t, 1.0, 0.0), jnp.sum)

    def max_below(t):
        return reduce_tokens(jnp.where(aff < t, aff, -1.0), jnp.max)

    def bisect(_, lohi):
        lo, hi = lohi
        mid = 0.5 * (lo + hi)
        ok = count_ge(mid) >= cap
        return jnp.where(ok, mid, lo), jnp.where(ok, hi, mid)

    top = reduce_tokens(aff, jnp.max)
    _, hi = lax.fori_loop(0, ROUTE_BISECT_STEPS, bisect, (jnp.zeros_like(top), 2.0 * top + 1e-30))

    def short(t):
        return jnp.max(jnp.where(count_ge(t) < cap, 1.0, 0.0)) > 0.5

    def walk(t):
        return jnp.where(count_ge(t) >= cap, t, max_below(t))

    thr = lax.while_loop(short, walk, max_below(hi))
    above = aff > thr
    equal = aff == thr
    need = cap - reduce_tokens(jnp.where(above, 1.0, 0.0), jnp.sum)

    def prefix(x3):
        x2 = x3.reshape(n_exp * n_rows, lanes)
        inrow = _dot(x2.astype(BF16), tri_ref[...])
        tot = inrow[:, lanes - 1:lanes]
        offs = _dot(lmat_ref[...], jnp.broadcast_to(tot, inrow.shape).astype(BF16))
        return inrow, offs, tot

    eq_f = jnp.where(equal, 1.0, 0.0)
    eq_in, eq_off, _ = prefix(eq_f)
    eq_rank = (eq_in + eq_off).reshape(n_exp, n_rows, lanes) - eq_f
    sel_f = jnp.where(above | (equal & (eq_rank < need)), 1.0, 0.0)
    inrow, offs, tot = prefix(sel_f)
    sel2 = sel_f.reshape(n_exp * n_rows, lanes)
    aff2 = aff.reshape(n_exp * n_rows, lanes)
    a_hi = aff2.astype(BF16)
    r1 = aff2 - a_hi.astype(F32)
    a_mid = r1.astype(BF16)
    a_lo = (r1 - a_mid.astype(F32)).astype(BF16)

    slot = lax.broadcasted_iota(jnp.int32, (1, cap), 1).astype(F32)
    row_id = lax.broadcasted_iota(jnp.int32, (n_rows, 1), 0).astype(F32)
    lane_id = lax.broadcasted_iota(jnp.int32, (lanes, 1), 0).astype(F32)
    for ei in range(n_exp):
        rows = slice(ei * n_rows, (ei + 1) * n_rows)
        off_e = offs[rows, 0:1]
        tot_e = tot[rows]
        in_row = jnp.where((off_e <= slot) & (slot < off_e + tot_e), 1.0, 0.0)
        row_base = (in_row * off_e).sum(axis=0, keepdims=True)
        row_of = (in_row * row_id).sum(axis=0, keepdims=True)
        in_row_b = in_row.astype(BF16)
        pick = lambda v: _dot(v, in_row_b, _TN)
        hit = (pick(inrow[rows].astype(BF16)) == slot + 1.0 - row_base) & (pick(sel2[rows].astype(BF16)) > 0.5)
        hit_f = jnp.where(hit, 1.0, 0.0)
        aff_rows = (pick(a_hi[rows]) + pick(a_mid[rows])) + pick(a_lo[rows])
        lane_of = (hit_f * lane_id).sum(axis=0, keepdims=True)
        idx_ref[0, ei:ei + 1, :] = (row_of * lanes + lane_of).astype(jnp.int32)
        gate_ref[0, ei:ei + 1, :] = (hit_f * aff_rows).sum(axis=0, keepdims=True)


def _route(aff, cap):
    bx, e, n = aff.shape
    unit = LANES * ROUTE_ROW_ALIGN
    n_pad = -(-n // unit) * unit
    if n_pad != n:
        aff = jnp.pad(aff, ((0, 0), (0, 0), (0, n_pad - n)), constant_values=-1.0)
    n_rows = n_pad // LANES
    tri = np.triu(np.ones((LANES, LANES), np.float32))
    r = np.arange(e * n_rows)
    lmat = ((r[:, None] // n_rows == r[None, :] // n_rows) & (r[None, :] < r[:, None])).astype(np.float32)
    return pl.pallas_call(
        functools.partial(_route_kernel, cap=cap),
        out_shape=[jax.ShapeDtypeStruct((bx, e, cap), jnp.int32), jax.ShapeDtypeStruct((bx, e, cap), F32)],
        grid=(bx,),
        in_specs=[
            pl.BlockSpec((1, e, n_rows, LANES), lambda b: (b, 0, 0, 0)),
            pl.BlockSpec((LANES, LANES), lambda b: (0, 0)),
            pl.BlockSpec((e * n_rows, e * n_rows), lambda b: (0, 0)),
        ],
        out_specs=[pl.BlockSpec((1, e, cap), lambda b: (b, 0, 0)), pl.BlockSpec((1, e, cap), lambda b: (b, 0, 0))],
        compiler_params=_params(("parallel",)),
        name="moe_route",
    )(aff.reshape(bx, e, n_rows, LANES), jnp.asarray(tri, BF16), jnp.asarray(lmat, BF16))


def _moe_kernel(idx_ref, h_hbm, gate_ref, g2_ref, wg_ref, wu_ref, wd_ref, x_hbm, out_hbm,
                xbuf, abuf, xb, gsem, asem, ssem, *, rows, cap, chunk, n_tiles):
    del x_hbm
    tr = xbuf.shape[1] // rows
    e, p, f = pl.program_id(0), pl.program_id(1), pl.program_id(2)
    n_steps = pl.num_programs(0) * n_tiles
    k = e * n_tiles + p
    slot = lax.rem(k, 2)
    lag = 2 if n_tiles >= 2 else 1

    def row_copies(step, src, dst, dslot, sem, gather):
        base = step * rows

        def body(i, carry):
            tok = idx_ref[base + i]
            hbm_row = src.at[pl.ds(pl.multiple_of(tok * tr, tr), tr)]
            buf_row = dst.at[dslot, pl.ds(pl.multiple_of(i * tr, tr), tr)]
            cp = (pltpu.make_async_copy(hbm_row, buf_row, sem.at[dslot]) if gather
                  else pltpu.make_async_copy(buf_row, hbm_row, sem.at[dslot]))
            cp.start()
            return carry

        lax.fori_loop(0, rows, body, 0, unroll=8)

    def wait_rows(buf, bslot, sem):
        pltpu.make_async_copy(buf.at[bslot], buf.at[bslot], sem.at[bslot]).wait()

    @pl.when(f == 0)
    def _begin():
        @pl.when(k == 0)
        def _():
            row_copies(k, h_hbm, xbuf, slot, gsem, True)

        @pl.when(k >= lag)
        def _():
            wait_rows(abuf, lax.rem(k - lag, 2), ssem)

        row_copies(k, out_hbm, abuf, slot, asem, True)
        wait_rows(xbuf, slot, gsem)

        @pl.when(k + 1 < n_steps)
        def _():
            row_copies(k + 1, h_hbm, xbuf, 1 - slot, gsem, True)

        for j in range(tr):
            xb[:, j * LANES:(j + 1) * LANES] = xbuf[slot, pl.ds(j, rows, stride=tr), :].astype(BF16)

    wg = wg_ref[0, 0].astype(BF16)
    wu = wu_ref[0, 0].astype(BF16)
    wd = wd_ref[0, 0].astype(BF16)
    for c in range(rows // chunk):
        rs = slice(c * chunk, (c + 1) * chunk)
        x = xb[rs]
        a = _dot(x, wg)
        u = _dot(x, wu)
        hcat = ((a * jax.nn.sigmoid(a)) * u).astype(BF16)
        y = _dot(hcat, wd) * (gate_ref[0, rs] * g2_ref[c * chunk // cap])
        if c == 0:
            @pl.when(f == 0)
            def _():
                wait_rows(abuf, slot, asem)
        for j in range(tr):
            abuf[slot, pl.ds(c * chunk * tr + j, chunk, stride=tr), :] += y[:, j * LANES:(j + 1) * LANES]

    @pl.when(f == pl.num_programs(2) - 1)
    def _finish():
        row_copies(k, out_hbm, abuf, slot, ssem, False)

        @pl.when(k == n_steps - 1)
        def _():
            if lag == 2:
                @pl.when(k >= 1)
                def _():
                    wait_rows(abuf, 1 - slot, ssem)
            wait_rows(abuf, slot, ssem)


def _ec_moe(x, h, aff, g2, layer, w_gate, w_up, w_down, dims):
    bx, n, d = dims
    tr = d // LANES
    e = aff.shape[1]
    ff = w_gate.shape[3]
    cap = EC_CAPACITY * n // e
    idx, gate = _route(aff, cap)
    rows_of = (idx + (jnp.arange(bx, dtype=jnp.int32) * n)[:, None, None])
    rows_of = jnp.transpose(rows_of, (1, 0, 2)).reshape(e * bx * cap)
    gate_c = jnp.transpose(gate, (1, 0, 2)).reshape(e, bx * cap, 1)
    g2 = jnp.broadcast_to(g2, (bx, 1, d))
    nb = max(1, min(bx, MOE_TILE_ROWS // cap))
    assert bx % nb == 0
    rows = nb * cap
    n_tiles = bx // nb
    chunk = min(MOE_ROW_CHUNK, cap)
    assert cap % chunk == 0
    tf = min(512, ff)
    grid_spec = pltpu.PrefetchScalarGridSpec(
        num_scalar_prefetch=1,
        grid=(e, n_tiles, ff // tf),
        in_specs=[
            pl.BlockSpec(memory_space=pl.ANY),
            pl.BlockSpec((1, rows, 1), lambda ei, p, f, idx: (ei, p, 0)),
            pl.BlockSpec((nb, 1, d), lambda ei, p, f, idx: (p, 0, 0)),
            pl.BlockSpec((1, 1, d, tf), lambda ei, p, f, idx: (layer, ei, 0, f)),
            pl.BlockSpec((1, 1, d, tf), lambda ei, p, f, idx: (layer, ei, 0, f)),
            pl.BlockSpec((1, 1, tf, d), lambda ei, p, f, idx: (layer, ei, f, 0)),
            pl.BlockSpec(memory_space=pl.ANY),
        ],
        out_specs=pl.BlockSpec(memory_space=pl.ANY),
        scratch_shapes=[
            pltpu.VMEM((2, rows * tr, LANES), F32),
            pltpu.VMEM((2, rows * tr, LANES), F32),
            pltpu.VMEM((rows, d), BF16),
            pltpu.SemaphoreType.DMA((2,)),
            pltpu.SemaphoreType.DMA((2,)),
            pltpu.SemaphoreType.DMA((2,)),
        ],
    )
    return pl.pallas_call(
        functools.partial(_moe_kernel, rows=rows, cap=cap, chunk=chunk, n_tiles=n_tiles),
        out_shape=jax.ShapeDtypeStruct(x.shape, F32),
        grid_spec=grid_spec,
        input_output_aliases={7: 0},
        compiler_params=_params(("arbitrary", "arbitrary", "arbitrary"), VMEM_LIMIT),
        name="moe_ffn",
    )(rows_of, h, gate_c, g2, w_gate, w_up, w_down, x)


def kernel(x, c, ctx, c_ctx, w_mod, b_mod, norm_mix, norm_ffn, a_w_qkv, a_q_gain, a_k_gain, a_sink, a_w_o,
           b_w_qkv, b_q_gain, b_k_gain, b_rpb, b_w_o, moe_router, moe_w_gate, moe_w_up, moe_w_down):
    b, s, d = x.shape
    depth = w_mod.shape[0]
    pad = (-(b + 1)) % 8
    cvec = jnp.concatenate([c, c_ctx[None], jnp.zeros((pad, d), F32)], axis=0)
    mods = _modulation(cvec, w_mod, b_mod)
    rope_tabs = _rope_tables(s)
    xl, xc = x, ctx
    for i in range(depth):
        last = i == depth - 1
        til_l = dict(dims=x.shape) if i > 0 else {}
        til_c = dict(dims=ctx.shape) if i > 0 else {}
        ml =[m.reshape(b, 1, d) for m in jnp.split(mods[i, :b], 6, axis=-1)]
        mc = [m.reshape(1, 1, d) for m in jnp.split(mods[i, b:b + 1], 6, axis=-1)]
        sh1_l, sc1_l, g1_l, sh2_l, sc2_l, g2_l = ml
        sh1_c, sc1_c, g1_c, sh2_c, sc2_c, g2_c = mc
        j = i // N_MIXERS
        if i % N_MIXERS == 0:
            ql, kl, vl = _project(xl, norm_mix[i], sh1_l, sc1_l, a_w_qkv[j], a_q_gain[j], a_k_gain[j],
                                  A_HEADS, A_KV_HEADS, rope_tabs, **til_l)
            qc, kc, vc = _project(xc, norm_mix[i], sh1_c, sc1_c, a_w_qkv[j], a_q_gain[j], a_k_gain[j],
                                  A_HEADS, A_KV_HEADS, **til_c)
            ol = _attn_a(ql, kl, vl, kc, vc, a_sink[j])
            oc = None if last else _attn_ctx(qc, kc, vc, a_sink[j])
            w_o = a_w_o[j]
        else:
            ql, kl, vl = _project(xl, norm_mix[i], sh1_l, sc1_l, b_w_qkv[j], b_q_gain[j], b_k_gain[j],
                                  B_HEADS, B_HEADS, **til_l)
            qc, kc, vc = _project(xc, norm_mix[i], sh1_c, sc1_c, b_w_qkv[j], b_q_gain[j], b_k_gain[j],
                                  B_HEADS, B_HEADS, **til_c)
            ol = _attn_b(ql, kl, vl, kc, vc, b_rpb[j])
            oc = None if last else _attn_ctx(qc, kc, vc, None)
            w_o = b_w_o[j]
        xl, hl, aff_l = _out_proj(ol, w_o, xl, g1_l, norm_ffn[i], sh2_l, sc2_l, moe_router[i], **til_l)
        xl = _ec_moe(xl, hl, aff_l, g2_l, i, moe_w_gate, moe_w_up, moe_w_down, x.shape)
        if not last:
            xc, hc, aff_c = _out_proj(oc, w_o, xc, g1_c, norm_ffn[i], sh2_c, sc2_c, moe_router[i], **til_c)
            xc = _ec_moe(xc, hc, aff_c, g2_c, i, moe_w_gate, moe_w_up, moe_w_down, ctx.shape)
    return _untile(xl, x.shape)
```

```python
import functools

import numpy as np
import jax
import jax.numpy as jnp
from jax import lax
from jax.experimental import pallas as pl
from jax.experimental.pallas import tpu as pltpu

F32 = jnp.float32
BF16 = jnp.bfloat16

GRID_W = 64
HEAD_DIM = 64
A_HEADS = 16
A_KV_HEADS = 4
A_WINDOW = 128
A_BLOCK = 128
B_HEADS = 16
NA_ROWS = 8
NA_COLS = 16
N_EXPERTS = 16
EC_CAPACITY = 2
ROPE_BASE = 10000.0
EPS = 1e-6
NEG_INF = -1e30
ATTN_SCALE = HEAD_DIM ** -0.5
N_MIXERS = 2

NA_QROWS = 2
NA_KROWS = NA_ROWS + NA_QROWS
NA_HEAD_GROUP = 16

LANES = 128
VMEM_LIMIT = 56 * 1024 * 1024

_NT = (((1,), (1,)), ((), ()))
_NN = (((1,), (0,)), ((), ()))


def _dot(a, b, dims=_NN):
    return lax.dot_general(a, b, dims, preferred_element_type=F32)


def _split(a):
    hi = a.astype(BF16)
    lo = (a - hi.astype(F32)).astype(BF16)
    return hi, lo


def _dot3(a, b, dims=_NN):
    ah, al = _split(a)
    bh, bl = _split(b)
    return _dot(ah, bh, dims) + (_dot(ah, bl, dims) + _dot(al, bh, dims))


def _params(sem, vmem=None):
    return pltpu.CompilerParams(dimension_semantics=sem, vmem_limit_bytes=vmem)


def _mod_kernel(c_ref, w_ref, b_ref, o_ref):
    c = c_ref[...]
    a = c * jax.nn.sigmoid(c)
    o_ref[0] = _dot3(a, w_ref[0]) + b_ref[0]


def _modulation(cvec, w_mod, b_mod):
    depth, d, n = w_mod.shape
    rows = cvec.shape[0]
    tn = 1024
    return pl.pallas_call(
        _mod_kernel,
        out_shape=jax.ShapeDtypeStruct((depth, rows, n), F32),
        grid=(depth, n // tn),
        in_specs=[
            pl.BlockSpec((rows, d), lambda i, j: (0, 0)),
            pl.BlockSpec((1, d, tn), lambda i, j: (i, 0, j)),
            pl.BlockSpec((1, 1, tn), lambda i, j: (i, 0, j)),
        ],
        out_specs=pl.BlockSpec((1, rows, tn), lambda i, j: (i, 0, j)),
        compiler_params=_params(("parallel", "parallel")),
        name="modulation",
    )(cvec, w_mod, b_mod.reshape(depth, 1, n))


def _rms_mod(x, g, shift, scale):
    y = x * lax.rsqrt(jnp.mean(x * x, axis=-1, keepdims=True) + EPS)
    return (y * g) * (1.0 + scale) + shift


def _load_tokens(ref, tm):
    chunks = ref.shape[0] // tm
    return jnp.concatenate([ref[pl.ds(j, tm, stride=chunks), :] for j in range(chunks)], axis=1)


def _store_tokens(ref, val):
    tm, d = val.shape
    chunks = d // LANES
    for j in range(chunks):
        ref[pl.ds(j, tm, stride=chunks), :] = val[:, j * LANES:(j + 1) * LANES]


def _token_spec(tm, d, n_tiles, tiled):
    if tiled:
        return pl.BlockSpec((tm * (d // LANES), LANES), lambda b, t: (b * n_tiles + t, 0))
    return pl.BlockSpec((1, tm, d), lambda b, t: (b, t, 0))


def _proj_kernel(*refs, n_q, n_kv, rope, tiled):
    if rope:
        x_ref, g_ref, sh_ref, sc_ref, wt_ref, cq_ref, ck_ref, sq_ref, sk_ref, qt_ref, k_ref, vt_ref = refs
    else:
        x_ref, g_ref, sh_ref, sc_ref, wt_ref, cq_ref, ck_ref, qt_ref, k_ref, vt_ref = refs
        sq_ref = sk_ref = None
    dh = HEAD_DIM
    x = _load_tokens(x_ref, qt_ref.shape[3]) if tiled else x_ref[0]
    h = _rms_mod(x, g_ref[...], sh_ref[0], sc_ref[0]).astype(BF16)
    res = _dot(wt_ref[...], h, _NT)

    def head(j, c_ref, s_ref):
        t = res[j * dh:(j + 1) * dh]
        tn = t * lax.rsqrt(jnp.mean(t * t, axis=0, keepdims=True) + EPS)
        out = tn * c_ref[...]
        if rope:
            q4 = dh // 4
            partner = jnp.concatenate([tn[q4:2 * q4], tn[:q4], tn[3 * q4:], tn[2 * q4:3 * q4]], axis=0)
            out = out + partner * s_ref[...]
        return out

    for j in range(n_q):
        qt_ref[0, j] = head(j, cq_ref, sq_ref).astype(BF16)
    ks = [head(n_q + j, ck_ref, sk_ref) for j in range(n_kv)]
    k_ref[0] = jnp.concatenate(ks, axis=0).T.astype(BF16)
    v0 = (n_q + n_kv) * dh
    for j in range(n_kv):
        vt_ref[0, j] = res[v0 + j * dh:v0 + (j + 1) * dh].astype(BF16)


def _rope_perm():
    d = np.arange(HEAD_DIM)
    return np.where((d % 32) < 16, d + 16, d - 16)


def _rope_tables(n_tok):
    pos = jnp.arange(n_tok)
    row = (pos // GRID_W).astype(F32)
    col = (pos % GRID_W).astype(F32)
    n_freq = HEAD_DIM // 4
    inv = ROPE_BASE ** (-jnp.arange(n_freq, dtype=F32) / n_freq)
    ang_r = row[:, None] * inv
    ang_c = col[:, None] * inv
    cos_t = jnp.concatenate([jnp.cos(ang_r), jnp.cos(ang_r), jnp.cos(ang_c), jnp.cos(ang_c)], axis=-1)
    sin_t = jnp.concatenate([-jnp.sin(ang_r), jnp.sin(ang_r), -jnp.sin(ang_c), jnp.sin(ang_c)], axis=-1)
    return cos_t.T, sin_t.T


def _project(x, g, shift, scale, w, q_gain, k_gain, n_q, n_kv, rope_tabs=None, dims=None):
    tiled = dims is not None
    bx, n, d = dims if tiled else x.shape
    dh = HEAD_DIM
    rope = rope_tabs is not None
    tm = min(512, n)
    qg = (q_gain.astype(F32) * ATTN_SCALE).reshape(dh, 1)
    kg = k_gain.astype(F32).reshape(dh, 1)
    wt = w.T.astype(BF16)
    n_cols = wt.shape[0]
    bm = shift.shape[0]
    mod_map = (lambda b, t: (b, 0, 0)) if bm == bx else (lambda b, t: (0, 0, 0))
    const2 = lambda b, t: (0, 0)
    tok_map = lambda b, t: (0, t)
    in_specs = [
        _token_spec(tm, d, n // tm, tiled),
        pl.BlockSpec((1, d), const2),
        pl.BlockSpec((1, 1, d), mod_map),
        pl.BlockSpec((1, 1, d), mod_map),
        pl.BlockSpec((n_cols, d), const2),
    ]
    args = [x, g.reshape(1, d), shift, scale, wt]
    if rope:
        cos_t, sin_t = rope_tabs
        perm = _rope_perm()
        args += [qg * cos_t, kg * cos_t, qg[perm] * sin_t, kg[perm] * sin_t]
        in_specs += [pl.BlockSpec((dh, tm), tok_map)] * 4
    else:
        args += [jnp.broadcast_to(qg, (dh, tm)), jnp.broadcast_to(kg, (dh, tm))]
        in_specs += [pl.BlockSpec((dh, tm), const2)] * 2
    out_shape = [jax.ShapeDtypeStruct((bx, n_q, dh, n), BF16),
                 jax.ShapeDtypeStruct((bx, n, n_kv * dh), BF16),
                 jax.ShapeDtypeStruct((bx, n_kv, dh, n), BF16)]
    out_specs = [pl.BlockSpec((1, n_q, dh, tm), lambda b, t: (b, 0, 0, t)),
                 pl.BlockSpec((1, tm, n_kv * dh), lambda b, t: (b, t, 0)),
                 pl.BlockSpec((1, n_kv, dh, tm), lambda b, t: (b, 0, 0, t))]
    return pl.pallas_call(
        functools.partial(_proj_kernel, n_q=n_q, n_kv=n_kv, rope=rope, tiled=tiled),
        out_shape=out_shape,
        grid=(bx, n // tm),
        in_specs=in_specs,
        out_specs=out_specs,
        compiler_params=_params(("parallel", "parallel"), VMEM_LIMIT),
        name="qkv_proj_rope" if rope else "qkv_proj",
    )(*args)


def _softmax_pv_t(scores, vts, sinks):
    ms = []
    for parts, sink in zip(scores, sinks):
        m = parts[0].max(axis=0, keepdims=True)
        for s in parts[1:]:
            m = jnp.maximum(m, s.max(axis=0, keepdims=True))
        ms.append(m if sink is None else jnp.maximum(m, sink))
    probs, dens = [], []
    for parts, m, sink in zip(scores, ms, sinks):
        ps = [jnp.exp(s - m) for s in parts]
        den = ps[0].sum(axis=0, keepdims=True)
        for p in ps[1:]:
            den = den + p.sum(axis=0, keepdims=True)
        dens.append(den if sink is None else den + jnp.exp(sink - m))
        probs.append([p.astype(BF16) for p in ps])
    outs = []
    for ps, vt_parts, den in zip(probs, vts, dens):
        acc = _dot(vt_parts[0], ps[0])
        for vt, p in zip(vt_parts[1:], ps[1:]):
            acc = acc + _dot(vt, p)
        outs.append(acc / den)
    return outs


def _attn_a_kernel(qt_ref, k_ref, vt_ref, kc_ref, vct_ref, sink_ref, o_ref, *, seq, n_kv, group):
    blk = A_BLOCK
    span = 3 * blk
    dh = HEAD_DIM
    i = pl.program_id(1)
    start = pl.multiple_of(jnp.clip((i - 1) * blk, 0, seq - span), blk)
    krow = lax.broadcasted_iota(jnp.int32, (span, group * blk), 0)
    qcol = lax.broadcasted_iota(jnp.int32, (span, group * blk), 1)
    rel = (i * blk - start) + (qcol & (blk - 1)) - krow
    valid = jnp.abs(rel) <= A_WINDOW
    scores, vts = [], []
    for h in range(n_kv):
        qt = jnp.concatenate([qt_ref[0, h * group + g] for g in range(group)], axis=1)
        kw = k_ref[0, pl.ds(start, span), h * dh:(h + 1) * dh]
        s_loc = jnp.where(valid, _dot(kw, qt), NEG_INF)
        s_ctx = _dot(kc_ref[0, :, h * dh:(h + 1) * dh], qt)
        scores.append([s_loc, s_ctx])
        vts.append([vt_ref[0, h, :, pl.ds(start, span)], vct_ref[0, h]])
    ots = _softmax_pv_t(scores, vts, [sink_ref[h] for h in range(n_kv)])
    outs = [ot[:, g * blk:(g + 1) * blk] for ot in ots for g in range(group)]
    o_ref[0] = jnp.concatenate(outs, axis=0).T.astype(BF16)


def _sink_rows(sink, n_kv, group, cols):
    s = sink.astype(F32).reshape(n_kv, 1, group, 1)
    return jnp.broadcast_to(s, (n_kv, 1, group, cols)).reshape(n_kv, 1, group * cols)


def _attn_a(qt, k, vt, kc, vct, sink):
    b, hq, dh, s = qt.shape
    n_kv = vt.shape[1]
    group = hq // n_kv
    nc = kc.shape[1]
    blk = A_BLOCK
    assert s % blk == 0 and s >= 3 * blk
    return pl.pallas_call(
        functools.partial(_attn_a_kernel, seq=s, n_kv=n_kv, group=group),
        out_shape=jax.ShapeDtypeStruct((b, s, hq * dh), BF16),
        grid=(b, s // blk),
        in_specs=[
            pl.BlockSpec((1, hq, dh, blk), lambda bi, i: (bi, 0, 0, i)),
            pl.BlockSpec((1, s, n_kv * dh), lambda bi, i: (bi, 0, 0)),
            pl.BlockSpec((1, n_kv, dh, s), lambda bi, i: (bi, 0, 0, 0)),
            pl.BlockSpec((1, nc, n_kv * dh), lambda bi, i: (bi, 0, 0)),
            pl.BlockSpec((1, n_kv, dh, nc), lambda bi, i: (bi, 0, 0, 0)),
            pl.BlockSpec((n_kv, 1, group * blk), lambda bi, i: (0, 0, 0)),
        ],
        out_specs=pl.BlockSpec((1, blk, hq * dh), lambda bi, i: (bi, i, 0)),
        compiler_params=_params(("parallel", "arbitrary"), VMEM_LIMIT),
        name="attn_window",
    )(qt, k, vt, kc, vct, _sink_rows(sink, n_kv, group, blk))


def _attn_ctx_kernel(qt_ref, k_ref, vt_ref, sink_ref, o_ref, *, n_kv, group, use_sink):
    dh = HEAD_DIM
    nq = qt_ref.shape[3]
    scores, vts = [], []
    for h in range(n_kv):
        qt = jnp.concatenate([qt_ref[0, h * group + g] for g in range(group)], axis=1)
        scores.append([_dot(k_ref[0, :, h * dh:(h + 1) * dh], qt)])
        vts.append([vt_ref[0, h]])
    ots = _softmax_pv_t(scores, vts, [sink_ref[h] if use_sink else None for h in range(n_kv)])
    outs = [ot[:, g * nq:(g + 1) * nq] for ot in ots for g in range(group)]
    o_ref[0] = jnp.concatenate(outs, axis=0).T.astype(BF16)


def _attn_ctx(qt, k, vt, sink):
    b, hq, dh, n = qt.shape
    n_kv = vt.shape[1]
    group = hq // n_kv
    use_sink = sink is not None
    sink_r = _sink_rows(sink if use_sink else jnp.zeros((hq,), F32), n_kv, group, n)
    return pl.pallas_call(
        functools.partial(_attn_ctx_kernel, n_kv=n_kv, group=group, use_sink=use_sink),
        out_shape=jax.ShapeDtypeStruct((b, n, hq * dh), BF16),
        grid=(b,),
        in_specs=[
            pl.BlockSpec((1, hq, dh, n), lambda bi: (bi, 0, 0, 0)),
            pl.BlockSpec((1, n, n_kv * dh), lambda bi: (bi, 0, 0)),
            pl.BlockSpec((1, n_kv, dh, n), lambda bi: (bi, 0, 0, 0)),
            pl.BlockSpec((n_kv, 1, group * n), lambda bi: (0, 0, 0)),
        ],
        out_specs=pl.BlockSpec((1, n, hq * dh), lambda bi: (bi, 0, 0)),
        compiler_params=_params(("parallel",)),
        name="attn_ctx",
    )(qt, k, vt, sink_r)


def _na_bias_tables(rpb, n_rows):
    w = GRID_W
    n_blocks = n_rows // NA_QROWS
    half = NA_ROWS // 2
    cfgs, cfg_of = {}, []
    for rb in range(n_blocks):
        r0 = rb * NA_QROWS
        rs = int(np.clip(r0 - half, 0, n_rows - NA_KROWS))
        key = (r0 - rs,) + tuple(int(np.clip(r0 + a - half, 0, n_rows - NA_ROWS)) - rs for a in range(NA_QROWS))
        cfg_of.append(cfgs.setdefault(key, len(cfgs)))
    kc = np.arange(w)[:, None]
    qc = np.arange(w)[None, :]
    wstart = np.clip(qc - NA_COLS // 2, 0, w - NA_COLS)
    col_ok = (kc >= wstart) & (kc < wstart + NA_COLS)
    dcol = np.clip(kc - qc, -(NA_COLS - 1), NA_COLS - 1) + NA_COLS - 1
    onehot = (dcol[None] == np.arange(2 * NA_COLS - 1)[:, None, None]).astype(np.float32)
    tiles = jnp.einsum('hrd,dkq->hrkq', rpb.astype(F32), jnp.asarray(onehot), precision=lax.Precision.HIGHEST)
    tiles = jnp.where(col_ok[None, None], tiles, NEG_INF)
    masked = jnp.full((rpb.shape[0], w, w), NEG_INF, F32)
    tabs = []
    for key in cfgs:
        off, rsr = key[0], key[1:]
        rows = []
        for c in range(NA_KROWS):
            cols = []
            for a in range(NA_QROWS):
                assert 0 <= rsr[a] and rsr[a] + NA_ROWS <= NA_KROWS
                row_ok = rsr[a] <= c < rsr[a] + NA_ROWS
                cols.append(tiles[:, c - off - a + NA_ROWS - 1] if row_ok else masked)
            rows.append(jnp.concatenate(cols, axis=2))
        tabs.append(jnp.concatenate(rows, axis=1))
    return jnp.asarray(np.array(cfg_of, np.int32)), jnp.stack(tabs)


def _attn_b_kernel(cfg_ref, qt_ref, k_ref, vt_ref, kc_ref, vct_ref, bias_ref, o_ref, *, n_rows, heads):
    del cfg_ref
    w = GRID_W
    dh = HEAD_DIM
    rb = pl.program_id(2)
    rs2 = jnp.clip(rb - NA_ROWS // (2 * NA_QROWS), 0, (n_rows - NA_KROWS) // NA_QROWS)
    start = pl.multiple_of(rs2 * (NA_QROWS * w), NA_QROWS * w)
    span = NA_KROWS * w
    scores, vts = [], []
    for h in range(heads):
        qt = qt_ref[0, h]
        kw = k_ref[0, pl.ds(start, span), h * dh:(h + 1) * dh]
        s_loc = _dot(kw, qt) + bias_ref[0, h]
        s_ctx = _dot(kc_ref[0, :, h * dh:(h + 1) * dh], qt)
        scores.append([s_loc, s_ctx])
        vts.append([vt_ref[0, h, :, pl.ds(start, span)], vct_ref[0, h]])
    outs = _softmax_pv_t(scores, vts, [None] * heads)
    o_ref[0] = jnp.concatenate(outs, axis=0).T.astype(BF16)


def _attn_b(qt, k, vt, kc, vct, rpb):
    b, hq, dh, s = qt.shape
    nc = kc.shape[1]
    w = GRID_W
    n_rows = s // w
    assert s % w == 0 and n_rows >= NA_KROWS and n_rows % NA_QROWS == 0
    cfg_of, table = _na_bias_tables(rpb, n_rows)
    hg = NA_HEAD_GROUP
    tq = NA_QROWS * w
    span = NA_KROWS * w
    grid_spec = pltpu.PrefetchScalarGridSpec(
        num_scalar_prefetch=1,
        grid=(b, hq // hg, n_rows // NA_QROWS),
        in_specs=[
            pl.BlockSpec((1, hg, dh, tq), lambda bi, g, r, cfg: (bi, g, 0, r)),
            pl.BlockSpec((1, s, hg * dh), lambda bi, g, r, cfg: (bi, 0, g), pipeline_mode=pl.Buffered(1)),
            pl.BlockSpec((1, hg, dh, s), lambda bi, g, r, cfg: (bi, g, 0, 0), pipeline_mode=pl.Buffered(1)),
            pl.BlockSpec((1, nc, hg * dh), lambda bi, g, r, cfg: (bi, 0, g)),
            pl.BlockSpec((1, hg, dh, nc), lambda bi, g, r, cfg: (bi, g, 0, 0)),
            pl.BlockSpec((1, hg, span, tq), lambda bi, g, r, cfg: (cfg[r], g, 0, 0)),
        ],
        out_specs=pl.BlockSpec((1, tq, hg * dh), lambda bi, g, r, cfg: (bi, r, g)),
    )
    return pl.pallas_call(
        functools.partial(_attn_b_kernel, n_rows=n_rows, heads=hg),
        out_shape=jax.ShapeDtypeStruct((b, s, hq * dh), BF16),
        grid_spec=grid_spec,
        compiler_params=_params(("parallel", "parallel", "arbitrary"), VMEM_LIMIT),
        name="attn_neighbourhood",
    )(cfg_of, qt, k, vt, kc, vct, table)


def _out_kernel(o_ref, wo_ref, x_ref, g1_ref, g_ref, sh_ref, sc_ref, wr_ref, xn_ref, h_ref, aff_ref, *, tiled):
    y = _dot(o_ref[0], wo_ref[...])
    x = _load_tokens(x_ref, o_ref.shape[1]) if tiled else x_ref[0]
    xn = x + g1_ref[0] * y
    _store_tokens(xn_ref, xn)
    h = _rms_mod(xn, g_ref[...], sh_ref[0], sc_ref[0])
    _store_tokens(h_ref, h)
    logits = _dot3(wr_ref[...], h, _NT)
    z = jnp.exp(logits - logits.max(axis=0, keepdims=True))
    aff_ref[0] = z / z.sum(axis=0, keepdims=True)


def _out_proj(o, w_o, x, g1, g, shift, scale, w_router, dims=None):
    tiled = dims is not None
    bx, n, d = dims if tiled else x.shape
    assert d % (8 * LANES) == 0
    da = o.shape[2]
    e = w_router.shape[1]
    tm = min(512, n)
    bm = g1.shape[0]
    mod_map = (lambda b, t: (b, 0, 0)) if bm == bx else (lambda b, t: (0, 0, 0))
    const2 = lambda b, t: (0, 0)
    mod_spec = pl.BlockSpec((1, 1, d), mod_map)
    tiled_shape = jax.ShapeDtypeStruct((bx * n * (d // LANES), LANES), F32)
    return pl.pallas_call(
        functools.partial(_out_kernel, tiled=tiled),
        out_shape=[tiled_shape, tiled_shape, jax.ShapeDtypeStruct((bx, e, n), F32)],
        grid=(bx, n // tm),
        in_specs=[
            pl.BlockSpec((1, tm, da), lambda b, t: (b, t, 0)),
            pl.BlockSpec((da, d), const2),
            _token_spec(tm, d, n // tm, tiled),
            mod_spec,
            pl.BlockSpec((1, d), const2),
            mod_spec,
            mod_spec,
            pl.BlockSpec((e, d), const2),
        ],
        out_specs=[_token_spec(tm, d, n // tm, True),
                   _token_spec(tm, d, n // tm, True),
                   pl.BlockSpec((1, e, tm), lambda b, t: (b, 0, t))],
        compiler_params=_params(("parallel", "parallel"), VMEM_LIMIT),
        name="out_proj_router",
    )(o, w_o.astype(BF16), x, g1, g.reshape(1, d), shift, scale, w_router.T)


def _untile_kernel(x_ref, o_ref):
    o_ref[0] = _load_tokens(x_ref, o_ref.shape[1])


def _untile(x, dims):
    bx, n, d = dims
    tm = min(512, n)
    return pl.pallas_call(
        _untile_kernel,
        out_shape=jax.ShapeDtypeStruct(dims, F32),
        grid=(bx, n // tm),
        in_specs=[_token_spec(tm, d, n // tm, True)],
        out_specs=_token_spec(tm, d, n // tm, False),
        compiler_params=_params(("parallel", "parallel")),
        name="untile_tokens",
    )(x)


ROUTE_ROW_ALIGN = 16
ROUTE_BISECT_STEPS = 40
MOE_TILE_ROWS = 1024
MOE_ROW_CHUNK = 512
_TN = (((0,), (0,)), ((), ()))


def _route_kernel(aff_ref, tri_ref, lmat_ref, idx_ref, gate_ref, *, cap):
    n_exp, n_rows, lanes = aff_ref.shape[1:]
    aff = aff_ref[0]

    def reduce_tokens(x, op):
        return op(op(x, axis=1, keepdims=True), axis=2, keepdims=True)

    def count_ge(t):
        return reduce_tokens(jnp.where(aff >= t, 1.0, 0.0), jnp.sum)

    def max_below(t):
        return reduce_tokens(jnp.where(aff < t, aff, -1.0), jnp.max)

    def bisect(_, lohi):
        lo, hi = lohi
        mid = 0.5 * (lo + hi)
        ok = count_ge(mid) >= cap
        return jnp.where(ok, mid, lo), jnp.where(ok, hi, mid)

    top = reduce_tokens(aff, jnp.max)
    _, hi = lax.fori_loop(0, ROUTE_BISECT_STEPS, bisect, (jnp.zeros_like(top), 2.0 * top + 1e-30))

    def short(t):
        return jnp.max(jnp.where(count_ge(t) < cap, 1.0, 0.0)) > 0.5

    def walk(t):
        return jnp.where(count_ge(t) >= cap, t, max_below(t))

    thr = lax.while_loop(short, walk, max_below(hi))
    above = aff > thr
    equal = aff == thr
    need = cap - reduce_tokens(jnp.where(above, 1.0, 0.0), jnp.sum)

    def prefix(x3):
        x2 = x3.reshape(n_exp * n_rows, lanes)
        inrow = _dot(x2.astype(BF16), tri_ref[...])
        tot = inrow[:, lanes - 1:lanes]
        offs = _dot(lmat_ref[...], jnp.broadcast_to(tot, inrow.shape).astype(BF16))
        return inrow, offs, tot

    eq_f = jnp.where(equal, 1.0, 0.0)
    eq_in, eq_off, _ = prefix(eq_f)
    eq_rank = (eq_in + eq_off).reshape(n_exp, n_rows, lanes) - eq_f
    sel_f = jnp.where(above | (equal & (eq_rank < need)), 1.0, 0.0)
    inrow, offs, tot = prefix(sel_f)
    sel2 = sel_f.reshape(n_exp * n_rows, lanes)
    aff2 = aff.reshape(n_exp * n_rows, lanes)
    a_hi = aff2.astype(BF16)
    r1 = aff2 - a_hi.astype(F32)
    a_mid = r1.astype(BF16)
    a_lo = (r1 - a_mid.astype(F32)).astype(BF16)

    slot = lax.broadcasted_iota(jnp.int32, (1, cap), 1).astype(F32)
    row_id = lax.broadcasted_iota(jnp.int32, (n_rows, 1), 0).astype(F32)
    lane_id = lax.broadcasted_iota(jnp.int32, (lanes, 1), 0).astype(F32)
    for ei in range(n_exp):
        rows = slice(ei * n_rows, (ei + 1) * n_rows)
        off_e = offs[rows, 0:1]
        tot_e = tot[rows]
        in_row = jnp.where((off_e <= slot) & (slot < off_e + tot_e), 1.0, 0.0)
        row_base = (in_row * off_e).sum(axis=0, keepdims=True)
        row_of = (in_row * row_id).sum(axis=0, keepdims=True)
        in_row_b = in_row.astype(BF16)
        pick = lambda v: _dot(v, in_row_b, _TN)
        hit = (pick(inrow[rows].astype(BF16)) == slot + 1.0 - row_base) & (pick(sel2[rows].astype(BF16)) > 0.5)
        hit_f = jnp.where(hit, 1.0, 0.0)
        aff_rows = (pick(a_hi[rows]) + pick(a_mid[rows])) + pick(a_lo[rows])
        lane_of = (hit_f * lane_id).sum(axis=0, keepdims=True)
        idx_ref[0, ei:ei + 1, :] = (row_of * lanes + lane_of).astype(jnp.int32)
        gate_ref[0, ei:ei + 1, :] = (hit_f * aff_rows).sum(axis=0, keepdims=True)


def _route(aff, cap):
    bx, e, n = aff.shape
    unit = LANES * ROUTE_ROW_ALIGN
    n_pad = -(-n // unit) * unit
    if n_pad != n:
        aff = jnp.pad(aff, ((0, 0), (0, 0), (0, n_pad - n)), constant_values=-1.0)
    n_rows = n_pad // LANES
    tri = np.triu(np.ones((LANES, LANES), np.float32))
    r = np.arange(e * n_rows)
    lmat = ((r[:, None] // n_rows == r[None, :] // n_rows) & (r[None, :] < r[:, None])).astype(np.float32)
    return pl.pallas_call(
        functools.partial(_route_kernel, cap=cap),
        out_shape=[jax.ShapeDtypeStruct((bx, e, cap), jnp.int32), jax.ShapeDtypeStruct((bx, e, cap), F32)],
        grid=(bx,),
        in_specs=[
            pl.BlockSpec((1, e, n_rows, LANES), lambda b: (b, 0, 0, 0)),
            pl.BlockSpec((LANES, LANES), lambda b: (0, 0)),
            pl.BlockSpec((e * n_rows, e * n_rows), lambda b: (0, 0)),
        ],
        out_specs=[pl.BlockSpec((1, e, cap), lambda b: (b, 0, 0)), pl.BlockSpec((1, e, cap), lambda b: (b, 0, 0))],
        compiler_params=_params(("parallel",)),
        name="moe_route",
    )(aff.reshape(bx, e, n_rows, LANES), jnp.asarray(tri, BF16), jnp.asarray(lmat, BF16))


def _moe_kernel(idx_ref, h_hbm, gate_ref, g2_ref, wg_ref, wu_ref, wd_ref, x_hbm, out_hbm,
                xbuf, abuf, xb, gsem, asem, ssem, *, rows, cap, chunk, n_tiles, n_ff):
    del x_hbm
    tr = xbuf.shape[1] // rows
    e, p, f = pl.program_id(0), pl.program_id(1), pl.program_id(2)
    n_steps = pl.num_programs(0) * n_tiles
    k = e * n_tiles + p
    slot = lax.rem(k, 2)
    lag = 2 if n_tiles >= 2 else 1

    def row_copy(step, i, src, dst, dslot, sem, gather, priority):
        tok = idx_ref[step * rows + i]
        hbm_row = src.at[pl.ds(pl.multiple_of(tok * tr, tr), tr)]
        buf_row = dst.at[dslot, pl.ds(pl.multiple_of(i * tr, tr), tr)]
        cp = (pltpu.make_async_copy(hbm_row, buf_row, sem.at[dslot]) if gather
              else pltpu.make_async_copy(buf_row, hbm_row, sem.at[dslot]))
        cp.start(priority=priority)

    def row_copies(step, src, dst, dslot, sem, gather):
        def body(i, carry):
            for u in range(2):
                row_copy(step, 2 * i + u, src, dst, dslot, sem, gather, u)
            return carry

        lax.fori_loop(0, rows // 2, body, 0, unroll=4)

    def row_copies_inline(step, lo, count, src, dst, dslot, sem, gather):
        for u in range(count):
            row_copy(step, lo + u, src, dst, dslot, sem, gather, u % 2)

    def wait_rows(buf, bslot, sem):
        pltpu.make_async_copy(buf.at[bslot], buf.at[bslot], sem.at[bslot]).wait()

    pipelined = abuf.shape[0] == 3
    acur = lax.rem(k, 3) if pipelined else slot
    anext, aprev = lax.rem(k + 1, 3), lax.rem(k + 2, 3)

    @pl.when(f == 0)
    def _begin():
        @pl.when(k == 0)
        def _():
            row_copies(k, h_hbm, xbuf, slot, gsem, True)
            if pipelined:
                row_copies(k, out_hbm, abuf, acur, asem, True)
                row_copies(k, out_hbm, abuf, aprev, asem, True)
                wait_rows(abuf, aprev, asem)

        if pipelined:
            @pl.when(k >= 1)
            def _():
                wait_rows(abuf, anext, ssem)
        else:
            @pl.when(k >= lag)
            def _():
                wait_rows(abuf, lax.rem(k - lag, 2), ssem)

            row_copies(k, out_hbm, abuf, slot, asem, True)
        wait_rows(xbuf, slot, gsem)
        for j in range(tr):
            xb[:, j * LANES:(j + 1) * LANES] = xbuf[slot, pl.ds(j, rows, stride=tr), :].astype(BF16)
        if pipelined:
            wait_rows(abuf, acur, asem)

    wg = wg_ref[0, 0].astype(BF16)
    wu = wu_ref[0, 0].astype(BF16)
    wd = wd_ref[0, 0].astype(BF16)
    n_chunks = rows // chunk
    ahead = rows // (n_ff * n_chunks)
    nxt = jnp.minimum(k + 1, n_steps - 1)
    prv = jnp.maximum(k - 1, 0)
    for c in range(n_chunks):
        rs = slice(c * chunk, (c + 1) * chunk)
        lo = (f * n_chunks + c) * ahead
        x = xb[rs]
        a = _dot(x, wg)
        row_copies_inline(nxt, lo, ahead, h_hbm, xbuf, 1 - slot, gsem, True)
        u = _dot(x, wu)
        if pipelined:
            row_copies_inline(nxt, lo, ahead, out_hbm, abuf, anext, asem, True)
        hcat = ((a * jax.nn.sigmoid(a)) * u).astype(BF16)
        y = _dot(hcat, wd) * (gate_ref[0, rs] * g2_ref[c * chunk // cap])
        if pipelined:
            row_copies_inline(prv, lo, ahead, out_hbm, abuf, aprev, ssem, False)
        elif c == 0:
            @pl.when(f == 0)
            def _():
                wait_rows(abuf, slot, asem)
        for j in range(tr):
            abuf[acur, pl.ds(c * chunk * tr + j, chunk, stride=tr), :] += y[:, j * LANES:(j + 1) * LANES]

    @pl.when(f == pl.num_programs(2) - 1)
    def _finish():
        if not pipelined:
            row_copies(k, out_hbm, abuf, slot, ssem, False)

        @pl.when(k == n_steps - 1)
        def _():
            wait_rows(xbuf, 1 - slot, gsem)
            if pipelined:
                wait_rows(abuf, anext, asem)
                row_copies(k, out_hbm, abuf, acur, ssem, False)
                wait_rows(abuf, aprev, ssem)
                wait_rows(abuf, acur, ssem)
            else:
                if lag == 2:
                    @pl.when(k >= 1)
                    def _():
                        wait_rows(abuf, 1 - slot, ssem)
                wait_rows(abuf, slot, ssem)


def _ec_moe(x, h, aff, g2, layer, w_gate, w_up, w_down, dims):
    bx, n, d = dims
    tr = d // LANES
    e = aff.shape[1]
    ff = w_gate.shape[3]
    cap = EC_CAPACITY * n // e
    idx, gate = _route(aff, cap)
    rows_of = (idx + (jnp.arange(bx, dtype=jnp.int32) * n)[:, None, None])
    rows_of = jnp.transpose(rows_of, (1, 0, 2)).reshape(e * bx * cap)
    gate_c = jnp.transpose(gate, (1, 0, 2)).reshape(e, bx * cap, 1)
    g2 = jnp.broadcast_to(g2, (bx, 1, d))
    nb = max(1, min(bx, MOE_TILE_ROWS // cap))
    assert bx % nb == 0
    rows = nb * cap
    n_tiles = bx // nb
    chunk = min(MOE_ROW_CHUNK, cap)
    assert cap % chunk == 0
    tf = min(512, ff)
    assert rows % ((ff // tf) * (rows // chunk)) == 0
    n_abuf = 3 if n_tiles >= 3 else 2
    grid_spec = pltpu.PrefetchScalarGridSpec(
        num_scalar_prefetch=1,
        grid=(e, n_tiles, ff // tf),
        in_specs=[
            pl.BlockSpec(memory_space=pl.ANY),
            pl.BlockSpec((1, rows, 1), lambda ei, p, f, idx: (ei, p, 0)),
            pl.BlockSpec((nb, 1, d), lambda ei, p, f, idx: (p, 0, 0)),
            pl.BlockSpec((1, 1, d, tf), lambda ei, p, f, idx: (layer, ei, 0, f)),
            pl.BlockSpec((1, 1, d, tf), lambda ei, p, f, idx: (layer, ei, 0, f)),
            pl.BlockSpec((1, 1, tf, d), lambda ei, p, f, idx: (layer, ei, f, 0)),
            pl.BlockSpec(memory_space=pl.ANY),
        ],
        out_specs=pl.BlockSpec(memory_space=pl.ANY),
        scratch_shapes=[
            pltpu.VMEM((2, rows * tr, LANES), F32),
            pltpu.VMEM((n_abuf, rows * tr, LANES), F32),
            pltpu.VMEM((rows, d), BF16),
            pltpu.SemaphoreType.DMA((2,)),
            pltpu.SemaphoreType.DMA((n_abuf,)),
            pltpu.SemaphoreType.DMA((n_abuf,)),
        ],
    )
    return pl.pallas_call(
        functools.partial(_moe_kernel, rows=rows, cap=cap, chunk=chunk, n_tiles=n_tiles, n_ff=ff // tf),
        out_shape=jax.ShapeDtypeStruct(x.shape, F32),
        grid_spec=grid_spec,
        input_output_aliases={7: 0},
        compiler_params=_params(("arbitrary", "arbitrary", "arbitrary"), VMEM_LIMIT),
        name="moe_ffn",
    )(rows_of, h, gate_c, g2, w_gate, w_up, w_down, x)


def kernel(x, c, ctx, c_ctx, w_mod, b_mod, norm_mix, norm_ffn, a_w_qkv, a_q_gain, a_k_gain, a_sink, a_w_o,
           b_w_qkv, b_q_gain, b_k_gain, b_rpb, b_w_o, moe_router, moe_w_gate, moe_w_up, moe_w_down):
    b, s, d = x.shape
    depth = w_mod.shape[0]
    pad = (-(b + 1)) % 8
    cvec = jnp.concatenate([c, c_ctx[None], jnp.zeros((pad, d), F32)], axis=0)
    mods = _modulation(cvec, w_mod, b_mod)
    rope_tabs = _rope_tables(s)
    xl, xc = x, ctx
    for i in range(depth):
        last = i == depth - 1
        til_l = dict(dims=x.shape) if i > 0 else {}
        til_c = dict(dims=ctx.shape) if i > 0 else {}
        ml =[m.reshape(b, 1, d) for m in jnp.split(mods[i, :b], 6, axis=-1)]
        mc = [m.reshape(1, 1, d) for m in jnp.split(mods[i, b:b + 1], 6, axis=-1)]
        sh1_l, sc1_l, g1_l, sh2_l, sc2_l, g2_l = ml
        sh1_c, sc1_c, g1_c, sh2_c, sc2_c, g2_c = mc
        j = i // N_MIXERS
        if i % N_MIXERS == 0:
            ql, kl, vl = _project(xl, norm_mix[i], sh1_l, sc1_l, a_w_qkv[j], a_q_gain[j], a_k_gain[j],
                                  A_HEADS, A_KV_HEADS, rope_tabs, **til_l)
            qc, kc, vc = _project(xc, norm_mix[i], sh1_c, sc1_c, a_w_qkv[j], a_q_gain[j], a_k_gain[j],
                                  A_HEADS, A_KV_HEADS, **til_c)
            ol = _attn_a(ql, kl, vl, kc, vc, a_sink[j])
            oc = None if last else _attn_ctx(qc, kc, vc, a_sink[j])
            w_o = a_w_o[j]
        else:
            ql, kl, vl = _project(xl, norm_mix[i], sh1_l, sc1_l, b_w_qkv[j], b_q_gain[j], b_k_gain[j],
                                  B_HEADS, B_HEADS, **til_l)
            qc, kc, vc = _project(xc, norm_mix[i], sh1_c, sc1_c, b_w_qkv[j], b_q_gain[j], b_k_gain[j],
                                  B_HEADS, B_HEADS, **til_c)
            ol = _attn_b(ql, kl, vl, kc, vc, b_rpb[j])
            oc = None if last else _attn_ctx(qc, kc, vc, None)
            w_o = b_w_o[j]
        xl, hl, aff_l = _out_proj(ol, w_o, xl, g1_l, norm_ffn[i], sh2_l, sc2_l, moe_router[i], **til_l)
        xl = _ec_moe(xl, hl, aff_l, g2_l, i, moe_w_gate, moe_w_up, moe_w_down, x.shape)
        if not last:
            xc, hc, aff_c = _out_proj(oc, w_o, xc, g1_c, norm_ffn[i], sh2_c, sc2_c, moe_router[i], **til_c)
            xc = _ec_moe(xc, hc, aff_c, g2_c, i, moe_w_gate, moe_w_up, moe_w_down, ctx.shape)
    return _untile(xl, x.shape)
```

```python
import functools

import numpy as np
import jax
import jax.numpy as jnp
from jax import lax
from jax.experimental import pallas as pl
from jax.experimental.pallas import tpu as pltpu

F32 = jnp.float32
BF16 = jnp.bfloat16

GRID_W = 64
HEAD_DIM = 64
A_HEADS = 16
A_KV_HEADS = 4
A_WINDOW = 128
A_BLOCK = 128
B_HEADS = 16
NA_ROWS = 8
NA_COLS = 16
N_EXPERTS = 16
EC_CAPACITY = 2
ROPE_BASE = 10000.0
EPS = 1e-6
NEG_INF = -1e30
ATTN_SCALE = HEAD_DIM ** -0.5
LOG2E = 1.4426950408889634
N_MIXERS = 2

NA_QROWS = 2
NA_KROWS = NA_ROWS + NA_QROWS
NA_HEAD_GROUP = 16

LANES = 128
VMEM_LIMIT = 56 * 1024 * 1024

_NT = (((1,), (1,)), ((), ()))
_NN = (((1,), (0,)), ((), ()))


def _dot(a, b, dims=_NN):
    return lax.dot_general(a, b, dims, preferred_element_type=F32)


def _split(a):
    hi = a.astype(BF16)
    lo = (a - hi.astype(F32)).astype(BF16)
    return hi, lo


def _dot3(a, b, dims=_NN):
    ah, al = _split(a)
    bh, bl = _split(b)
    return _dot(ah, bh, dims) + (_dot(ah, bl, dims) + _dot(al, bh, dims))


def _params(sem, vmem=None):
    return pltpu.CompilerParams(dimension_semantics=sem, vmem_limit_bytes=vmem)


def _mod_kernel(c_ref, w_ref, b_ref, o_ref):
    c = c_ref[...]
    a = c * jax.nn.sigmoid(c)
    o_ref[0] = _dot3(a, w_ref[0]) + b_ref[0]


def _modulation(cvec, w_mod, b_mod):
    depth, d, n = w_mod.shape
    rows = cvec.shape[0]
    tn = 1024
    return pl.pallas_call(
        _mod_kernel,
        out_shape=jax.ShapeDtypeStruct((depth, rows, n), F32),
        grid=(depth, n // tn),
        in_specs=[
            pl.BlockSpec((rows, d), lambda i, j: (0, 0)),
            pl.BlockSpec((1, d, tn), lambda i, j: (i, 0, j)),
            pl.BlockSpec((1, 1, tn), lambda i, j: (i, 0, j)),
        ],
        out_specs=pl.BlockSpec((1, rows, tn), lambda i, j: (i, 0, j)),
        compiler_params=_params(("parallel", "parallel")),
        name="modulation",
    )(cvec, w_mod, b_mod.reshape(depth, 1, n))


def _rms_mod(x, g, shift, scale):
    y = x * lax.rsqrt(jnp.mean(x * x, axis=-1, keepdims=True) + EPS)
    return (y * g) * (1.0 + scale) + shift


def _load_tokens(ref, tm):
    chunks = ref.shape[0] // tm
    return jnp.concatenate([ref[pl.ds(j, tm, stride=chunks), :] for j in range(chunks)], axis=1)


def _store_tokens(ref, val):
    tm, d = val.shape
    chunks = d // LANES
    for j in range(chunks):
        ref[pl.ds(j, tm, stride=chunks), :] = val[:, j * LANES:(j + 1) * LANES]


def _token_spec(tm, d, n_tiles, tiled):
    if tiled:
        return pl.BlockSpec((tm * (d // LANES), LANES), lambda b, t: (b * n_tiles + t, 0))
    return pl.BlockSpec((1, tm, d), lambda b, t: (b, t, 0))


def _proj_kernel(*refs, n_q, n_kv, rope, tiled):
    if rope:
        x_ref, g_ref, sh_ref, sc_ref, wt_ref, cq_ref, ck_ref, sq_ref, sk_ref, qt_ref, k_ref, vt_ref = refs
    else:
        x_ref, g_ref, sh_ref, sc_ref, wt_ref, cq_ref, ck_ref, qt_ref, k_ref, vt_ref = refs
        sq_ref = sk_ref = None
    dh = HEAD_DIM
    x = _load_tokens(x_ref, qt_ref.shape[3]) if tiled else x_ref[0]
    h = _rms_mod(x, g_ref[...], sh_ref[0], sc_ref[0]).astype(BF16)
    res = _dot(wt_ref[...], h, _NT)

    def head(j, c_ref, s_ref):
        t = res[j * dh:(j + 1) * dh]
        tn = t * lax.rsqrt(jnp.mean(t * t, axis=0, keepdims=True) + EPS)
        out = tn * c_ref[...]
        if rope:
            q4 = dh // 4
            partner = jnp.concatenate([tn[q4:2 * q4], tn[:q4], tn[3 * q4:], tn[2 * q4:3 * q4]], axis=0)
            out = out + partner * s_ref[...]
        return out

    for j in range(n_q):
        qt_ref[0, j] = head(j, cq_ref, sq_ref).astype(BF16)
    ks = [head(n_q + j, ck_ref, sk_ref) for j in range(n_kv)]
    k_ref[0] = jnp.concatenate(ks, axis=0).T.astype(BF16)
    v0 = (n_q + n_kv) * dh
    for j in range(n_kv):
        vt_ref[0, j] = res[v0 + j * dh:v0 + (j + 1) * dh].astype(BF16)


def _rope_perm():
    d = np.arange(HEAD_DIM)
    return np.where((d % 32) < 16, d + 16, d - 16)


def _rope_tables(n_tok):
    pos = jnp.arange(n_tok)
    row = (pos // GRID_W).astype(F32)
    col = (pos % GRID_W).astype(F32)
    n_freq = HEAD_DIM // 4
    inv = ROPE_BASE ** (-jnp.arange(n_freq, dtype=F32) / n_freq)
    ang_r = row[:, None] * inv
    ang_c = col[:, None] * inv
    cos_t = jnp.concatenate([jnp.cos(ang_r), jnp.cos(ang_r), jnp.cos(ang_c), jnp.cos(ang_c)], axis=-1)
    sin_t = jnp.concatenate([-jnp.sin(ang_r), jnp.sin(ang_r), -jnp.sin(ang_c), jnp.sin(ang_c)], axis=-1)
    return cos_t.T, sin_t.T


def _project(x, g, shift, scale, w, q_gain, k_gain, n_q, n_kv, rope_tabs=None, dims=None):
    tiled = dims is not None
    bx, n, d = dims if tiled else x.shape
    dh = HEAD_DIM
    rope = rope_tabs is not None
    tm = min(512, n)
    qg = (q_gain.astype(F32) * (ATTN_SCALE * LOG2E)).reshape(dh, 1)
    kg = k_gain.astype(F32).reshape(dh, 1)
    wt = w.T.astype(BF16)
    n_cols = wt.shape[0]
    bm = shift.shape[0]
    mod_map = (lambda b, t: (b, 0, 0)) if bm == bx else (lambda b, t: (0, 0, 0))
    const2 = lambda b, t: (0, 0)
    tok_map = lambda b, t: (0, t)
    in_specs = [
        _token_spec(tm, d, n // tm, tiled),
        pl.BlockSpec((1, d), const2),
        pl.BlockSpec((1, 1, d), mod_map),
        pl.BlockSpec((1, 1, d), mod_map),
        pl.BlockSpec((n_cols, d), const2),
    ]
    args = [x, g.reshape(1, d), shift, scale, wt]
    if rope:
        cos_t, sin_t = rope_tabs
        perm = _rope_perm()
        args += [qg * cos_t, kg * cos_t, qg[perm] * sin_t, kg[perm] * sin_t]
        in_specs += [pl.BlockSpec((dh, tm), tok_map)] * 4
    else:
        args += [jnp.broadcast_to(qg, (dh, tm)), jnp.broadcast_to(kg, (dh, tm))]
        in_specs += [pl.BlockSpec((dh, tm), const2)] * 2
    out_shape = [jax.ShapeDtypeStruct((bx, n_q, dh, n), BF16),
                 jax.ShapeDtypeStruct((bx, n, n_kv * dh), BF16),
                 jax.ShapeDtypeStruct((bx, n_kv, dh, n), BF16)]
    out_specs = [pl.BlockSpec((1, n_q, dh, tm), lambda b, t: (b, 0, 0, t)),
                 pl.BlockSpec((1, tm, n_kv * dh), lambda b, t: (b, t, 0)),
                 pl.BlockSpec((1, n_kv, dh, tm), lambda b, t: (b, 0, 0, t))]
    return pl.pallas_call(
        functools.partial(_proj_kernel, n_q=n_q, n_kv=n_kv, rope=rope, tiled=tiled),
        out_shape=out_shape,
        grid=(bx, n // tm),
        in_specs=in_specs,
        out_specs=out_specs,
        compiler_params=_params(("parallel", "parallel"), VMEM_LIMIT),
        name="qkv_proj_rope" if rope else "qkv_proj",
    )(*args)


def _softmax_pv_t(scores, vts, sinks):
    ms = []
    for parts, sink in zip(scores, sinks):
        m = parts[0].max(axis=0, keepdims=True)
        for s in parts[1:]:
            m = jnp.maximum(m, s.max(axis=0, keepdims=True))
        ms.append(m if sink is None else jnp.maximum(m, sink))
    probs, dens = [], []
    for parts, m, sink in zip(scores, ms, sinks):
        ps = [jnp.exp2(s - m) for s in parts]
        den = ps[0].sum(axis=0, keepdims=True)
        for p in ps[1:]:
            den = den + p.sum(axis=0, keepdims=True)
        dens.append(den if sink is None else den + jnp.exp2(sink - m))
        probs.append([p.astype(BF16) for p in ps])
    outs = []
    for ps, vt_parts, den in zip(probs, vts, dens):
        acc = _dot(vt_parts[0], ps[0])
        for vt, p in zip(vt_parts[1:], ps[1:]):
            acc = acc + _dot(vt, p)
        outs.append(acc / den)
    return outs


def _attn_a_kernel(qt_ref, k_ref, vt_ref, kc_ref, vct_ref, sink_ref, mask_ref, o_ref, *, seq, n_kv, group):
    blk = A_BLOCK
    span = 3 * blk
    dh = HEAD_DIM
    i = pl.program_id(1)
    start = pl.multiple_of(jnp.clip((i - 1) * blk, 0, seq - span), blk)
    scores, vts = [], []
    for h in range(n_kv):
        qt = jnp.concatenate([qt_ref[0, h * group + g] for g in range(group)], axis=1)
        kw = k_ref[0, pl.ds(start, span), h * dh:(h + 1) * dh]
        s_loc = _dot(kw, qt) + mask_ref[0]
        s_ctx = _dot(kc_ref[0, :, h * dh:(h + 1) * dh], qt)
        scores.append([s_loc, s_ctx])
        vts.append([vt_ref[0, h, :, pl.ds(start, span)], vct_ref[0, h]])
    ots = _softmax_pv_t(scores, vts, [sink_ref[h] for h in range(n_kv)])
    outs = [ot[:, g * blk:(g + 1) * blk] for ot in ots for g in range(group)]
    o_ref[0] = jnp.concatenate(outs, axis=0).T.astype(BF16)


def _sink_rows(sink, n_kv, group, cols):
    s = (sink.astype(F32) * LOG2E).reshape(n_kv, 1, group, 1)
    return jnp.broadcast_to(s, (n_kv, 1, group, cols)).reshape(n_kv, 1, group * cols)


def _attn_a(qt, k, vt, kc, vct, sink):
    b, hq, dh, s = qt.shape
    n_kv = vt.shape[1]
    group = hq // n_kv
    nc = kc.shape[1]
    blk = A_BLOCK
    n_blk = s // blk
    assert s % blk == 0 and n_blk >= 3
    off = (np.arange(3) * blk)[:, None, None]
    rel = off + (np.arange(group * blk) % blk)[None, None, :] - np.arange(3 * blk)[None, :, None]
    mask = jnp.asarray(np.where(np.abs(rel) <= A_WINDOW, 0.0, NEG_INF).astype(np.float32))
    return pl.pallas_call(
        functools.partial(_attn_a_kernel, seq=s, n_kv=n_kv, group=group),
        out_shape=jax.ShapeDtypeStruct((b, s, hq * dh), BF16),
        grid=(b, s // blk),
        in_specs=[
            pl.BlockSpec((1, hq, dh, blk), lambda bi, i: (bi, 0, 0, i)),
            pl.BlockSpec((1, s, n_kv * dh), lambda bi, i: (bi, 0, 0)),
            pl.BlockSpec((1, n_kv, dh, s), lambda bi, i: (bi, 0, 0, 0)),
            pl.BlockSpec((1, nc, n_kv * dh), lambda bi, i: (bi, 0, 0)),
            pl.BlockSpec((1, n_kv, dh, nc), lambda bi, i: (bi, 0, 0, 0)),
            pl.BlockSpec((n_kv, 1, group * blk), lambda bi, i: (0, 0, 0)),
            pl.BlockSpec((1, 3 * blk, group * blk),
                         lambda bi, i: (jnp.where(i == 0, 0, jnp.where(i == n_blk - 1, 2, 1)), 0, 0)),
        ],
        out_specs=pl.BlockSpec((1, blk, hq * dh), lambda bi, i: (bi, i, 0)),
        compiler_params=_params(("parallel", "arbitrary"), VMEM_LIMIT),
        name="attn_window",
    )(qt, k, vt, kc, vct, _sink_rows(sink, n_kv, group, blk), mask)


def _attn_ctx_kernel(qt_ref, k_ref, vt_ref, sink_ref, o_ref, *, n_kv, group, use_sink):
    dh = HEAD_DIM
    nq = qt_ref.shape[3]
    scores, vts = [], []
    for h in range(n_kv):
        qt = jnp.concatenate([qt_ref[0, h * group + g] for g in range(group)], axis=1)
        scores.append([_dot(k_ref[0, :, h * dh:(h + 1) * dh], qt)])
        vts.append([vt_ref[0, h]])
    ots = _softmax_pv_t(scores, vts, [sink_ref[h] if use_sink else None for h in range(n_kv)])
    outs = [ot[:, g * nq:(g + 1) * nq] for ot in ots for g in range(group)]
    o_ref[0] = jnp.concatenate(outs, axis=0).T.astype(BF16)


def _attn_ctx(qt, k, vt, sink):
    b, hq, dh, n = qt.shape
    n_kv = vt.shape[1]
    group = hq // n_kv
    use_sink = sink is not None
    sink_r = _sink_rows(sink if use_sink else jnp.zeros((hq,), F32), n_kv, group, n)
    return pl.pallas_call(
        functools.partial(_attn_ctx_kernel, n_kv=n_kv, group=group, use_sink=use_sink),
        out_shape=jax.ShapeDtypeStruct((b, n, hq * dh), BF16),
        grid=(b,),
        in_specs=[
            pl.BlockSpec((1, hq, dh, n), lambda bi: (bi, 0, 0, 0)),
            pl.BlockSpec((1, n, n_kv * dh), lambda bi: (bi, 0, 0)),
            pl.BlockSpec((1, n_kv, dh, n), lambda bi: (bi, 0, 0, 0)),
            pl.BlockSpec((n_kv, 1, group * n), lambda bi: (0, 0, 0)),
        ],
        out_specs=pl.BlockSpec((1, n, hq * dh), lambda bi: (bi, 0, 0)),
        compiler_params=_params(("parallel",)),
        name="attn_ctx",
    )(qt, k, vt, sink_r)


def _na_bias_tables(rpb, n_rows):
    w = GRID_W
    n_blocks = n_rows // NA_QROWS
    half = NA_ROWS // 2
    cfgs, cfg_of = {}, []
    for rb in range(n_blocks):
        r0 = rb * NA_QROWS
        rs = int(np.clip(r0 - half, 0, n_rows - NA_KROWS))
        key = (r0 - rs,) + tuple(int(np.clip(r0 + a - half, 0, n_rows - NA_ROWS)) - rs for a in range(NA_QROWS))
        cfg_of.append(cfgs.setdefault(key, len(cfgs)))
    kc = np.arange(w)[:, None]
    qc = np.arange(w)[None, :]
    wstart = np.clip(qc - NA_COLS // 2, 0, w - NA_COLS)
    col_ok = (kc >= wstart) & (kc < wstart + NA_COLS)
    dcol = np.clip(kc - qc, -(NA_COLS - 1), NA_COLS - 1) + NA_COLS - 1
    onehot = (dcol[None] == np.arange(2 * NA_COLS - 1)[:, None, None]).astype(np.float32)
    tiles = jnp.einsum('hrd,dkq->hrkq', rpb.astype(F32), jnp.asarray(onehot), precision=lax.Precision.HIGHEST)
    tiles = jnp.where(col_ok[None, None], tiles * LOG2E, NEG_INF)
    masked = jnp.full((rpb.shape[0], w, w), NEG_INF, F32)
    tabs = []
    for key in cfgs:
        off, rsr = key[0], key[1:]
        rows = []
        for c in range(NA_KROWS):
            cols = []
            for a in range(NA_QROWS):
                assert 0 <= rsr[a] and rsr[a] + NA_ROWS <= NA_KROWS
                row_ok = rsr[a] <= c < rsr[a] + NA_ROWS
                cols.append(tiles[:, c - off - a + NA_ROWS - 1] if row_ok else masked)
            rows.append(jnp.concatenate(cols, axis=2))
        tabs.append(jnp.concatenate(rows, axis=1))
    return jnp.asarray(np.array(cfg_of, np.int32)), jnp.stack(tabs)


def _attn_b_kernel(cfg_ref, qt_ref, k_ref, vt_ref, kc_ref, vct_ref, bias_ref, o_ref, *, n_rows, heads):
    del cfg_ref
    w = GRID_W
    dh = HEAD_DIM
    rb = pl.program_id(2)
    rs2 = jnp.clip(rb - NA_ROWS // (2 * NA_QROWS), 0, (n_rows - NA_KROWS) // NA_QROWS)
    start = pl.multiple_of(rs2 * (NA_QROWS * w), NA_QROWS * w)
    span = NA_KROWS * w
    scores, vts = [], []
    for h in range(heads):
        qt = qt_ref[0, h]
        kw = k_ref[0, pl.ds(start, span), h * dh:(h + 1) * dh]
        s_loc = _dot(kw, qt) + bias_ref[0, h]
        s_ctx = _dot(kc_ref[0, :, h * dh:(h + 1) * dh], qt)
        scores.append([s_loc, s_ctx])
        vts.append([vt_ref[0, h, :, pl.ds(start, span)], vct_ref[0, h]])
    outs = _softmax_pv_t(scores, vts, [None] * heads)
    o_ref[0] = jnp.concatenate(outs, axis=0).T.astype(BF16)


def _attn_b(qt, k, vt, kc, vct, rpb):
    b, hq, dh, s = qt.shape
    nc = kc.shape[1]
    w = GRID_W
    n_rows = s // w
    assert s % w == 0 and n_rows >= NA_KROWS and n_rows % NA_QROWS == 0
    cfg_of, table = _na_bias_tables(rpb, n_rows)
    hg = NA_HEAD_GROUP
    tq = NA_QROWS * w
    span = NA_KROWS * w
    grid_spec = pltpu.PrefetchScalarGridSpec(
        num_scalar_prefetch=1,
        grid=(b, hq // hg, n_rows // NA_QROWS),
        in_specs=[
            pl.BlockSpec((1, hg, dh, tq), lambda bi, g, r, cfg: (bi, g, 0, r)),
            pl.BlockSpec((1, s, hg * dh), lambda bi, g, r, cfg: (bi, 0, g), pipeline_mode=pl.Buffered(1)),
            pl.BlockSpec((1, hg, dh, s), lambda bi, g, r, cfg: (bi, g, 0, 0), pipeline_mode=pl.Buffered(1)),
            pl.BlockSpec((1, nc, hg * dh), lambda bi, g, r, cfg: (bi, 0, g)),
            pl.BlockSpec((1, hg, dh, nc), lambda bi, g, r, cfg: (bi, g, 0, 0)),
            pl.BlockSpec((1, hg, span, tq), lambda bi, g, r, cfg: (cfg[r], g, 0, 0)),
        ],
        out_specs=pl.BlockSpec((1, tq, hg * dh), lambda bi, g, r, cfg: (bi, r, g)),
    )
    return pl.pallas_call(
        functools.partial(_attn_b_kernel, n_rows=n_rows, heads=hg),
        out_shape=jax.ShapeDtypeStruct((b, s, hq * dh), BF16),
        grid_spec=grid_spec,
        compiler_params=_params(("parallel", "parallel", "arbitrary"), VMEM_LIMIT),
        name="attn_neighbourhood",
    )(cfg_of, qt, k, vt, kc, vct, table)


def _out_kernel(o_ref, wo_ref, x_ref, g1_ref, g_ref, sh_ref, sc_ref, wr_ref, xn_ref, h_ref, aff_ref, *, tiled):
    y = _dot(o_ref[0], wo_ref[...])
    x = _load_tokens(x_ref, o_ref.shape[1]) if tiled else x_ref[0]
    xn = x + g1_ref[0] * y
    _store_tokens(xn_ref, xn)
    h = _rms_mod(xn, g_ref[...], sh_ref[0], sc_ref[0])
    _store_tokens(h_ref, h)
    logits = _dot3(wr_ref[...], h, _NT)
    z = jnp.exp(logits - logits.max(axis=0, keepdims=True))
    aff_ref[0] = z / z.sum(axis=0, keepdims=True)


def _out_proj(o, w_o, x, g1, g, shift, scale, w_router, dims=None):
    tiled = dims is not None
    bx, n, d = dims if tiled else x.shape
    assert d % (8 * LANES) == 0
    da = o.shape[2]
    e = w_router.shape[1]
    tm = min(512, n)
    bm = g1.shape[0]
    mod_map = (lambda b, t: (b, 0, 0)) if bm == bx else (lambda b, t: (0, 0, 0))
    const2 = lambda b, t: (0, 0)
    mod_spec = pl.BlockSpec((1, 1, d), mod_map)
    tiled_shape = jax.ShapeDtypeStruct((bx * n * (d // LANES), LANES), F32)
    return pl.pallas_call(
        functools.partial(_out_kernel, tiled=tiled),
        out_shape=[tiled_shape, tiled_shape, jax.ShapeDtypeStruct((bx, e, n), F32)],
        grid=(bx, n // tm),
        in_specs=[
            pl.BlockSpec((1, tm, da), lambda b, t: (b, t, 0)),
            pl.BlockSpec((da, d), const2),
            _token_spec(tm, d, n // tm, tiled),
            mod_spec,
            pl.BlockSpec((1, d), const2),
            mod_spec,
            mod_spec,
            pl.BlockSpec((e, d), const2),
        ],
        out_specs=[_token_spec(tm, d, n // tm, True),
                   _token_spec(tm, d, n // tm, True),
                   pl.BlockSpec((1, e, tm), lambda b, t: (b, 0, t))],
        compiler_params=_params(("parallel", "parallel"), VMEM_LIMIT),
        name="out_proj_router",
    )(o, w_o.astype(BF16), x, g1, g.reshape(1, d), shift, scale, w_router.T)


def _untile_kernel(x_ref, o_ref):
    o_ref[0] = _load_tokens(x_ref, o_ref.shape[1])


def _untile(x, dims):
    bx, n, d = dims
    tm = min(512, n)
    return pl.pallas_call(
        _untile_kernel,
        out_shape=jax.ShapeDtypeStruct(dims, F32),
        grid=(bx, n // tm),
        in_specs=[_token_spec(tm, d, n // tm, True)],
        out_specs=_token_spec(tm, d, n // tm, False),
        compiler_params=_params(("parallel", "parallel")),
        name="untile_tokens",
    )(x)


ROUTE_ROW_ALIGN = 16
ROUTE_BISECT_STEPS = 40
MOE_TILE_ROWS = 1024
MOE_ROW_CHUNK = 512
_TN = (((0,), (0,)), ((), ()))


def _route_kernel(aff_ref, tri_ref, lmat_ref, idx_ref, gate_ref, *, cap):
    n_exp, n_rows, lanes = aff_ref.shape[1:]
    aff = aff_ref[0]

    def reduce_tokens(x, op):
        return op(op(x, axis=1, keepdims=True), axis=2, keepdims=True)

    def count_ge(t):
        return reduce_tokens(jnp.where(aff >= t, 1.0, 0.0), jnp.sum)

    def max_below(t):
        return reduce_tokens(jnp.where(aff < t, aff, -1.0), jnp.max)

    def bisect(_, lohi):
        lo, hi = lohi
        mid = 0.5 * (lo + hi)
        ok = count_ge(mid) >= cap
        return jnp.where(ok, mid, lo), jnp.where(ok, hi, mid)

    top = reduce_tokens(aff, jnp.max)
    _, hi = lax.fori_loop(0, ROUTE_BISECT_STEPS, bisect, (jnp.zeros_like(top), 2.0 * top + 1e-30))

    def short(t):
        return jnp.max(jnp.where(count_ge(t) < cap, 1.0, 0.0)) > 0.5

    def walk(t):
        return jnp.where(count_ge(t) >= cap, t, max_below(t))

    thr = lax.while_loop(short, walk, max_below(hi))
    above = aff > thr
    equal = aff == thr
    need = cap - reduce_tokens(jnp.where(above, 1.0, 0.0), jnp.sum)

    def prefix(x3):
        x2 = x3.reshape(n_exp * n_rows, lanes)
        inrow = _dot(x2.astype(BF16), tri_ref[...])
        tot = inrow[:, lanes - 1:lanes]
        offs = _dot(lmat_ref[...], jnp.broadcast_to(tot, inrow.shape).astype(BF16))
        return inrow, offs, tot

    eq_f = jnp.where(equal, 1.0, 0.0)
    eq_in, eq_off, _ = prefix(eq_f)
    eq_rank = (eq_in + eq_off).reshape(n_exp, n_rows, lanes) - eq_f
    sel_f = jnp.where(above | (equal & (eq_rank < need)), 1.0, 0.0)
    inrow, offs, tot = prefix(sel_f)
    sel2 = sel_f.reshape(n_exp * n_rows, lanes)
    aff2 = aff.reshape(n_exp * n_rows, lanes)
    a_hi = aff2.astype(BF16)
    r1 = aff2 - a_hi.astype(F32)
    a_mid = r1.astype(BF16)
    a_lo = (r1 - a_mid.astype(F32)).astype(BF16)

    slot = lax.broadcasted_iota(jnp.int32, (1, cap), 1).astype(F32)
    row_id = lax.broadcasted_iota(jnp.int32, (n_rows, 1), 0).astype(F32)
    lane_id = lax.broadcasted_iota(jnp.int32, (lanes, 1), 0).astype(F32)
    for ei in range(n_exp):
        rows = slice(ei * n_rows, (ei + 1) * n_rows)
        off_e = offs[rows, 0:1]
        tot_e = tot[rows]
        in_row = jnp.where((off_e <= slot) & (slot < off_e + tot_e), 1.0, 0.0)
        row_base = (in_row * off_e).sum(axis=0, keepdims=True)
        row_of = (in_row * row_id).sum(axis=0, keepdims=True)
        in_row_b = in_row.astype(BF16)
        pick = lambda v: _dot(v, in_row_b, _TN)
        hit = (pick(inrow[rows].astype(BF16)) == slot + 1.0 - row_base) & (pick(sel2[rows].astype(BF16)) > 0.5)
        hit_f = jnp.where(hit, 1.0, 0.0)
        aff_rows = (pick(a_hi[rows]) + pick(a_mid[rows])) + pick(a_lo[rows])
        lane_of = (hit_f * lane_id).sum(axis=0, keepdims=True)
        idx_ref[0, ei:ei + 1, :] = (row_of * lanes + lane_of).astype(jnp.int32)
        gate_ref[0, ei:ei + 1, :] = (hit_f * aff_rows).sum(axis=0, keepdims=True)


def _route(aff, cap):
    bx, e, n = aff.shape
    unit = LANES * ROUTE_ROW_ALIGN
    n_pad = -(-n // unit) * unit
    if n_pad != n:
        aff = jnp.pad(aff, ((0, 0), (0, 0), (0, n_pad - n)), constant_values=-1.0)
    n_rows = n_pad // LANES
    tri = np.triu(np.ones((LANES, LANES), np.float32))
    r = np.arange(e * n_rows)
    lmat = ((r[:, None] // n_rows == r[None, :] // n_rows) & (r[None, :] < r[:, None])).astype(np.float32)
    return pl.pallas_call(
        functools.partial(_route_kernel, cap=cap),
        out_shape=[jax.ShapeDtypeStruct((bx, e, cap), jnp.int32), jax.ShapeDtypeStruct((bx, e, cap), F32)],
        grid=(bx,),
        in_specs=[
            pl.BlockSpec((1, e, n_rows, LANES), lambda b: (b, 0, 0, 0)),
            pl.BlockSpec((LANES, LANES), lambda b: (0, 0)),
            pl.BlockSpec((e * n_rows, e * n_rows), lambda b: (0, 0)),
        ],
        out_specs=[pl.BlockSpec((1, e, cap), lambda b: (b, 0, 0)), pl.BlockSpec((1, e, cap), lambda b: (b, 0, 0))],
        compiler_params=_params(("parallel",)),
        name="moe_route",
    )(aff.reshape(bx, e, n_rows, LANES), jnp.asarray(tri, BF16), jnp.asarray(lmat, BF16))


def _moe_kernel(idx_ref, h_hbm, gate_ref, g2_ref, wg_ref, wu_ref, wd_ref, x_hbm, out_hbm,
                xbuf, abuf, xb, gsem, asem, ssem, *, rows, cap, chunk, n_tiles, n_ff):
    del x_hbm
    tr = xbuf.shape[1] // rows
    e, p, f = pl.program_id(0), pl.program_id(1), pl.program_id(2)
    n_steps = pl.num_programs(0) * n_tiles
    k = e * n_tiles + p
    slot = lax.rem(k, 2)
    lag = 2 if n_tiles >= 2 else 1

    def row_copy(step, i, src, dst, dslot, sem, gather, priority):
        tok = idx_ref[step * rows + i]
        hbm_row = src.at[pl.ds(pl.multiple_of(tok * tr, tr), tr)]
        buf_row = dst.at[dslot, pl.ds(pl.multiple_of(i * tr, tr), tr)]
        cp = (pltpu.make_async_copy(hbm_row, buf_row, sem.at[dslot]) if gather
              else pltpu.make_async_copy(buf_row, hbm_row, sem.at[dslot]))
        cp.start(priority=priority)

    def row_copies(step, src, dst, dslot, sem, gather):
        def body(i, carry):
            for u in range(2):
                row_copy(step, 2 * i + u, src, dst, dslot, sem, gather, u)
            return carry

        lax.fori_loop(0, rows // 2, body, 0, unroll=4)

    def row_copies_inline(step, lo, count, src, dst, dslot, sem, gather):
        for u in range(count):
            row_copy(step, lo + u, src, dst, dslot, sem, gather, u % 2)

    def wait_rows(buf, bslot, sem):
        pltpu.make_async_copy(buf.at[bslot], buf.at[bslot], sem.at[bslot]).wait()

    na = abuf.shape[0]
    pipelined = na >= 3
    acur = lax.rem(k, na) if pipelined else slot
    anext, aprev = lax.rem(k + 1, na), lax.rem(k + na - 1, na)

    @pl.when(f == 0)
    def _begin():
        @pl.when(k == 0)
        def _():
            row_copies(k, h_hbm, xbuf, slot, gsem, True)
            if pipelined:
                row_copies(k, out_hbm, abuf, acur, asem, True)
                row_copies(k, out_hbm, abuf, aprev, asem, True)
                wait_rows(abuf, aprev, asem)

        if pipelined:
            @pl.when(k == 1)
            def _():
                wait_rows(abuf, na - 1, ssem)

            @pl.when(k >= na - 1)
            def _():
                wait_rows(abuf, anext, ssem)
        else:
            @pl.when(k >= lag)
            def _():
                wait_rows(abuf, lax.rem(k - lag, 2), ssem)

            row_copies(k, out_hbm, abuf, slot, asem, True)
        wait_rows(xbuf, slot, gsem)
        for j in range(tr):
            xb[:, j * LANES:(j + 1) * LANES] = xbuf[slot, pl.ds(j, rows, stride=tr), :].astype(BF16)
        if pipelined:
            wait_rows(abuf, acur, asem)

    wg = wg_ref[0, 0].astype(BF16)
    wu = wu_ref[0, 0].astype(BF16)
    wd = wd_ref[0, 0].astype(BF16)
    n_chunks = rows // chunk
    ahead = rows // (n_ff * n_chunks)
    nxt = jnp.minimum(k + 1, n_steps - 1)
    prv = jnp.maximum(k - 1, 0)
    for c in range(n_chunks):
        rs = slice(c * chunk, (c + 1) * chunk)
        lo = (f * n_chunks + c) * ahead
        x = xb[rs]
        a = _dot(x, wg)
        row_copies_inline(nxt, lo, ahead, h_hbm, xbuf, 1 - slot, gsem, True)
        u = _dot(x, wu)
        if pipelined:
            row_copies_inline(nxt, lo, ahead, out_hbm, abuf, anext, asem, True)
        hcat = ((a * jax.nn.sigmoid(a)) * u).astype(BF16)
        y = _dot(hcat, wd) * (gate_ref[0, rs] * g2_ref[c * chunk // cap])
        if pipelined:
            row_copies_inline(prv, lo, ahead, out_hbm, abuf, aprev, ssem, False)
        elif c == 0:
            @pl.when(f == 0)
            def _():
                wait_rows(abuf, slot, asem)
        for j in range(tr):
            abuf[acur, pl.ds(c * chunk * tr + j, chunk, stride=tr), :] += y[:, j * LANES:(j + 1) * LANES]

    @pl.when(f == pl.num_programs(2) - 1)
    def _finish():
        if not pipelined:
            row_copies(k, out_hbm, abuf, slot, ssem, False)

        @pl.when(k == n_steps - 1)
        def _():
            wait_rows(xbuf, 1 - slot, gsem)
            if pipelined:
                wait_rows(abuf, anext, asem)
                row_copies(k, out_hbm, abuf, acur, ssem, False)
                for back in range(na - 1):
                    wait_rows(abuf, lax.rem(k + na - back, na), ssem)
            else:
                if lag == 2:
                    @pl.when(k >= 1)
                    def _():
                        wait_rows(abuf, 1 - slot, ssem)
                wait_rows(abuf, slot, ssem)


def _ec_moe(x, h, aff, g2, layer, w_gate, w_up, w_down, dims):
    bx, n, d = dims
    tr = d // LANES
    e = aff.shape[1]
    ff = w_gate.shape[3]
    cap = EC_CAPACITY * n // e
    idx, gate = _route(aff, cap)
    rows_of = (idx + (jnp.arange(bx, dtype=jnp.int32) * n)[:, None, None])
    rows_of = jnp.transpose(rows_of, (1, 0, 2)).reshape(e * bx * cap)
    gate_c = jnp.transpose(gate, (1, 0, 2)).reshape(e, bx * cap, 1)
    g2 = jnp.broadcast_to(g2, (bx, 1, d))
    nb = max(1, min(bx, MOE_TILE_ROWS // cap))
    assert bx % nb == 0
    rows = nb * cap
    n_tiles = bx // nb
    chunk = min(MOE_ROW_CHUNK, cap)
    assert cap % chunk == 0
    tf = min(512, ff)
    assert rows % ((ff // tf) * (rows // chunk)) == 0
    n_abuf = min(4, n_tiles) if n_tiles >= 3 else 2
    grid_spec = pltpu.PrefetchScalarGridSpec(
        num_scalar_prefetch=1,
        grid=(e, n_tiles, ff // tf),
        in_specs=[
            pl.BlockSpec(memory_space=pl.ANY),
            pl.BlockSpec((1, rows, 1), lambda ei, p, f, idx: (ei, p, 0)),
            pl.BlockSpec((nb, 1, d), lambda ei, p, f, idx: (p, 0, 0)),
            pl.BlockSpec((1, 1, d, tf), lambda ei, p, f, idx: (layer, ei, 0, f)),
            pl.BlockSpec((1, 1, d, tf), lambda ei, p, f, idx: (layer, ei, 0, f)),
            pl.BlockSpec((1, 1, tf, d), lambda ei, p, f, idx: (layer, ei, f, 0)),
            pl.BlockSpec(memory_space=pl.ANY),
        ],
        out_specs=pl.BlockSpec(memory_space=pl.ANY),
        scratch_shapes=[
            pltpu.VMEM((2, rows * tr, LANES), F32),
            pltpu.VMEM((n_abuf, rows * tr, LANES), F32),
            pltpu.VMEM((rows, d), BF16),
            pltpu.SemaphoreType.DMA((2,)),
            pltpu.SemaphoreType.DMA((n_abuf,)),
            pltpu.SemaphoreType.DMA((n_abuf,)),
        ],
    )
    return pl.pallas_call(
        functools.partial(_moe_kernel, rows=rows, cap=cap, chunk=chunk, n_tiles=n_tiles, n_ff=ff // tf),
        out_shape=jax.ShapeDtypeStruct(x.shape, F32),
        grid_spec=grid_spec,
        input_output_aliases={7: 0},
        compiler_params=_params(("arbitrary", "arbitrary", "arbitrary"), VMEM_LIMIT),
        name="moe_ffn",
    )(rows_of, h, gate_c, g2, w_gate, w_up, w_down, x)


def kernel(x, c, ctx, c_ctx, w_mod, b_mod, norm_mix, norm_ffn, a_w_qkv, a_q_gain, a_k_gain, a_sink, a_w_o,
           b_w_qkv, b_q_gain, b_k_gain, b_rpb, b_w_o, moe_router, moe_w_gate, moe_w_up, moe_w_down):
    b, s, d = x.shape
    depth = w_mod.shape[0]
    pad = (-(b + 1)) % 8
    cvec = jnp.concatenate([c, c_ctx[None], jnp.zeros((pad, d), F32)], axis=0)
    mods = _modulation(cvec, w_mod, b_mod)
    rope_tabs = _rope_tables(s)
    xl, xc = x, ctx
    for i in range(depth):
        last = i == depth - 1
        til_l = dict(dims=x.shape) if i > 0 else {}
        til_c = dict(dims=ctx.shape) if i > 0 else {}
        ml =[m.reshape(b, 1, d) for m in jnp.split(mods[i, :b], 6, axis=-1)]
        mc = [m.reshape(1, 1, d) for m in jnp.split(mods[i, b:b + 1], 6, axis=-1)]
        sh1_l, sc1_l, g1_l, sh2_l, sc2_l, g2_l = ml
        sh1_c, sc1_c, g1_c, sh2_c, sc2_c, g2_c = mc
        j = i // N_MIXERS
        if i % N_MIXERS == 0:
            ql, kl, vl = _project(xl, norm_mix[i], sh1_l, sc1_l, a_w_qkv[j], a_q_gain[j], a_k_gain[j],
                                  A_HEADS, A_KV_HEADS, rope_tabs, **til_l)
            qc, kc, vc = _project(xc, norm_mix[i], sh1_c, sc1_c, a_w_qkv[j], a_q_gain[j], a_k_gain[j],
                                  A_HEADS, A_KV_HEADS, **til_c)
            ol = _attn_a(ql, kl, vl, kc, vc, a_sink[j])
            oc = None if last else _attn_ctx(qc, kc, vc, a_sink[j])
            w_o = a_w_o[j]
        else:
            ql, kl, vl = _project(xl, norm_mix[i], sh1_l, sc1_l, b_w_qkv[j], b_q_gain[j], b_k_gain[j],
                                  B_HEADS, B_HEADS, **til_l)
            qc, kc, vc = _project(xc, norm_mix[i], sh1_c, sc1_c, b_w_qkv[j], b_q_gain[j], b_k_gain[j],
                                  B_HEADS, B_HEADS, **til_c)
            ol = _attn_b(ql, kl, vl, kc, vc, b_rpb[j])
            oc = None if last else _attn_ctx(qc, kc, vc, None)
            w_o = b_w_o[j]
        xl, hl, aff_l = _out_proj(ol, w_o, xl, g1_l, norm_ffn[i], sh2_l, sc2_l, moe_router[i], **til_l)
        xl = _ec_moe(xl, hl, aff_l, g2_l, i, moe_w_gate, moe_w_up, moe_w_down, x.shape)
        if not last:
            xc, hc, aff_c = _out_proj(oc, w_o, xc, g1_c, norm_ffn[i], sh2_c, sc2_c, moe_router[i], **til_c)
            xc = _ec_moe(xc, hc, aff_c, g2_c, i, moe_w_gate, moe_w_up, moe_w_down, ctx.shape)
    return _untile(xl, x.shape)
```

```python
import functools

import numpy as np
import jax
import jax.numpy as jnp
from jax import lax
from jax.experimental import pallas as pl
from jax.experimental.pallas import tpu as pltpu

F32 = jnp.float32
BF16 = jnp.bfloat16

GRID_W = 64
HEAD_DIM = 64
A_HEADS = 16
A_KV_HEADS = 4
A_WINDOW = 128
A_BLOCK = 128
B_HEADS = 16
NA_ROWS = 8
NA_COLS = 16
N_EXPERTS = 16
EC_CAPACITY = 2
ROPE_BASE = 10000.0
EPS = 1e-6
NEG_INF = -1e30
ATTN_SCALE = HEAD_DIM ** -0.5
LOG2E = 1.4426950408889634
N_MIXERS = 2

NA_QROWS = 2
NA_KROWS = NA_ROWS + NA_QROWS
NA_HEAD_GROUP = 16

LANES = 128
VMEM_LIMIT = 56 * 1024 * 1024

_NT = (((1,), (1,)), ((), ()))
_NN = (((1,), (0,)), ((), ()))


def _dot(a, b, dims=_NN):
    return lax.dot_general(a, b, dims, preferred_element_type=F32)


def _split(a):
    hi = a.astype(BF16)
    lo = (a - hi.astype(F32)).astype(BF16)
    return hi, lo


def _dot3(a, b, dims=_NN):
    ah, al = _split(a)
    bh, bl = _split(b)
    return _dot(ah, bh, dims) + (_dot(ah, bl, dims) + _dot(al, bh, dims))


def _params(sem, vmem=None):
    return pltpu.CompilerParams(dimension_semantics=sem, vmem_limit_bytes=vmem)


def _mod_kernel(c_ref, w_ref, b_ref, o_ref):
    c = c_ref[...]
    a = c * jax.nn.sigmoid(c)
    o_ref[0] = _dot3(a, w_ref[0]) + b_ref[0]


def _modulation(cvec, w_mod, b_mod):
    depth, d, n = w_mod.shape
    rows = cvec.shape[0]
    tn = 1024
    return pl.pallas_call(
        _mod_kernel,
        out_shape=jax.ShapeDtypeStruct((depth, rows, n), F32),
        grid=(depth, n // tn),
        in_specs=[
            pl.BlockSpec((rows, d), lambda i, j: (0, 0)),
            pl.BlockSpec((1, d, tn), lambda i, j: (i, 0, j)),
            pl.BlockSpec((1, 1, tn), lambda i, j: (i, 0, j)),
        ],
        out_specs=pl.BlockSpec((1, rows, tn), lambda i, j: (i, 0, j)),
        compiler_params=_params(("parallel", "parallel")),
        name="modulation",
    )(cvec, w_mod, b_mod.reshape(depth, 1, n))


def _rms_mod(x, g, shift, scale):
    y = x * lax.rsqrt(jnp.mean(x * x, axis=-1, keepdims=True) + EPS)
    return (y * g) * (1.0 + scale) + shift


def _load_tokens(ref, tm):
    chunks = ref.shape[0] // tm
    return jnp.concatenate([ref[pl.ds(j, tm, stride=chunks), :] for j in range(chunks)], axis=1)


def _store_tokens(ref, val):
    tm, d = val.shape
    chunks = d // LANES
    for j in range(chunks):
        ref[pl.ds(j, tm, stride=chunks), :] = val[:, j * LANES:(j + 1) * LANES]


def _token_spec(tm, d, n_tiles, tiled):
    if tiled:
        return pl.BlockSpec((tm * (d // LANES), LANES), lambda b, t: (b * n_tiles + t, 0))
    return pl.BlockSpec((1, tm, d), lambda b, t: (b, t, 0))


def _proj_kernel(*refs, n_q, n_kv, rope, tiled):
    if rope:
        x_ref, g_ref, sh_ref, sc_ref, wt_ref, cq_ref, ck_ref, sq_ref, sk_ref, qt_ref, k_ref, vt_ref = refs
    else:
        x_ref, g_ref, sh_ref, sc_ref, wt_ref, cq_ref, ck_ref, qt_ref, k_ref, vt_ref = refs
        sq_ref = sk_ref = None
    dh = HEAD_DIM
    x = _load_tokens(x_ref, qt_ref.shape[3]) if tiled else x_ref[0]
    h = _rms_mod(x, g_ref[...], sh_ref[0], sc_ref[0]).astype(BF16)
    res = _dot(wt_ref[...], h, _NT)

    def head(j, c_ref, s_ref):
        t = res[j * dh:(j + 1) * dh]
        tn = t * lax.rsqrt(jnp.mean(t * t, axis=0, keepdims=True) + EPS)
        out = tn * c_ref[...]
        if rope:
            q4 = dh // 4
            partner = jnp.concatenate([tn[q4:2 * q4], tn[:q4], tn[3 * q4:], tn[2 * q4:3 * q4]], axis=0)
            out = out + partner * s_ref[...]
        return out

    for j in range(n_q):
        qt_ref[0, j] = head(j, cq_ref, sq_ref).astype(BF16)
    ks = [head(n_q + j, ck_ref, sk_ref) for j in range(n_kv)]
    k_ref[0] = jnp.concatenate(ks, axis=0).T.astype(BF16)
    v0 = (n_q + n_kv) * dh
    for j in range(n_kv):
        vt_ref[0, j] = res[v0 + j * dh:v0 + (j + 1) * dh].astype(BF16)


def _rope_perm():
    d = np.arange(HEAD_DIM)
    return np.where((d % 32) < 16, d + 16, d - 16)


def _rope_tables(n_tok):
    pos = jnp.arange(n_tok)
    row = (pos // GRID_W).astype(F32)
    col = (pos % GRID_W).astype(F32)
    n_freq = HEAD_DIM // 4
    inv = ROPE_BASE ** (-jnp.arange(n_freq, dtype=F32) / n_freq)
    ang_r = row[:, None] * inv
    ang_c = col[:, None] * inv
    cos_t = jnp.concatenate([jnp.cos(ang_r), jnp.cos(ang_r), jnp.cos(ang_c), jnp.cos(ang_c)], axis=-1)
    sin_t = jnp.concatenate([-jnp.sin(ang_r), jnp.sin(ang_r), -jnp.sin(ang_c), jnp.sin(ang_c)], axis=-1)
    return cos_t.T, sin_t.T


def _project(x, g, shift, scale, w, q_gain, k_gain, n_q, n_kv, rope_tabs=None, dims=None):
    tiled = dims is not None
    bx, n, d = dims if tiled else x.shape
    dh = HEAD_DIM
    rope = rope_tabs is not None
    tm = min(512, n)
    qg = (q_gain.astype(F32) * (ATTN_SCALE * LOG2E)).reshape(dh, 1)
    kg = k_gain.astype(F32).reshape(dh, 1)
    wt = w.T.astype(BF16)
    n_cols = wt.shape[0]
    bm = shift.shape[0]
    mod_map = (lambda b, t: (b, 0, 0)) if bm == bx else (lambda b, t: (0, 0, 0))
    const2 = lambda b, t: (0, 0)
    tok_map = lambda b, t: (0, t)
    in_specs = [
        _token_spec(tm, d, n // tm, tiled),
        pl.BlockSpec((1, d), const2),
        pl.BlockSpec((1, 1, d), mod_map),
        pl.BlockSpec((1, 1, d), mod_map),
        pl.BlockSpec((n_cols, d), const2),
    ]
    args = [x, g.reshape(1, d), shift, scale, wt]
    if rope:
        cos_t, sin_t = rope_tabs
        perm = _rope_perm()
        args += [qg * cos_t, kg * cos_t, qg[perm] * sin_t, kg[perm] * sin_t]
        in_specs += [pl.BlockSpec((dh, tm), tok_map)] * 4
    else:
        args += [jnp.broadcast_to(qg, (dh, tm)), jnp.broadcast_to(kg, (dh, tm))]
        in_specs += [pl.BlockSpec((dh, tm), const2)] * 2
    out_shape = [jax.ShapeDtypeStruct((bx, n_q, dh, n), BF16),
                 jax.ShapeDtypeStruct((bx, n, n_kv * dh), BF16),
                 jax.ShapeDtypeStruct((bx, n_kv, dh, n), BF16)]
    out_specs = [pl.BlockSpec((1, n_q, dh, tm), lambda b, t: (b, 0, 0, t)),
                 pl.BlockSpec((1, tm, n_kv * dh), lambda b, t: (b, t, 0)),
                 pl.BlockSpec((1, n_kv, dh, tm), lambda b, t: (b, 0, 0, t))]
    return pl.pallas_call(
        functools.partial(_proj_kernel, n_q=n_q, n_kv=n_kv, rope=rope, tiled=tiled),
        out_shape=out_shape,
        grid=(bx, n // tm),
        in_specs=in_specs,
        out_specs=out_specs,
        compiler_params=_params(("parallel", "parallel"), VMEM_LIMIT),
        name="qkv_proj_rope" if rope else "qkv_proj",
    )(*args)


def _softmax_pv_t(scores, vts, sinks):
    ms = []
    for parts, sink in zip(scores, sinks):
        m = parts[0].max(axis=0, keepdims=True)
        for s in parts[1:]:
            m = jnp.maximum(m, s.max(axis=0, keepdims=True))
        ms.append(m if sink is None else jnp.maximum(m, sink))
    probs, dens = [], []
    for parts, m, sink in zip(scores, ms, sinks):
        ps = [jnp.exp2(s - m) for s in parts]
        den = ps[0].sum(axis=0, keepdims=True)
        for p in ps[1:]:
            den = den + p.sum(axis=0, keepdims=True)
        dens.append(den if sink is None else den + jnp.exp2(sink - m))
        probs.append([p.astype(BF16) for p in ps])
    outs = []
    for ps, vt_parts, den in zip(probs, vts, dens):
        acc = _dot(vt_parts[0], ps[0])
        for vt, p in zip(vt_parts[1:], ps[1:]):
            acc = acc + _dot(vt, p)
        outs.append(acc / den)
    return outs


def _attn_a_kernel(qt_ref, k_ref, vt_ref, kc_ref, vct_ref, sink_ref, mask_ref, o_ref, *, seq, n_kv, group):
    blk = A_BLOCK
    span = 3 * blk
    dh = HEAD_DIM
    i = pl.program_id(1)
    start = pl.multiple_of(jnp.clip((i - 1) * blk, 0, seq - span), blk)
    scores, vts = [], []
    for h in range(n_kv):
        qt = jnp.concatenate([qt_ref[0, h * group + g] for g in range(group)], axis=1)
        kw = k_ref[0, pl.ds(start, span), h * dh:(h + 1) * dh]
        s_loc = _dot(kw, qt) + mask_ref[0]
        s_ctx = _dot(kc_ref[0, :, h * dh:(h + 1) * dh], qt)
        scores.append([s_loc, s_ctx])
        vts.append([vt_ref[0, h, :, pl.ds(start, span)], vct_ref[0, h]])
    ots = _softmax_pv_t(scores, vts, [sink_ref[h] for h in range(n_kv)])
    outs = [ot[:, g * blk:(g + 1) * blk] for ot in ots for g in range(group)]
    o_ref[0] = jnp.concatenate(outs, axis=0).T.astype(BF16)


def _sink_rows(sink, n_kv, group, cols):
    s = (sink.astype(F32) * LOG2E).reshape(n_kv, 1, group, 1)
    return jnp.broadcast_to(s, (n_kv, 1, group, cols)).reshape(n_kv, 1, group * cols)


def _attn_a(qt, k, vt, kc, vct, sink):
    b, hq, dh, s = qt.shape
    n_kv = vt.shape[1]
    group = hq // n_kv
    nc = kc.shape[1]
    blk = A_BLOCK
    n_blk = s // blk
    assert s % blk == 0 and n_blk >= 3
    off = (np.arange(3) * blk)[:, None, None]
    rel = off + (np.arange(group * blk) % blk)[None, None, :] - np.arange(3 * blk)[None, :, None]
    mask = jnp.asarray(np.where(np.abs(rel) <= A_WINDOW, 0.0, NEG_INF).astype(np.float32))
    return pl.pallas_call(
        functools.partial(_attn_a_kernel, seq=s, n_kv=n_kv, group=group),
        out_shape=jax.ShapeDtypeStruct((b, s, hq * dh), BF16),
        grid=(b, s // blk),
        in_specs=[
            pl.BlockSpec((1, hq, dh, blk), lambda bi, i: (bi, 0, 0, i)),
            pl.BlockSpec((1, s, n_kv * dh), lambda bi, i: (bi, 0, 0)),
            pl.BlockSpec((1, n_kv, dh, s), lambda bi, i: (bi, 0, 0, 0)),
            pl.BlockSpec((1, nc, n_kv * dh), lambda bi, i: (bi, 0, 0)),
            pl.BlockSpec((1, n_kv, dh, nc), lambda bi, i: (bi, 0, 0, 0)),
            pl.BlockSpec((n_kv, 1, group * blk), lambda bi, i: (0, 0, 0)),
            pl.BlockSpec((1, 3 * blk, group * blk),
                         lambda bi, i: (jnp.where(i == 0, 0, jnp.where(i == n_blk - 1, 2, 1)), 0, 0)),
        ],
        out_specs=pl.BlockSpec((1, blk, hq * dh), lambda bi, i: (bi, i, 0)),
        compiler_params=_params(("parallel", "arbitrary"), VMEM_LIMIT),
        name="attn_window",
    )(qt, k, vt, kc, vct, _sink_rows(sink, n_kv, group, blk), mask)


def _attn_ctx_kernel(qt_ref, k_ref, vt_ref, sink_ref, o_ref, *, n_kv, group, use_sink):
    dh = HEAD_DIM
    nq = qt_ref.shape[3]
    scores, vts = [], []
    for h in range(n_kv):
        qt = jnp.concatenate([qt_ref[0, h * group + g] for g in range(group)], axis=1)
        scores.append([_dot(k_ref[0, :, h * dh:(h + 1) * dh], qt)])
        vts.append([vt_ref[0, h]])
    ots = _softmax_pv_t(scores, vts, [sink_ref[h] if use_sink else None for h in range(n_kv)])
    outs = [ot[:, g * nq:(g + 1) * nq] for ot in ots for g in range(group)]
    o_ref[0] = jnp.concatenate(outs, axis=0).T.astype(BF16)


def _attn_ctx(qt, k, vt, sink):
    b, hq, dh, n = qt.shape
    n_kv = vt.shape[1]
    group = hq // n_kv
    use_sink = sink is not None
    sink_r = _sink_rows(sink if use_sink else jnp.zeros((hq,), F32), n_kv, group, n)
    return pl.pallas_call(
        functools.partial(_attn_ctx_kernel, n_kv=n_kv, group=group, use_sink=use_sink),
        out_shape=jax.ShapeDtypeStruct((b, n, hq * dh), BF16),
        grid=(b,),
        in_specs=[
            pl.BlockSpec((1, hq, dh, n), lambda bi: (bi, 0, 0, 0)),
            pl.BlockSpec((1, n, n_kv * dh), lambda bi: (bi, 0, 0)),
            pl.BlockSpec((1, n_kv, dh, n), lambda bi: (bi, 0, 0, 0)),
            pl.BlockSpec((n_kv, 1, group * n), lambda bi: (0, 0, 0)),
        ],
        out_specs=pl.BlockSpec((1, n, hq * dh), lambda bi: (bi, 0, 0)),
        compiler_params=_params(("parallel",)),
        name="attn_ctx",
    )(qt, k, vt, sink_r)


def _na_bias_tables(rpb, n_rows):
    w = GRID_W
    n_blocks = n_rows // NA_QROWS
    half = NA_ROWS // 2
    cfgs, cfg_of = {}, []
    for rb in range(n_blocks):
        r0 = rb * NA_QROWS
        rs = int(np.clip(r0 - half, 0, n_rows - NA_KROWS))
        key = (r0 - rs,) + tuple(int(np.clip(r0 + a - half, 0, n_rows - NA_ROWS)) - rs for a in range(NA_QROWS))
        cfg_of.append(cfgs.setdefault(key, len(cfgs)))
    kc = np.arange(w)[:, None]
    qc = np.arange(w)[None, :]
    wstart = np.clip(qc - NA_COLS // 2, 0, w - NA_COLS)
    col_ok = (kc >= wstart) & (kc < wstart + NA_COLS)
    dcol = np.clip(kc - qc, -(NA_COLS - 1), NA_COLS - 1) + NA_COLS - 1
    onehot = (dcol[None] == np.arange(2 * NA_COLS - 1)[:, None, None]).astype(np.float32)
    tiles = jnp.einsum('hrd,dkq->hrkq', rpb.astype(F32), jnp.asarray(onehot), precision=lax.Precision.HIGHEST)
    tiles = jnp.where(col_ok[None, None], tiles * LOG2E, NEG_INF)
    masked = jnp.full((rpb.shape[0], w, w), NEG_INF, F32)
    tabs = []
    for key in cfgs:
        off, rsr = key[0], key[1:]
        rows = []
        for c in range(NA_KROWS):
            cols = []
            for a in range(NA_QROWS):
                assert 0 <= rsr[a] and rsr[a] + NA_ROWS <= NA_KROWS
                row_ok = rsr[a] <= c < rsr[a] + NA_ROWS
                cols.append(tiles[:, c - off - a + NA_ROWS - 1] if row_ok else masked)
            rows.append(jnp.concatenate(cols, axis=2))
        tabs.append(jnp.concatenate(rows, axis=1))
    return jnp.asarray(np.array(cfg_of, np.int32)), jnp.stack(tabs)


def _attn_b_kernel(cfg_ref, qt_ref, k_ref, vt_ref, kc_ref, vct_ref, bias_ref, o_ref, *, n_rows, heads):
    del cfg_ref
    w = GRID_W
    dh = HEAD_DIM
    rb = pl.program_id(2)
    rs2 = jnp.clip(rb - NA_ROWS // (2 * NA_QROWS), 0, (n_rows - NA_KROWS) // NA_QROWS)
    start = pl.multiple_of(rs2 * (NA_QROWS * w), NA_QROWS * w)
    span = NA_KROWS * w
    scores, vts = [], []
    for h in range(heads):
        qt = qt_ref[0, h]
        kw = k_ref[0, pl.ds(start, span), h * dh:(h + 1) * dh]
        s_loc = _dot(kw, qt) + bias_ref[0, h]
        s_ctx = _dot(kc_ref[0, :, h * dh:(h + 1) * dh], qt)
        scores.append([s_loc, s_ctx])
        vts.append([vt_ref[0, h, :, pl.ds(start, span)], vct_ref[0, h]])
    outs = _softmax_pv_t(scores, vts, [None] * heads)
    o_ref[0] = jnp.concatenate(outs, axis=0).T.astype(BF16)


def _attn_b(qt, k, vt, kc, vct, rpb):
    b, hq, dh, s = qt.shape
    nc = kc.shape[1]
    w = GRID_W
    n_rows = s // w
    assert s % w == 0 and n_rows >= NA_KROWS and n_rows % NA_QROWS == 0
    cfg_of, table = _na_bias_tables(rpb, n_rows)
    hg = NA_HEAD_GROUP
    tq = NA_QROWS * w
    span = NA_KROWS * w
    grid_spec = pltpu.PrefetchScalarGridSpec(
        num_scalar_prefetch=1,
        grid=(b, hq // hg, n_rows // NA_QROWS),
        in_specs=[
            pl.BlockSpec((1, hg, dh, tq), lambda bi, g, r, cfg: (bi, g, 0, r)),
            pl.BlockSpec((1, s, hg * dh), lambda bi, g, r, cfg: (bi, 0, g), pipeline_mode=pl.Buffered(1)),
            pl.BlockSpec((1, hg, dh, s), lambda bi, g, r, cfg: (bi, g, 0, 0), pipeline_mode=pl.Buffered(1)),
            pl.BlockSpec((1, nc, hg * dh), lambda bi, g, r, cfg: (bi, 0, g)),
            pl.BlockSpec((1, hg, dh, nc), lambda bi, g, r, cfg: (bi, g, 0, 0)),
            pl.BlockSpec((1, hg, span, tq), lambda bi, g, r, cfg: (cfg[r], g, 0, 0)),
        ],
        out_specs=pl.BlockSpec((1, tq, hg * dh), lambda bi, g, r, cfg: (bi, r, g)),
    )
    return pl.pallas_call(
        functools.partial(_attn_b_kernel, n_rows=n_rows, heads=hg),
        out_shape=jax.ShapeDtypeStruct((b, s, hq * dh), BF16),
        grid_spec=grid_spec,
        compiler_params=_params(("parallel", "parallel", "arbitrary"), VMEM_LIMIT),
        name="attn_neighbourhood",
    )(cfg_of, qt, k, vt, kc, vct, table)


def _out_kernel(o_ref, wo_ref, x_ref, g1_ref, g_ref, sh_ref, sc_ref, wr_ref, xn_ref, h_ref, aff_ref, *, tiled):
    y = _dot(o_ref[0], wo_ref[...])
    x = _load_tokens(x_ref, o_ref.shape[1]) if tiled else x_ref[0]
    xn = x + g1_ref[0] * y
    _store_tokens(xn_ref, xn)
    h = _rms_mod(xn, g_ref[...], sh_ref[0], sc_ref[0])
    _store_tokens(h_ref, h)
    logits = _dot3(wr_ref[...], h, _NT)
    z = jnp.exp(logits - logits.max(axis=0, keepdims=True))
    aff_ref[0] = z / z.sum(axis=0, keepdims=True)


def _out_proj(o, w_o, x, g1, g, shift, scale, w_router, dims=None):
    tiled = dims is not None
    bx, n, d = dims if tiled else x.shape
    assert d % (8 * LANES) == 0
    da = o.shape[2]
    e = w_router.shape[1]
    tm = min(512, n)
    bm = g1.shape[0]
    mod_map = (lambda b, t: (b, 0, 0)) if bm == bx else (lambda b, t: (0, 0, 0))
    const2 = lambda b, t: (0, 0)
    mod_spec = pl.BlockSpec((1, 1, d), mod_map)
    tiled_shape = jax.ShapeDtypeStruct((bx * n * (d // LANES), LANES), F32)
    return pl.pallas_call(
        functools.partial(_out_kernel, tiled=tiled),
        out_shape=[tiled_shape, tiled_shape, jax.ShapeDtypeStruct((bx, e, n), F32)],
        grid=(bx, n // tm),
        in_specs=[
            pl.BlockSpec((1, tm, da), lambda b, t: (b, t, 0)),
            pl.BlockSpec((da, d), const2),
            _token_spec(tm, d, n // tm, tiled),
            mod_spec,
            pl.BlockSpec((1, d), const2),
            mod_spec,
            mod_spec,
            pl.BlockSpec((e, d), const2),
        ],
        out_specs=[_token_spec(tm, d, n // tm, True),
                   _token_spec(tm, d, n // tm, True),
                   pl.BlockSpec((1, e, tm), lambda b, t: (b, 0, t))],
        compiler_params=_params(("parallel", "parallel"), VMEM_LIMIT),
        name="out_proj_router",
    )(o, w_o.astype(BF16), x, g1, g.reshape(1, d), shift, scale, w_router.T)


def _untile_kernel(x_ref, o_ref):
    o_ref[0] = _load_tokens(x_ref, o_ref.shape[1])


def _untile(x, dims):
    bx, n, d = dims
    tm = min(512, n)
    return pl.pallas_call(
        _untile_kernel,
        out_shape=jax.ShapeDtypeStruct(dims, F32),
        grid=(bx, n // tm),
        in_specs=[_token_spec(tm, d, n // tm, True)],
        out_specs=_token_spec(tm, d, n // tm, False),
        compiler_params=_params(("parallel", "parallel")),
        name="untile_tokens",
    )(x)


ROUTE_ROW_ALIGN = 16
ROUTE_BISECT_STEPS = 40
MOE_TILE_ROWS = 1024
MOE_ROW_CHUNK = 512
_TN = (((0,), (0,)), ((), ()))


def _route_kernel(aff_ref, tri_ref, lmat_ref, idx_ref, gate_ref, *, cap):
    n_exp, n_rows, lanes = aff_ref.shape[1:]
    aff = aff_ref[0]

    def reduce_tokens(x, op):
        return op(op(x, axis=1, keepdims=True), axis=2, keepdims=True)

    def count_ge(t):
        return reduce_tokens(jnp.where(aff >= t, 1.0, 0.0), jnp.sum)

    def max_below(t):
        return reduce_tokens(jnp.where(aff < t, aff, -1.0), jnp.max)

    def bisect(_, lohi):
        lo, hi = lohi
        mid = 0.5 * (lo + hi)
        ok = count_ge(mid) >= cap
        return jnp.where(ok, mid, lo), jnp.where(ok, hi, mid)

    top = reduce_tokens(aff, jnp.max)
    _, hi = lax.fori_loop(0, ROUTE_BISECT_STEPS, bisect, (jnp.zeros_like(top), 2.0 * top + 1e-30))

    def short(t):
        return jnp.max(jnp.where(count_ge(t) < cap, 1.0, 0.0)) > 0.5

    def walk(t):
        return jnp.where(count_ge(t) >= cap, t, max_below(t))

    thr = lax.while_loop(short, walk, max_below(hi))
    above = aff > thr
    equal = aff == thr
    need = cap - reduce_tokens(jnp.where(above, 1.0, 0.0), jnp.sum)

    def prefix(x3):
        x2 = x3.reshape(n_exp * n_rows, lanes)
        inrow = _dot(x2.astype(BF16), tri_ref[...])
        tot = inrow[:, lanes - 1:lanes]
        offs = _dot(lmat_ref[...], jnp.broadcast_to(tot, inrow.shape).astype(BF16))
        return inrow, offs, tot

    eq_f = jnp.where(equal, 1.0, 0.0)
    eq_in, eq_off, _ = prefix(eq_f)
    eq_rank = (eq_in + eq_off).reshape(n_exp, n_rows, lanes) - eq_f
    sel_f = jnp.where(above | (equal & (eq_rank < need)), 1.0, 0.0)
    inrow, offs, tot = prefix(sel_f)
    sel2 = sel_f.reshape(n_exp * n_rows, lanes)
    aff2 = aff.reshape(n_exp * n_rows, lanes)
    a_hi = aff2.astype(BF16)
    r1 = aff2 - a_hi.astype(F32)
    a_mid = r1.astype(BF16)
    a_lo = (r1 - a_mid.astype(F32)).astype(BF16)

    slot = lax.broadcasted_iota(jnp.int32, (1, cap), 1).astype(F32)
    row_id = lax.broadcasted_iota(jnp.int32, (n_rows, 1), 0).astype(F32)
    lane_id = lax.broadcasted_iota(jnp.int32, (lanes, 1), 0).astype(F32)
    for ei in range(n_exp):
        rows = slice(ei * n_rows, (ei + 1) * n_rows)
        off_e = offs[rows, 0:1]
        tot_e = tot[rows]
        in_row = jnp.where((off_e <= slot) & (slot < off_e + tot_e), 1.0, 0.0)
        row_base = (in_row * off_e).sum(axis=0, keepdims=True)
        row_of = (in_row * row_id).sum(axis=0, keepdims=True)
        in_row_b = in_row.astype(BF16)
        pick = lambda v: _dot(v, in_row_b, _TN)
        hit = (pick(inrow[rows].astype(BF16)) == slot + 1.0 - row_base) & (pick(sel2[rows].astype(BF16)) > 0.5)
        hit_f = jnp.where(hit, 1.0, 0.0)
        aff_rows = (pick(a_hi[rows]) + pick(a_mid[rows])) + pick(a_lo[rows])
        lane_of = (hit_f * lane_id).sum(axis=0, keepdims=True)
        idx_ref[0, ei:ei + 1, :] = (row_of * lanes + lane_of).astype(jnp.int32)
        gate_ref[0, ei:ei + 1, :] = (hit_f * aff_rows).sum(axis=0, keepdims=True)


def _route(aff, cap):
    bx, e, n = aff.shape
    unit = LANES * ROUTE_ROW_ALIGN
    n_pad = -(-n // unit) * unit
    if n_pad != n:
        aff = jnp.pad(aff, ((0, 0), (0, 0), (0, n_pad - n)), constant_values=-1.0)
    n_rows = n_pad // LANES
    tri = np.triu(np.ones((LANES, LANES), np.float32))
    r = np.arange(e * n_rows)
    lmat = ((r[:, None] // n_rows == r[None, :] // n_rows) & (r[None, :] < r[:, None])).astype(np.float32)
    return pl.pallas_call(
        functools.partial(_route_kernel, cap=cap),
        out_shape=[jax.ShapeDtypeStruct((bx, e, cap), jnp.int32), jax.ShapeDtypeStruct((bx, e, cap), F32)],
        grid=(bx,),
        in_specs=[
            pl.BlockSpec((1, e, n_rows, LANES), lambda b: (b, 0, 0, 0)),
            pl.BlockSpec((LANES, LANES), lambda b: (0, 0)),
            pl.BlockSpec((e * n_rows, e * n_rows), lambda b: (0, 0)),
        ],
        out_specs=[pl.BlockSpec((1, e, cap), lambda b: (b, 0, 0)), pl.BlockSpec((1, e, cap), lambda b: (b, 0, 0))],
        compiler_params=_params(("parallel",)),
        name="moe_route",
    )(aff.reshape(bx, e, n_rows, LANES), jnp.asarray(tri, BF16), jnp.asarray(lmat, BF16))


def _moe_kernel(idx_ref, h_hbm, gate_ref, g2_ref, wg_ref, wu_ref, wd_ref, x_hbm, out_hbm,
                xbuf, abuf, xb, yacc, gsem, asem, ssem, *, rows, cap, chunk, n_tiles, n_ff):
    del x_hbm
    tr = xbuf.shape[1] // rows
    e, p, f = pl.program_id(0), pl.program_id(1), pl.program_id(2)
    n_steps = pl.num_programs(0) * n_tiles
    k = e * n_tiles + p
    slot = lax.rem(k, 2)
    lag = 2 if n_tiles >= 2 else 1

    def row_copy(step, i, src, dst, dslot, sem, gather, priority):
        tok = idx_ref[step * rows + i]
        hbm_row = src.at[pl.ds(pl.multiple_of(tok * tr, tr), tr)]
        buf_row = dst.at[dslot, pl.ds(pl.multiple_of(i * tr, tr), tr)]
        cp = (pltpu.make_async_copy(hbm_row, buf_row, sem.at[dslot]) if gather
              else pltpu.make_async_copy(buf_row, hbm_row, sem.at[dslot]))
        cp.start(priority=priority)

    def row_copies(step, src, dst, dslot, sem, gather):
        def body(i, carry):
            for u in range(2):
                row_copy(step, 2 * i + u, src, dst, dslot, sem, gather, u)
            return carry

        lax.fori_loop(0, rows // 2, body, 0, unroll=4)

    def row_copies_inline(step, lo, count, src, dst, dslot, sem, gather):
        for u in range(count):
            row_copy(step, lo + u, src, dst, dslot, sem, gather, u % 2)

    def wait_rows(buf, bslot, sem):
        pltpu.make_async_copy(buf.at[bslot], buf.at[bslot], sem.at[bslot]).wait()

    na = abuf.shape[0]
    pipelined = na >= 3
    acur = lax.rem(k, na) if pipelined else slot
    anext, aprev = lax.rem(k + 1, na), lax.rem(k + na - 1, na)

    @pl.when(f == 0)
    def _begin():
        @pl.when(k == 0)
        def _():
            row_copies(k, h_hbm, xbuf, slot, gsem, True)
            if pipelined:
                row_copies(k, out_hbm, abuf, acur, asem, True)
                row_copies(k, out_hbm, abuf, aprev, asem, True)
                wait_rows(abuf, aprev, asem)

        if pipelined:
            @pl.when(k == 1)
            def _():
                wait_rows(abuf, na - 1, ssem)

            @pl.when(k >= na - 1)
            def _():
                wait_rows(abuf, anext, ssem)
        else:
            @pl.when(k >= lag)
            def _():
                wait_rows(abuf, lax.rem(k - lag, 2), ssem)

            row_copies(k, out_hbm, abuf, slot, asem, True)
        wait_rows(xbuf, slot, gsem)
        for j in range(tr):
            xb[:, j * LANES:(j + 1) * LANES] = xbuf[slot, pl.ds(j, rows, stride=tr), :].astype(BF16)
        yacc[...] = jnp.zeros_like(yacc)
        if pipelined:
            wait_rows(abuf, acur, asem)

    wg = wg_ref[0, 0].astype(BF16)
    wu = wu_ref[0, 0].astype(BF16)
    wd = wd_ref[0, 0].astype(BF16)
    n_chunks = rows // chunk
    ahead = rows // (n_ff * n_chunks)
    nxt = jnp.minimum(k + 1, n_steps - 1)
    prv = jnp.maximum(k - 1, 0)
    for c in range(n_chunks):
        rs = slice(c * chunk, (c + 1) * chunk)
        lo = (f * n_chunks + c) * ahead
        x = xb[rs]
        a = _dot(x, wg)
        row_copies_inline(nxt, lo, ahead, h_hbm, xbuf, 1 - slot, gsem, True)
        u = _dot(x, wu)
        if pipelined:
            row_copies_inline(nxt, lo, ahead, out_hbm, abuf, anext, asem, True)
        hcat = ((a * jax.nn.sigmoid(a)) * u).astype(BF16)
        yacc[rs] += _dot(hcat, wd)
        if pipelined:
            row_copies_inline(prv, lo, ahead, out_hbm, abuf, aprev, ssem, False)

    @pl.when(f == pl.num_programs(2) - 1)
    def _finish():
        if not pipelined:
            wait_rows(abuf, acur, asem)
        for c in range(n_chunks):
            rs = slice(c * chunk, (c + 1) * chunk)
            y = yacc[rs] * (gate_ref[0, rs] * g2_ref[c * chunk // cap])
            for j in range(tr):
                abuf[acur, pl.ds(c * chunk * tr + j, chunk, stride=tr), :] += y[:, j * LANES:(j + 1) * LANES]
        if not pipelined:
            row_copies(k, out_hbm, abuf, slot, ssem, False)

        @pl.when(k == n_steps - 1)
        def _():
            wait_rows(xbuf, 1 - slot, gsem)
            if pipelined:
                wait_rows(abuf, anext, asem)
                row_copies(k, out_hbm, abuf, acur, ssem, False)
                for back in range(na - 1):
                    wait_rows(abuf, lax.rem(k + na - back, na), ssem)
            else:
                if lag == 2:
                    @pl.when(k >= 1)
                    def _():
                        wait_rows(abuf, 1 - slot, ssem)
                wait_rows(abuf, slot, ssem)


def _ec_moe(x, h, aff, g2, layer, w_gate, w_up, w_down, dims):
    bx, n, d = dims
    tr = d // LANES
    e = aff.shape[1]
    ff = w_gate.shape[3]
    cap = EC_CAPACITY * n // e
    idx, gate = _route(aff, cap)
    rows_of = (idx + (jnp.arange(bx, dtype=jnp.int32) * n)[:, None, None])
    rows_of = jnp.transpose(rows_of, (1, 0, 2)).reshape(e * bx * cap)
    gate_c = jnp.transpose(gate, (1, 0, 2)).reshape(e, bx * cap, 1)
    g2 = jnp.broadcast_to(g2, (bx, 1, d))
    nb = max(1, min(bx, MOE_TILE_ROWS // cap))
    assert bx % nb == 0
    rows = nb * cap
    n_tiles = bx // nb
    chunk = min(MOE_ROW_CHUNK, cap)
    assert cap % chunk == 0
    tf = min(512, ff)
    assert rows % ((ff // tf) * (rows // chunk)) == 0
    n_abuf = min(4, n_tiles) if n_tiles >= 3 else 2
    grid_spec = pltpu.PrefetchScalarGridSpec(
        num_scalar_prefetch=1,
        grid=(e, n_tiles, ff // tf),
        in_specs=[
            pl.BlockSpec(memory_space=pl.ANY),
            pl.BlockSpec((1, rows, 1), lambda ei, p, f, idx: (ei, p, 0)),
            pl.BlockSpec((nb, 1, d), lambda ei, p, f, idx: (p, 0, 0)),
            pl.BlockSpec((1, 1, d, tf), lambda ei, p, f, idx: (layer, ei, 0, f)),
            pl.BlockSpec((1, 1, d, tf), lambda ei, p, f, idx: (layer, ei, 0, f)),
            pl.BlockSpec((1, 1, tf, d), lambda ei, p, f, idx: (layer, ei, f, 0)),
            pl.BlockSpec(memory_space=pl.ANY),
        ],
        out_specs=pl.BlockSpec(memory_space=pl.ANY),
        scratch_shapes=[
            pltpu.VMEM((2, rows * tr, LANES), F32),
            pltpu.VMEM((n_abuf, rows * tr, LANES), F32),
            pltpu.VMEM((rows, d), BF16),
            pltpu.VMEM((rows, d), F32),
            pltpu.SemaphoreType.DMA((2,)),
            pltpu.SemaphoreType.DMA((n_abuf,)),
            pltpu.SemaphoreType.DMA((n_abuf,)),
        ],
    )
    return pl.pallas_call(
        functools.partial(_moe_kernel, rows=rows, cap=cap, chunk=chunk, n_tiles=n_tiles, n_ff=ff // tf),
        out_shape=jax.ShapeDtypeStruct(x.shape, F32),
        grid_spec=grid_spec,
        input_output_aliases={7: 0},
        compiler_params=_params(("arbitrary", "arbitrary", "arbitrary"), VMEM_LIMIT),
        name="moe_ffn",
    )(rows_of, h, gate_c, g2, w_gate, w_up, w_down, x)


def kernel(x, c, ctx, c_ctx, w_mod, b_mod, norm_mix, norm_ffn, a_w_qkv, a_q_gain, a_k_gain, a_sink, a_w_o,
           b_w_qkv, b_q_gain, b_k_gain, b_rpb, b_w_o, moe_router, moe_w_gate, moe_w_up, moe_w_down):
    b, s, d = x.shape
    depth = w_mod.shape[0]
    pad = (-(b + 1)) % 8
    cvec = jnp.concatenate([c, c_ctx[None], jnp.zeros((pad, d), F32)], axis=0)
    mods = _modulation(cvec, w_mod, b_mod)
    rope_tabs = _rope_tables(s)
    xl, xc = x, ctx
    for i in range(depth):
        last = i == depth - 1
        til_l = dict(dims=x.shape) if i > 0 else {}
        til_c = dict(dims=ctx.shape) if i > 0 else {}
        ml =[m.reshape(b, 1, d) for m in jnp.split(mods[i, :b], 6, axis=-1)]
        mc = [m.reshape(1, 1, d) for m in jnp.split(mods[i, b:b + 1], 6, axis=-1)]
        sh1_l, sc1_l, g1_l, sh2_l, sc2_l, g2_l = ml
        sh1_c, sc1_c, g1_c, sh2_c, sc2_c, g2_c = mc
        j = i // N_MIXERS
        if i % N_MIXERS == 0:
            ql, kl, vl = _project(xl, norm_mix[i], sh1_l, sc1_l, a_w_qkv[j], a_q_gain[j], a_k_gain[j],
                                  A_HEADS, A_KV_HEADS, rope_tabs, **til_l)
            qc, kc, vc = _project(xc, norm_mix[i], sh1_c, sc1_c, a_w_qkv[j], a_q_gain[j], a_k_gain[j],
                                  A_HEADS, A_KV_HEADS, **til_c)
            ol = _attn_a(ql, kl, vl, kc, vc, a_sink[j])
            oc = None if last else _attn_ctx(qc, kc, vc, a_sink[j])
            w_o = a_w_o[j]
        else:
            ql, kl, vl = _project(xl, norm_mix[i], sh1_l, sc1_l, b_w_qkv[j], b_q_gain[j], b_k_gain[j],
                                  B_HEADS, B_HEADS, **til_l)
            qc, kc, vc = _project(xc, norm_mix[i], sh1_c, sc1_c, b_w_qkv[j], b_q_gain[j], b_k_gain[j],
                                  B_HEADS, B_HEADS, **til_c)
            ol = _attn_b(ql, kl, vl, kc, vc, b_rpb[j])
            oc = None if last else _attn_ctx(qc, kc, vc, None)
            w_o = b_w_o[j]
        xl, hl, aff_l = _out_proj(ol, w_o, xl, g1_l, norm_ffn[i], sh2_l, sc2_l, moe_router[i], **til_l)
        xl = _ec_moe(xl, hl, aff_l, g2_l, i, moe_w_gate, moe_w_up, moe_w_down, x.shape)
        if not last:
            xc, hc, aff_c = _out_proj(oc, w_o, xc, g1_c, norm_ffn[i], sh2_c, sc2_c, moe_router[i], **til_c)
            xc = _ec_moe(xc, hc, aff_c, g2_c, i, moe_w_gate, moe_w_up, moe_w_down, ctx.shape)
    return _untile(xl, x.shape)
```

```python
import functools

import numpy as np
import jax
import jax.numpy as jnp
from jax import lax
from jax.experimental import pallas as pl
from jax.experimental.pallas import tpu as pltpu

F32 = jnp.float32
BF16 = jnp.bfloat16

GRID_W = 64
HEAD_DIM = 64
A_HEADS = 16
A_KV_HEADS = 4
A_WINDOW = 128
A_BLOCK = 128
Q_BLOCK = 128
B_HEADS = 16
NA_ROWS = 8
NA_COLS = 16
N_EXPERTS = 16
EC_CAPACITY = 2
ROPE_BASE = 10000.0
EPS = 1e-6
NEG_INF = -1e30
ATTN_SCALE = HEAD_DIM ** -0.5
LOG2E = 1.4426950408889634
N_MIXERS = 2

NA_QROWS = 2
NA_KROWS = NA_ROWS + NA_QROWS
NA_HEAD_GROUP = 16

LANES = 128
VMEM_LIMIT = 56 * 1024 * 1024

_NT = (((1,), (1,)), ((), ()))
_NN = (((1,), (0,)), ((), ()))


def _dot(a, b, dims=_NN):
    return lax.dot_general(a, b, dims, preferred_element_type=F32)


def _split(a):
    hi = a.astype(BF16)
    lo = (a - hi.astype(F32)).astype(BF16)
    return hi, lo


def _dot3(a, b, dims=_NN):
    ah, al = _split(a)
    bh, bl = _split(b)
    return _dot(ah, bh, dims) + (_dot(ah, bl, dims) + _dot(al, bh, dims))


def _params(sem, vmem=None):
    return pltpu.CompilerParams(dimension_semantics=sem, vmem_limit_bytes=vmem)


def _mod_kernel(c_ref, w_ref, b_ref, o_ref):
    c = c_ref[...]
    a = c * jax.nn.sigmoid(c)
    o_ref[0] = _dot3(a, w_ref[0]) + b_ref[0]


def _modulation(cvec, w_mod, b_mod):
    depth, d, n = w_mod.shape
    rows = cvec.shape[0]
    tn = 1024
    return pl.pallas_call(
        _mod_kernel,
        out_shape=jax.ShapeDtypeStruct((depth, rows, n), F32),
        grid=(depth, n // tn),
        in_specs=[
            pl.BlockSpec((rows, d), lambda i, j: (0, 0)),
            pl.BlockSpec((1, d, tn), lambda i, j: (i, 0, j)),
            pl.BlockSpec((1, 1, tn), lambda i, j: (i, 0, j)),
        ],
        out_specs=pl.BlockSpec((1, rows, tn), lambda i, j: (i, 0, j)),
        compiler_params=_params(("parallel", "parallel")),
        name="modulation",
    )(cvec, w_mod, b_mod.reshape(depth, 1, n))


def _rms_mod(x, g, shift, scale):
    y = x * lax.rsqrt(jnp.mean(x * x, axis=-1, keepdims=True) + EPS)
    return (y * g) * (1.0 + scale) + shift


def _load_tokens(ref, tm):
    chunks = ref.shape[0] // tm
    return jnp.concatenate([ref[pl.ds(j, tm, stride=chunks), :] for j in range(chunks)], axis=1)


def _store_tokens(ref, val):
    tm, d = val.shape
    chunks = d // LANES
    for j in range(chunks):
        ref[pl.ds(j, tm, stride=chunks), :] = val[:, j * LANES:(j + 1) * LANES]


def _token_spec(tm, d, n_tiles, tiled):
    if tiled:
        return pl.BlockSpec((tm * (d // LANES), LANES), lambda b, t: (b * n_tiles + t, 0))
    return pl.BlockSpec((1, tm, d), lambda b, t: (b, t, 0))


def _proj_kernel(*refs, n_q, n_kv, rope, tiled):
    if rope:
        x_ref, g_ref, sh_ref, sc_ref, wt_ref, cq_ref, ck_ref, sq_ref, sk_ref, qt_ref, k_ref, vt_ref = refs
    else:
        x_ref, g_ref, sh_ref, sc_ref, wt_ref, cq_ref, ck_ref, qt_ref, k_ref, vt_ref = refs
        sq_ref = sk_ref = None
    dh = HEAD_DIM
    x = _load_tokens(x_ref, k_ref.shape[1]) if tiled else x_ref[0]
    h = _rms_mod(x, g_ref[...], sh_ref[0], sc_ref[0]).astype(BF16)
    res = _dot(wt_ref[...], h, _NT)

    def head(j, c_ref, s_ref):
        t = res[j * dh:(j + 1) * dh]
        tn = t * lax.rsqrt(jnp.mean(t * t, axis=0, keepdims=True) + EPS)
        out = tn * c_ref[...]
        if rope:
            q4 = dh // 4
            partner = jnp.concatenate([tn[q4:2 * q4], tn[:q4], tn[3 * q4:], tn[2 * q4:3 * q4]], axis=0)
            out = out + partner * s_ref[...]
        return out

    for j in range(n_q):
        qh = head(j, cq_ref, sq_ref).astype(BF16)
        for qb in range(qt_ref.shape[1]):
            qt_ref[0, qb, j] = qh[:, qb * Q_BLOCK:(qb + 1) * Q_BLOCK]
    ks =[head(n_q + j, ck_ref, sk_ref) for j in range(n_kv)]
    k_ref[0] = jnp.concatenate(ks, axis=0).T.astype(BF16)
    v0 = (n_q + n_kv) * dh
    for j in range(n_kv):
        vt_ref[0, j] = res[v0 + j * dh:v0 + (j + 1) * dh].astype(BF16)


def _rope_perm():
    d = np.arange(HEAD_DIM)
    return np.where((d % 32) < 16, d + 16, d - 16)


def _rope_tables(n_tok):
    pos = jnp.arange(n_tok)
    row = (pos // GRID_W).astype(F32)
    col = (pos % GRID_W).astype(F32)
    n_freq = HEAD_DIM // 4
    inv = ROPE_BASE ** (-jnp.arange(n_freq, dtype=F32) / n_freq)
    ang_r = row[:, None] * inv
    ang_c = col[:, None] * inv
    cos_t = jnp.concatenate([jnp.cos(ang_r), jnp.cos(ang_r), jnp.cos(ang_c), jnp.cos(ang_c)], axis=-1)
    sin_t = jnp.concatenate([-jnp.sin(ang_r), jnp.sin(ang_r), -jnp.sin(ang_c), jnp.sin(ang_c)], axis=-1)
    return cos_t.T, sin_t.T


def _project(x, g, shift, scale, w, q_gain, k_gain, n_q, n_kv, rope_tabs=None, dims=None):
    tiled = dims is not None
    bx, n, d = dims if tiled else x.shape
    dh = HEAD_DIM
    rope = rope_tabs is not None
    tm = min(512, n)
    qg = (q_gain.astype(F32) * (ATTN_SCALE * LOG2E)).reshape(dh, 1)
    kg = k_gain.astype(F32).reshape(dh, 1)
    wt = w.T.astype(BF16)
    n_cols = wt.shape[0]
    bm = shift.shape[0]
    mod_map = (lambda b, t: (b, 0, 0)) if bm == bx else (lambda b, t: (0, 0, 0))
    const2 = lambda b, t: (0, 0)
    tok_map = lambda b, t: (0, t)
    in_specs = [
        _token_spec(tm, d, n // tm, tiled),
        pl.BlockSpec((1, d), const2),
        pl.BlockSpec((1, 1, d), mod_map),
        pl.BlockSpec((1, 1, d), mod_map),
        pl.BlockSpec((n_cols, d), const2),
    ]
    args = [x, g.reshape(1, d), shift, scale, wt]
    if rope:
        cos_t, sin_t = rope_tabs
        perm = _rope_perm()
        args += [qg * cos_t, kg * cos_t, qg[perm] * sin_t, kg[perm] * sin_t]
        in_specs += [pl.BlockSpec((dh, tm), tok_map)] * 4
    else:
        args += [jnp.broadcast_to(qg, (dh, tm)), jnp.broadcast_to(kg, (dh, tm))]
        in_specs += [pl.BlockSpec((dh, tm), const2)] * 2
    assert tm % Q_BLOCK == 0
    out_shape = [jax.ShapeDtypeStruct((bx, n // Q_BLOCK, n_q, dh, Q_BLOCK), BF16),
                 jax.ShapeDtypeStruct((bx, n, n_kv * dh), BF16),
                 jax.ShapeDtypeStruct((bx, n_kv, dh, n), BF16)]
    out_specs = [pl.BlockSpec((1, tm // Q_BLOCK, n_q, dh, Q_BLOCK), lambda b, t: (b, t, 0, 0, 0)),
                 pl.BlockSpec((1, tm, n_kv * dh), lambda b, t: (b, t, 0)),
                 pl.BlockSpec((1, n_kv, dh, tm), lambda b, t: (b, 0, 0, t))]
    return pl.pallas_call(
        functools.partial(_proj_kernel, n_q=n_q, n_kv=n_kv, rope=rope, tiled=tiled),
        out_shape=out_shape,
        grid=(bx, n // tm),
        in_specs=in_specs,
        out_specs=out_specs,
        compiler_params=_params(("parallel", "parallel"), VMEM_LIMIT),
        name="qkv_proj_rope" if rope else "qkv_proj",
    )(*args)


def _softmax_pv_t(scores, vts, sinks):
    ms = []
    for parts, sink in zip(scores, sinks):
        m = parts[0].max(axis=0, keepdims=True)
        for s in parts[1:]:
            m = jnp.maximum(m, s.max(axis=0, keepdims=True))
        ms.append(m if sink is None else jnp.maximum(m, sink))
    probs, dens = [], []
    for parts, m, sink in zip(scores, ms, sinks):
        ps = [jnp.exp2(s - m) for s in parts]
        den = ps[0].sum(axis=0, keepdims=True)
        for p in ps[1:]:
            den = den + p.sum(axis=0, keepdims=True)
        dens.append(den if sink is None else den + jnp.exp2(sink - m))
        probs.append([p.astype(BF16) for p in ps])
    outs = []
    for ps, vt_parts, den in zip(probs, vts, dens):
        acc = _dot(vt_parts[0], ps[0])
        for vt, p in zip(vt_parts[1:], ps[1:]):
            acc = acc + _dot(vt, p)
        outs.append(acc / den)
    return outs


def _attn_a_kernel(qt_ref, k_ref, vt_ref, kc_ref, vct_ref, sink_ref, mask_ref, o_ref, *, seq, n_kv, group):
    blk = A_BLOCK
    span = 3 * blk
    dh = HEAD_DIM
    i = pl.program_id(1)
    start = pl.multiple_of(jnp.clip((i - 1) * blk, 0, seq - span), blk)
    scores, vts = [], []
    for h in range(n_kv):
        qt = jnp.concatenate([qt_ref[0, 0, h * group + g] for g in range(group)], axis=1)
        kw = k_ref[0, pl.ds(start, span), h * dh:(h + 1) * dh]
        s_loc = _dot(kw, qt) + mask_ref[0]
        s_ctx = _dot(kc_ref[0, :, h * dh:(h + 1) * dh], qt)
        scores.append([s_loc, s_ctx])
        vts.append([vt_ref[0, h, :, pl.ds(start, span)], vct_ref[0, h]])
    ots = _softmax_pv_t(scores, vts, [sink_ref[h] for h in range(n_kv)])
    outs = [ot[:, g * blk:(g + 1) * blk] for ot in ots for g in range(group)]
    o_ref[0] = jnp.concatenate(outs, axis=0).T.astype(BF16)


def _sink_rows(sink, n_kv, group, cols):
    s = (sink.astype(F32) * LOG2E).reshape(n_kv, 1, group, 1)
    return jnp.broadcast_to(s, (n_kv, 1, group, cols)).reshape(n_kv, 1, group * cols)


def _attn_a(qt, k, vt, kc, vct, sink):
    b, _, hq, dh, blk = qt.shape
    s = k.shape[1]
    n_kv = vt.shape[1]
    group = hq // n_kv
    nc = kc.shape[1]
    assert blk == A_BLOCK
    n_blk = s // blk
    assert s % blk == 0 and n_blk >= 3
    off = (np.arange(3) * blk)[:, None, None]
    rel = off + (np.arange(group * blk) % blk)[None, None, :] - np.arange(3 * blk)[None, :, None]
    mask = jnp.asarray(np.where(np.abs(rel) <= A_WINDOW, 0.0, NEG_INF).astype(np.float32))
    return pl.pallas_call(
        functools.partial(_attn_a_kernel, seq=s, n_kv=n_kv, group=group),
        out_shape=jax.ShapeDtypeStruct((b, s, hq * dh), BF16),
        grid=(b, s // blk),
        in_specs=[
            pl.BlockSpec((1, 1, hq, dh, blk), lambda bi, i: (bi, i, 0, 0, 0)),
            pl.BlockSpec((1, s, n_kv * dh), lambda bi, i: (bi, 0, 0)),
            pl.BlockSpec((1, n_kv, dh, s), lambda bi, i: (bi, 0, 0, 0)),
            pl.BlockSpec((1, nc, n_kv * dh), lambda bi, i: (bi, 0, 0)),
            pl.BlockSpec((1, n_kv, dh, nc), lambda bi, i: (bi, 0, 0, 0)),
            pl.BlockSpec((n_kv, 1, group * blk), lambda bi, i: (0, 0, 0)),
            pl.BlockSpec((1, 3 * blk, group * blk),
                         lambda bi, i: (jnp.where(i == 0, 0, jnp.where(i == n_blk - 1, 2, 1)), 0, 0)),
        ],
        out_specs=pl.BlockSpec((1, blk, hq * dh), lambda bi, i: (bi, i, 0)),
        compiler_params=_params(("parallel", "arbitrary"), VMEM_LIMIT),
        name="attn_window",
    )(qt, k, vt, kc, vct, _sink_rows(sink, n_kv, group, blk), mask)


def _attn_ctx_kernel(qt_ref, k_ref, vt_ref, sink_ref, o_ref, *, n_kv, group, use_sink):
    dh = HEAD_DIM
    n_qb = qt_ref.shape[1]
    nq = n_qb * qt_ref.shape[4]
    scores, vts = [], []
    for h in range(n_kv):
        qt = jnp.concatenate([qt_ref[0, qb, h * group + g] for g in range(group) for qb in range(n_qb)], axis=1)
        scores.append([_dot(k_ref[0, :, h * dh:(h + 1) * dh], qt)])
        vts.append([vt_ref[0, h]])
    ots = _softmax_pv_t(scores, vts, [sink_ref[h] if use_sink else None for h in range(n_kv)])
    outs = [ot[:, g * nq:(g + 1) * nq] for ot in ots for g in range(group)]
    o_ref[0] = jnp.concatenate(outs, axis=0).T.astype(BF16)


def _attn_ctx(qt, k, vt, sink):
    b, n_qb, hq, dh, qblk = qt.shape
    n = n_qb * qblk
    n_kv = vt.shape[1]
    group = hq // n_kv
    use_sink = sink is not None
    sink_r = _sink_rows(sink if use_sink else jnp.zeros((hq,), F32), n_kv, group, n)
    return pl.pallas_call(
        functools.partial(_attn_ctx_kernel, n_kv=n_kv, group=group, use_sink=use_sink),
        out_shape=jax.ShapeDtypeStruct((b, n, hq * dh), BF16),
        grid=(b,),
        in_specs=[
            pl.BlockSpec((1, n_qb, hq, dh, qblk), lambda bi: (bi, 0, 0, 0, 0)),
            pl.BlockSpec((1, n, n_kv * dh), lambda bi: (bi, 0, 0)),
            pl.BlockSpec((1, n_kv, dh, n), lambda bi: (bi, 0, 0, 0)),
            pl.BlockSpec((n_kv, 1, group * n), lambda bi: (0, 0, 0)),
        ],
        out_specs=pl.BlockSpec((1, n, hq * dh), lambda bi: (bi, 0, 0)),
        compiler_params=_params(("parallel",)),
        name="attn_ctx",
    )(qt, k, vt, sink_r)


def _na_bias_tables(rpb, n_rows):
    w = GRID_W
    n_blocks = n_rows // NA_QROWS
    half = NA_ROWS // 2
    cfgs, cfg_of = {}, []
    for rb in range(n_blocks):
        r0 = rb * NA_QROWS
        rs = int(np.clip(r0 - half, 0, n_rows - NA_KROWS))
        key = (r0 - rs,) + tuple(int(np.clip(r0 + a - half, 0, n_rows - NA_ROWS)) - rs for a in range(NA_QROWS))
        cfg_of.append(cfgs.setdefault(key, len(cfgs)))
    kc = np.arange(w)[:, None]
    qc = np.arange(w)[None, :]
    wstart = np.clip(qc - NA_COLS // 2, 0, w - NA_COLS)
    col_ok = (kc >= wstart) & (kc < wstart + NA_COLS)
    dcol = np.clip(kc - qc, -(NA_COLS - 1), NA_COLS - 1) + NA_COLS - 1
    onehot = (dcol[None] == np.arange(2 * NA_COLS - 1)[:, None, None]).astype(np.float32)
    tiles = jnp.einsum('hrd,dkq->hrkq', rpb.astype(F32), jnp.asarray(onehot), precision=lax.Precision.HIGHEST)
    tiles = jnp.where(col_ok[None, None], tiles * LOG2E, NEG_INF)
    masked = jnp.full((rpb.shape[0], w, w), NEG_INF, F32)
    tabs = []
    for key in cfgs:
        off, rsr = key[0], key[1:]
        rows = []
        for c in range(NA_KROWS):
            cols = []
            for a in range(NA_QROWS):
                assert 0 <= rsr[a] and rsr[a] + NA_ROWS <= NA_KROWS
                row_ok = rsr[a] <= c < rsr[a] + NA_ROWS
                cols.append(tiles[:, c - off - a + NA_ROWS - 1] if row_ok else masked)
            rows.append(jnp.concatenate(cols, axis=2))
        tabs.append(jnp.concatenate(rows, axis=1))
    return jnp.asarray(np.array(cfg_of, np.int32)), jnp.stack(tabs)


def _attn_b_kernel(cfg_ref, qt_ref, k_ref, vt_ref, kc_ref, vct_ref, bias_ref, o_ref, *, n_rows, heads):
    del cfg_ref
    w = GRID_W
    dh = HEAD_DIM
    rb = pl.program_id(2)
    rs2 = jnp.clip(rb - NA_ROWS // (2 * NA_QROWS), 0, (n_rows - NA_KROWS) // NA_QROWS)
    start = pl.multiple_of(rs2 * (NA_QROWS * w), NA_QROWS * w)
    span = NA_KROWS * w
    scores, vts = [], []
    for h in range(heads):
        qt = qt_ref[0, 0, h]
        kw = k_ref[0, pl.ds(start, span), h * dh:(h + 1) * dh]
        s_loc = _dot(kw, qt) + bias_ref[0, h]
        s_ctx = _dot(kc_ref[0, :, h * dh:(h + 1) * dh], qt)
        scores.append([s_loc, s_ctx])
        vts.append([vt_ref[0, h, :, pl.ds(start, span)], vct_ref[0, h]])
    outs = _softmax_pv_t(scores, vts, [None] * heads)
    o_ref[0] = jnp.concatenate(outs, axis=0).T.astype(BF16)


def _attn_b(qt, k, vt, kc, vct, rpb):
    b, _, hq, dh, qblk = qt.shape
    s = k.shape[1]
    nc = kc.shape[1]
    w = GRID_W
    n_rows = s // w
    assert s % w == 0 and n_rows >= NA_KROWS and n_rows % NA_QROWS == 0 and qblk == NA_QROWS * w
    cfg_of, table = _na_bias_tables(rpb, n_rows)
    hg = NA_HEAD_GROUP
    tq = NA_QROWS * w
    span = NA_KROWS * w
    grid_spec = pltpu.PrefetchScalarGridSpec(
        num_scalar_prefetch=1,
        grid=(b, hq // hg, n_rows // NA_QROWS),
        in_specs=[
            pl.BlockSpec((1, 1, hg, dh, tq), lambda bi, g, r, cfg: (bi, r, g, 0, 0)),
            pl.BlockSpec((1, s, hg * dh), lambda bi, g, r, cfg: (bi, 0, g), pipeline_mode=pl.Buffered(1)),
            pl.BlockSpec((1, hg, dh, s), lambda bi, g, r, cfg: (bi, g, 0, 0), pipeline_mode=pl.Buffered(1)),
            pl.BlockSpec((1, nc, hg * dh), lambda bi, g, r, cfg: (bi, 0, g)),
            pl.BlockSpec((1, hg, dh, nc), lambda bi, g, r, cfg: (bi, g, 0, 0)),
            pl.BlockSpec((1, hg, span, tq), lambda bi, g, r, cfg: (cfg[r], g, 0, 0)),
        ],
        out_specs=pl.BlockSpec((1, tq, hg * dh), lambda bi, g, r, cfg: (bi, r, g)),
    )
    return pl.pallas_call(
        functools.partial(_attn_b_kernel, n_rows=n_rows, heads=hg),
        out_shape=jax.ShapeDtypeStruct((b, s, hq * dh), BF16),
        grid_spec=grid_spec,
        compiler_params=_params(("parallel", "parallel", "arbitrary"), VMEM_LIMIT),
        name="attn_neighbourhood",
    )(cfg_of, qt, k, vt, kc, vct, table)


def _out_kernel(o_ref, wo_ref, x_ref, g1_ref, g_ref, sh_ref, sc_ref, wr_ref, xn_ref, h_ref, aff_ref, *, tiled):
    y = _dot(o_ref[0], wo_ref[...])
    x = _load_tokens(x_ref, o_ref.shape[1]) if tiled else x_ref[0]
    xn = x + g1_ref[0] * y
    _store_tokens(xn_ref, xn)
    h = _rms_mod(xn, g_ref[...], sh_ref[0], sc_ref[0])
    _store_tokens(h_ref, h)
    logits = _dot3(wr_ref[...], h, _NT)
    z = jnp.exp(logits - logits.max(axis=0, keepdims=True))
    aff_ref[0] = z / z.sum(axis=0, keepdims=True)


def _out_proj(o, w_o, x, g1, g, shift, scale, w_router, dims=None):
    tiled = dims is not None
    bx, n, d = dims if tiled else x.shape
    assert d % (8 * LANES) == 0
    da = o.shape[2]
    e = w_router.shape[1]
    tm = min(512, n)
    bm = g1.shape[0]
    mod_map = (lambda b, t: (b, 0, 0)) if bm == bx else (lambda b, t: (0, 0, 0))
    const2 = lambda b, t: (0, 0)
    mod_spec = pl.BlockSpec((1, 1, d), mod_map)
    tiled_shape = jax.ShapeDtypeStruct((bx * n * (d // LANES), LANES), F32)
    return pl.pallas_call(
        functools.partial(_out_kernel, tiled=tiled),
        out_shape=[tiled_shape, tiled_shape, jax.ShapeDtypeStruct((bx, e, n), F32)],
        grid=(bx, n // tm),
        in_specs=[
            pl.BlockSpec((1, tm, da), lambda b, t: (b, t, 0)),
            pl.BlockSpec((da, d), const2),
            _token_spec(tm, d, n // tm, tiled),
            mod_spec,
            pl.BlockSpec((1, d), const2),
            mod_spec,
            mod_spec,
            pl.BlockSpec((e, d), const2),
        ],
        out_specs=[_token_spec(tm, d, n // tm, True),
                   _token_spec(tm, d, n // tm, True),
                   pl.BlockSpec((1, e, tm), lambda b, t: (b, 0, t))],
        compiler_params=_params(("parallel", "parallel"), VMEM_LIMIT),
        name="out_proj_router",
    )(o, w_o.astype(BF16), x, g1, g.reshape(1, d), shift, scale, w_router.T)


def _untile_kernel(x_ref, o_ref):
    o_ref[0] = _load_tokens(x_ref, o_ref.shape[1])


def _untile(x, dims):
    bx, n, d = dims
    tm = min(512, n)
    return pl.pallas_call(
        _untile_kernel,
        out_shape=jax.ShapeDtypeStruct(dims, F32),
        grid=(bx, n // tm),
        in_specs=[_token_spec(tm, d, n // tm, True)],
        out_specs=_token_spec(tm, d, n // tm, False),
        compiler_params=_params(("parallel", "parallel")),
        name="untile_tokens",
    )(x)


ROUTE_ROW_ALIGN = 16
ROUTE_BISECT_STEPS = 40
MOE_TILE_ROWS = 1024
MOE_ROW_CHUNK = 512
_TN = (((0,), (0,)), ((), ()))


def _route_kernel(aff_ref, tri_ref, lmat_ref, idx_ref, gate_ref, *, cap):
    n_exp, n_rows, lanes = aff_ref.shape[1:]
    aff = aff_ref[0]

    def reduce_tokens(x, op):
        return op(op(x, axis=1, keepdims=True), axis=2, keepdims=True)

    def count_ge(t):
        return reduce_tokens(jnp.where(aff >= t, 1.0, 0.0), jnp.sum)

    def max_below(t):
        return reduce_tokens(jnp.where(aff < t, aff, -1.0), jnp.max)

    def bisect(_, lohi):
        lo, hi = lohi
        mid = 0.5 * (lo + hi)
        ok = count_ge(mid) >= cap
        return jnp.where(ok, mid, lo), jnp.where(ok, hi, mid)

    top = reduce_tokens(aff, jnp.max)
    _, hi = lax.fori_loop(0, ROUTE_BISECT_STEPS, bisect, (jnp.zeros_like(top), 2.0 * top + 1e-30))

    def short(t):
        return jnp.max(jnp.where(count_ge(t) < cap, 1.0, 0.0)) > 0.5

    def walk(t):
        return jnp.where(count_ge(t) >= cap, t, max_below(t))

    thr = lax.while_loop(short, walk, max_below(hi))
    above = aff > thr
    equal = aff == thr
    need = cap - reduce_tokens(jnp.where(above, 1.0, 0.0), jnp.sum)

    def prefix(x3):
        x2 = x3.reshape(n_exp * n_rows, lanes)
        inrow = _dot(x2.astype(BF16), tri_ref[...])
        tot = inrow[:, lanes - 1:lanes]
        offs = _dot(lmat_ref[...], jnp.broadcast_to(tot, inrow.shape).astype(BF16))
        return inrow, offs, tot

    eq_f = jnp.where(equal, 1.0, 0.0)
    eq_in, eq_off, _ = prefix(eq_f)
    eq_rank = (eq_in + eq_off).reshape(n_exp, n_rows, lanes) - eq_f
    sel_f = jnp.where(above | (equal & (eq_rank < need)), 1.0, 0.0)
    inrow, offs, tot = prefix(sel_f)
    sel2 = sel_f.reshape(n_exp * n_rows, lanes)
    aff2 = aff.reshape(n_exp * n_rows, lanes)
    a_hi = aff2.astype(BF16)
    r1 = aff2 - a_hi.astype(F32)
    a_mid = r1.astype(BF16)
    a_lo = (r1 - a_mid.astype(F32)).astype(BF16)

    slot = lax.broadcasted_iota(jnp.int32, (1, cap), 1).astype(F32)
    row_id = lax.broadcasted_iota(jnp.int32, (n_rows, 1), 0).astype(F32)
    lane_id = lax.broadcasted_iota(jnp.int32, (lanes, 1), 0).astype(F32)
    for ei in range(n_exp):
        rows = slice(ei * n_rows, (ei + 1) * n_rows)
        off_e = offs[rows, 0:1]
        tot_e = tot[rows]
        in_row = jnp.where((off_e <= slot) & (slot < off_e + tot_e), 1.0, 0.0)
        row_base = (in_row * off_e).sum(axis=0, keepdims=True)
        row_of = (in_row * row_id).sum(axis=0, keepdims=True)
        in_row_b = in_row.astype(BF16)
        pick = lambda v: _dot(v, in_row_b, _TN)
        hit = (pick(inrow[rows].astype(BF16)) == slot + 1.0 - row_base) & (pick(sel2[rows].astype(BF16)) > 0.5)
        hit_f = jnp.where(hit, 1.0, 0.0)
        aff_rows = (pick(a_hi[rows]) + pick(a_mid[rows])) + pick(a_lo[rows])
        lane_of = (hit_f * lane_id).sum(axis=0, keepdims=True)
        idx_ref[0, ei:ei + 1, :] = (row_of * lanes + lane_of).astype(jnp.int32)
        gate_ref[0, ei:ei + 1, :] = (hit_f * aff_rows).sum(axis=0, keepdims=True)


def _route(aff, cap):
    bx, e, n = aff.shape
    unit = LANES * ROUTE_ROW_ALIGN
    n_pad = -(-n // unit) * unit
    if n_pad != n:
        aff = jnp.pad(aff, ((0, 0), (0, 0), (0, n_pad - n)), constant_values=-1.0)
    n_rows = n_pad // LANES
    tri = np.triu(np.ones((LANES, LANES), np.float32))
    r = np.arange(e * n_rows)
    lmat = ((r[:, None] // n_rows == r[None, :] // n_rows) & (r[None, :] < r[:, None])).astype(np.float32)
    return pl.pallas_call(
        functools.partial(_route_kernel, cap=cap),
        out_shape=[jax.ShapeDtypeStruct((bx, e, cap), jnp.int32), jax.ShapeDtypeStruct((bx, e, cap), F32)],
        grid=(bx,),
        in_specs=[
            pl.BlockSpec((1, e, n_rows, LANES), lambda b: (b, 0, 0, 0)),
            pl.BlockSpec((LANES, LANES), lambda b: (0, 0)),
            pl.BlockSpec((e * n_rows, e * n_rows), lambda b: (0, 0)),
        ],
        out_specs=[pl.BlockSpec((1, e, cap), lambda b: (b, 0, 0)), pl.BlockSpec((1, e, cap), lambda b: (b, 0, 0))],
        compiler_params=_params(("parallel",)),
        name="moe_route",
    )(aff.reshape(bx, e, n_rows, LANES), jnp.asarray(tri, BF16), jnp.asarray(lmat, BF16))


def _moe_kernel(idx_ref, h_hbm, gate_ref, g2_ref, wg_ref, wu_ref, wd_ref, x_hbm, out_hbm,
                xbuf, abuf, xb, yacc, gsem, asem, ssem, *, rows, cap, chunk, n_tiles, n_ff):
    del x_hbm
    tr = xbuf.shape[1] // rows
    e, p, f = pl.program_id(0), pl.program_id(1), pl.program_id(2)
    n_steps = pl.num_programs(0) * n_tiles
    k = e * n_tiles + p
    slot = lax.rem(k, 2)
    lag = 2 if n_tiles >= 2 else 1

    def row_copy(step, i, src, dst, dslot, sem, gather, priority):
        tok = idx_ref[step * rows + i]
        hbm_row = src.at[pl.ds(pl.multiple_of(tok * tr, tr), tr)]
        buf_row = dst.at[dslot, pl.ds(pl.multiple_of(i * tr, tr), tr)]
        cp = (pltpu.make_async_copy(hbm_row, buf_row, sem.at[dslot]) if gather
              else pltpu.make_async_copy(buf_row, hbm_row, sem.at[dslot]))
        cp.start(priority=priority)

    def row_copies(step, src, dst, dslot, sem, gather):
        def body(i, carry):
            for u in range(2):
                row_copy(step, 2 * i + u, src, dst, dslot, sem, gather, u)
            return carry

        lax.fori_loop(0, rows // 2, body, 0, unroll=4)

    def row_copies_inline(step, lo, count, src, dst, dslot, sem, gather):
        for u in range(count):
            row_copy(step, lo + u, src, dst, dslot, sem, gather, u % 2)

    def wait_rows(buf, bslot, sem):
        pltpu.make_async_copy(buf.at[bslot], buf.at[bslot], sem.at[bslot]).wait()

    na = abuf.shape[0]
    pipelined = na >= 3
    acur = lax.rem(k, na) if pipelined else slot
    anext, aprev = lax.rem(k + 1, na), lax.rem(k + na - 1, na)

    @pl.when(f == 0)
    def _begin():
        @pl.when(k == 0)
        def _():
            row_copies(k, h_hbm, xbuf, slot, gsem, True)
            if pipelined:
                row_copies(k, out_hbm, abuf, acur, asem, True)
                row_copies(k, out_hbm, abuf, aprev, asem, True)
                wait_rows(abuf, aprev, asem)

        if pipelined:
            @pl.when(k == 1)
            def _():
                wait_rows(abuf, na - 1, ssem)

            @pl.when(k >= na - 1)
            def _():
                wait_rows(abuf, anext, ssem)
        else:
            @pl.when(k >= lag)
            def _():
                wait_rows(abuf, lax.rem(k - lag, 2), ssem)

            row_copies(k, out_hbm, abuf, slot, asem, True)
        wait_rows(xbuf, slot, gsem)
        for j in range(tr):
            xb[:, j * LANES:(j + 1) * LANES] = xbuf[slot, pl.ds(j, rows, stride=tr), :].astype(BF16)
        yacc[...] = jnp.zeros_like(yacc)
        if pipelined:
            wait_rows(abuf, acur, asem)

    wg = wg_ref[0, 0].astype(BF16)
    wu = wu_ref[0, 0].astype(BF16)
    wd = wd_ref[0, 0].astype(BF16)
    n_chunks = rows // chunk
    ahead = rows // (n_ff * n_chunks)
    nxt = jnp.minimum(k + 1, n_steps - 1)
    prv = jnp.maximum(k - 1, 0)
    for c in range(n_chunks):
        rs = slice(c * chunk, (c + 1) * chunk)
        lo = (f * n_chunks + c) * ahead
        x = xb[rs]
        a = _dot(x, wg)
        row_copies_inline(nxt, lo, ahead, h_hbm, xbuf, 1 - slot, gsem, True)
        u = _dot(x, wu)
        if pipelined:
            row_copies_inline(nxt, lo, ahead, out_hbm, abuf, anext, asem, True)
        hcat = ((a * jax.nn.sigmoid(a)) * u).astype(BF16)
        yacc[rs] += _dot(hcat, wd)
        if pipelined:
            row_copies_inline(prv, lo, ahead, out_hbm, abuf, aprev, ssem, False)

    @pl.when(f == pl.num_programs(2) - 1)
    def _finish():
        if not pipelined:
            wait_rows(abuf, acur, asem)
        for c in range(n_chunks):
            rs = slice(c * chunk, (c + 1) * chunk)
            y = yacc[rs] * (gate_ref[0, rs] * g2_ref[c * chunk // cap])
            for j in range(tr):
                abuf[acur, pl.ds(c * chunk * tr + j, chunk, stride=tr), :] += y[:, j * LANES:(j + 1) * LANES]
        if not pipelined:
            row_copies(k, out_hbm, abuf, slot, ssem, False)

        @pl.when(k == n_steps - 1)
        def _():
            wait_rows(xbuf, 1 - slot, gsem)
            if pipelined:
                wait_rows(abuf, anext, asem)
                row_copies(k, out_hbm, abuf, acur, ssem, False)
                for back in range(na - 1):
                    wait_rows(abuf, lax.rem(k + na - back, na), ssem)
            else:
                if lag == 2:
                    @pl.when(k >= 1)
                    def _():
                        wait_rows(abuf, 1 - slot, ssem)
                wait_rows(abuf, slot, ssem)


def _ec_moe(x, h, aff, g2, layer, w_gate, w_up, w_down, dims):
    bx, n, d = dims
    tr = d // LANES
    e = aff.shape[1]
    ff = w_gate.shape[3]
    cap = EC_CAPACITY * n // e
    idx, gate = _route(aff, cap)
    rows_of = (idx + (jnp.arange(bx, dtype=jnp.int32) * n)[:, None, None])
    rows_of = jnp.transpose(rows_of, (1, 0, 2)).reshape(e * bx * cap)
    gate_c = jnp.transpose(gate, (1, 0, 2)).reshape(e, bx * cap, 1)
    g2 = jnp.broadcast_to(g2, (bx, 1, d))
    nb = max(1, min(bx, MOE_TILE_ROWS // cap))
    assert bx % nb == 0
    rows = nb * cap
    n_tiles = bx // nb
    chunk = min(MOE_ROW_CHUNK, cap)
    assert cap % chunk == 0
    tf = min(512, ff)
    assert rows % ((ff // tf) * (rows // chunk)) == 0
    n_abuf = min(4, n_tiles) if n_tiles >= 3 else 2
    grid_spec = pltpu.PrefetchScalarGridSpec(
        num_scalar_prefetch=1,
        grid=(e, n_tiles, ff // tf),
        in_specs=[
            pl.BlockSpec(memory_space=pl.ANY),
            pl.BlockSpec((1, rows, 1), lambda ei, p, f, idx: (ei, p, 0)),
            pl.BlockSpec((nb, 1, d), lambda ei, p, f, idx: (p, 0, 0)),
            pl.BlockSpec((1, 1, d, tf), lambda ei, p, f, idx: (layer, ei, 0, f)),
            pl.BlockSpec((1, 1, d, tf), lambda ei, p, f, idx: (layer, ei, 0, f)),
            pl.BlockSpec((1, 1, tf, d), lambda ei, p, f, idx: (layer, ei, f, 0)),
            pl.BlockSpec(memory_space=pl.ANY),
        ],
        out_specs=pl.BlockSpec(memory_space=pl.ANY),
        scratch_shapes=[
            pltpu.VMEM((2, rows * tr, LANES), F32),
            pltpu.VMEM((n_abuf, rows * tr, LANES), F32),
            pltpu.VMEM((rows, d), BF16),
            pltpu.VMEM((rows, d), F32),
            pltpu.SemaphoreType.DMA((2,)),
            pltpu.SemaphoreType.DMA((n_abuf,)),
            pltpu.SemaphoreType.DMA((n_abuf,)),
        ],
    )
    return pl.pallas_call(
        functools.partial(_moe_kernel, rows=rows, cap=cap, chunk=chunk, n_tiles=n_tiles, n_ff=ff // tf),
        out_shape=jax.ShapeDtypeStruct(x.shape, F32),
        grid_spec=grid_spec,
        input_output_aliases={7: 0},
        compiler_params=_params(("arbitrary", "arbitrary", "arbitrary"), VMEM_LIMIT),
        name="moe_ffn",
    )(rows_of, h, gate_c, g2, w_gate, w_up, w_down, x)


def kernel(x, c, ctx, c_ctx, w_mod, b_mod, norm_mix, norm_ffn, a_w_qkv, a_q_gain, a_k_gain, a_sink, a_w_o,
           b_w_qkv, b_q_gain, b_k_gain, b_rpb, b_w_o, moe_router, moe_w_gate, moe_w_up, moe_w_down):
    b, s, d = x.shape
    depth = w_mod.shape[0]
    pad = (-(b + 1)) % 8
    cvec = jnp.concatenate([c, c_ctx[None], jnp.zeros((pad, d), F32)], axis=0)
    mods = _modulation(cvec, w_mod, b_mod)
    rope_tabs = _rope_tables(s)
    xl, xc = x, ctx
    for i in range(depth):
        last = i == depth - 1
        til_l = dict(dims=x.shape) if i > 0 else {}
        til_c = dict(dims=ctx.shape) if i > 0 else {}
        ml =[m.reshape(b, 1, d) for m in jnp.split(mods[i, :b], 6, axis=-1)]
        mc = [m.reshape(1, 1, d) for m in jnp.split(mods[i, b:b + 1], 6, axis=-1)]
        sh1_l, sc1_l, g1_l, sh2_l, sc2_l, g2_l = ml
        sh1_c, sc1_c, g1_c, sh2_c, sc2_c, g2_c = mc
        j = i // N_MIXERS
        if i % N_MIXERS == 0:
            ql, kl, vl = _project(xl, norm_mix[i], sh1_l, sc1_l, a_w_qkv[j], a_q_gain[j], a_k_gain[j],
                                  A_HEADS, A_KV_HEADS, rope_tabs, **til_l)
            qc, kc, vc = _project(xc, norm_mix[i], sh1_c, sc1_c, a_w_qkv[j], a_q_gain[j], a_k_gain[j],
                                  A_HEADS, A_KV_HEADS, **til_c)
            ol = _attn_a(ql, kl, vl, kc, vc, a_sink[j])
            oc = None if last else _attn_ctx(qc, kc, vc, a_sink[j])
            w_o = a_w_o[j]
        else:
            ql, kl, vl = _project(xl, norm_mix[i], sh1_l, sc1_l, b_w_qkv[j], b_q_gain[j], b_k_gain[j],
                                  B_HEADS, B_HEADS, **til_l)
            qc, kc, vc = _project(xc, norm_mix[i], sh1_c, sc1_c, b_w_qkv[j], b_q_gain[j], b_k_gain[j],
                                  B_HEADS, B_HEADS, **til_c)
            ol = _attn_b(ql, kl, vl, kc, vc, b_rpb[j])
            oc = None if last else _attn_ctx(qc, kc, vc, None)
            w_o = b_w_o[j]
        xl, hl, aff_l = _out_proj(ol, w_o, xl, g1_l, norm_ffn[i], sh2_l, sc2_l, moe_router[i], **til_l)
        xl = _ec_moe(xl, hl, aff_l, g2_l, i, moe_w_gate, moe_w_up, moe_w_down, x.shape)
        if not last:
            xc, hc, aff_c = _out_proj(oc, w_o, xc, g1_c, norm_ffn[i], sh2_c, sc2_c, moe_router[i], **til_c)
            xc = _ec_moe(xc, hc, aff_c, g2_c, i, moe_w_gate, moe_w_up, moe_w_down, ctx.shape)
    return _untile(xl, x.shape)
```

```python
import functools

import numpy as np
import jax
import jax.numpy as jnp
from jax import lax
from jax.experimental import pallas as pl
from jax.experimental.pallas import tpu as pltpu

F32 = jnp.float32
BF16 = jnp.bfloat16

GRID_W = 64
HEAD_DIM = 64
A_HEADS = 16
A_KV_HEADS = 4
A_WINDOW = 128
A_BLOCK = 128
Q_BLOCK = 128
B_HEADS = 16
NA_ROWS = 8
NA_COLS = 16
N_EXPERTS = 16
EC_CAPACITY = 2
ROPE_BASE = 10000.0
EPS = 1e-6
NEG_INF = -1e30
ATTN_SCALE = HEAD_DIM ** -0.5
LOG2E = 1.4426950408889634
N_MIXERS = 2

NA_QROWS = 2
NA_KROWS = NA_ROWS + NA_QROWS
NA_HEAD_GROUP = 16

LANES = 128
VMEM_LIMIT = 56 * 1024 * 1024

_NT = (((1,), (1,)), ((), ()))
_NN = (((1,), (0,)), ((), ()))


def _dot(a, b, dims=_NN):
    return lax.dot_general(a, b, dims, preferred_element_type=F32)


def _split(a):
    hi = a.astype(BF16)
    lo = (a - hi.astype(F32)).astype(BF16)
    return hi, lo


def _dot3(a, b, dims=_NN):
    ah, al = _split(a)
    bh, bl = _split(b)
    return _dot(ah, bh, dims) + (_dot(ah, bl, dims) + _dot(al, bh, dims))


def _params(sem, vmem=None):
    return pltpu.CompilerParams(dimension_semantics=sem, vmem_limit_bytes=vmem)


def _mod_kernel(c_ref, w_ref, b_ref, o_ref):
    c = c_ref[...]
    a = c * jax.nn.sigmoid(c)
    o_ref[0] = _dot3(a, w_ref[0]) + b_ref[0]


def _modulation(cvec, w_mod, b_mod):
    depth, d, n = w_mod.shape
    rows = cvec.shape[0]
    tn = 1024
    return pl.pallas_call(
        _mod_kernel,
        out_shape=jax.ShapeDtypeStruct((depth, rows, n), F32),
        grid=(depth, n // tn),
        in_specs=[
            pl.BlockSpec((rows, d), lambda i, j: (0, 0)),
            pl.BlockSpec((1, d, tn), lambda i, j: (i, 0, j)),
            pl.BlockSpec((1, 1, tn), lambda i, j: (i, 0, j)),
        ],
        out_specs=pl.BlockSpec((1, rows, tn), lambda i, j: (i, 0, j)),
        compiler_params=_params(("parallel", "parallel")),
        name="modulation",
    )(cvec, w_mod, b_mod.reshape(depth, 1, n))


def _rms_mod(x, g, shift, scale):
    y = x * lax.rsqrt(jnp.mean(x * x, axis=-1, keepdims=True) + EPS)
    return (y * g) * (1.0 + scale) + shift


def _load_tokens(ref, tm):
    chunks = ref.shape[0] // tm
    return jnp.concatenate([ref[pl.ds(j, tm, stride=chunks), :] for j in range(chunks)], axis=1)


def _store_tokens(ref, val):
    tm, d = val.shape
    chunks = d // LANES
    for j in range(chunks):
        ref[pl.ds(j, tm, stride=chunks), :] = val[:, j * LANES:(j + 1) * LANES]


def _token_spec(tm, d, n_tiles, tiled):
    if tiled:
        return pl.BlockSpec((tm * (d // LANES), LANES), lambda b, t: (b * n_tiles + t, 0))
    return pl.BlockSpec((1, tm, d), lambda b, t: (b, t, 0))


def _proj_kernel(*refs, n_q, n_kv, rope, tiled):
    if rope:
        x_ref, g_ref, sh_ref, sc_ref, wt_ref, cq_ref, ck_ref, sq_ref, sk_ref, qt_ref, k_ref, vt_ref = refs
    else:
        x_ref, g_ref, sh_ref, sc_ref, wt_ref, cq_ref, ck_ref, qt_ref, k_ref, vt_ref = refs
        sq_ref = sk_ref = None
    dh = HEAD_DIM
    x = _load_tokens(x_ref, k_ref.shape[1]) if tiled else x_ref[0]
    h = _rms_mod(x, g_ref[...], sh_ref[0], sc_ref[0]).astype(BF16)
    res = _dot(wt_ref[...], h, _NT)

    def head(j, c_ref, s_ref):
        t = res[j * dh:(j + 1) * dh]
        tn = t * lax.rsqrt(jnp.mean(t * t, axis=0, keepdims=True) + EPS)
        out = tn * c_ref[...]
        if rope:
            q4 = dh // 4
            partner = jnp.concatenate([tn[q4:2 * q4], tn[:q4], tn[3 * q4:], tn[2 * q4:3 * q4]], axis=0)
            out = out + partner * s_ref[...]
        return out

    for j in range(n_q):
        qh = head(j, cq_ref, sq_ref).astype(BF16)
        for qb in range(qt_ref.shape[1]):
            qt_ref[0, qb, j] = qh[:, qb * Q_BLOCK:(qb + 1) * Q_BLOCK]
    ks =[head(n_q + j, ck_ref, sk_ref) for j in range(n_kv)]
    k_ref[0] = jnp.concatenate(ks, axis=0).T.astype(BF16)
    v0 = (n_q + n_kv) * dh
    for j in range(n_kv):
        vt_ref[0, j] = res[v0 + j * dh:v0 + (j + 1) * dh].astype(BF16)


def _rope_perm():
    d = np.arange(HEAD_DIM)
    return np.where((d % 32) < 16, d + 16, d - 16)


def _rope_tables(n_tok):
    pos = jnp.arange(n_tok)
    row = (pos // GRID_W).astype(F32)
    col = (pos % GRID_W).astype(F32)
    n_freq = HEAD_DIM // 4
    inv = ROPE_BASE ** (-jnp.arange(n_freq, dtype=F32) / n_freq)
    ang_r = row[:, None] * inv
    ang_c = col[:, None] * inv
    cos_t = jnp.concatenate([jnp.cos(ang_r), jnp.cos(ang_r), jnp.cos(ang_c), jnp.cos(ang_c)], axis=-1)
    sin_t = jnp.concatenate([-jnp.sin(ang_r), jnp.sin(ang_r), -jnp.sin(ang_c), jnp.sin(ang_c)], axis=-1)
    return cos_t.T, sin_t.T


def _project(x, g, shift, scale, w, q_gain, k_gain, n_q, n_kv, rope_tabs=None, dims=None):
    tiled = dims is not None
    bx, n, d = dims if tiled else x.shape
    dh = HEAD_DIM
    rope = rope_tabs is not None
    tm = min(512, n)
    qg = (q_gain.astype(F32) * (ATTN_SCALE * LOG2E)).reshape(dh, 1)
    kg = k_gain.astype(F32).reshape(dh, 1)
    wt = w.T.astype(BF16)
    n_cols = wt.shape[0]
    bm = shift.shape[0]
    mod_map = (lambda b, t: (b, 0, 0)) if bm == bx else (lambda b, t: (0, 0, 0))
    const2 = lambda b, t: (0, 0)
    tok_map = lambda b, t: (0, t)
    in_specs = [
        _token_spec(tm, d, n // tm, tiled),
        pl.BlockSpec((1, d), const2),
        pl.BlockSpec((1, 1, d), mod_map),
        pl.BlockSpec((1, 1, d), mod_map),
        pl.BlockSpec((n_cols, d), const2),
    ]
    args = [x, g.reshape(1, d), shift, scale, wt]
    if rope:
        cos_t, sin_t = rope_tabs
        perm = _rope_perm()
        args += [qg * cos_t, kg * cos_t, qg[perm] * sin_t, kg[perm] * sin_t]
        in_specs += [pl.BlockSpec((dh, tm), tok_map)] * 4
    else:
        args += [jnp.broadcast_to(qg, (dh, tm)), jnp.broadcast_to(kg, (dh, tm))]
        in_specs += [pl.BlockSpec((dh, tm), const2)] * 2
    assert tm % Q_BLOCK == 0
    out_shape = [jax.ShapeDtypeStruct((bx, n // Q_BLOCK, n_q, dh, Q_BLOCK), BF16),
                 jax.ShapeDtypeStruct((bx, n, n_kv * dh), BF16),
                 jax.ShapeDtypeStruct((bx, n_kv, dh, n), BF16)]
    out_specs = [pl.BlockSpec((1, tm // Q_BLOCK, n_q, dh, Q_BLOCK), lambda b, t: (b, t, 0, 0, 0)),
                 pl.BlockSpec((1, tm, n_kv * dh), lambda b, t: (b, t, 0)),
                 pl.BlockSpec((1, n_kv, dh, tm), lambda b, t: (b, 0, 0, t))]
    return pl.pallas_call(
        functools.partial(_proj_kernel, n_q=n_q, n_kv=n_kv, rope=rope, tiled=tiled),
        out_shape=out_shape,
        grid=(bx, n // tm),
        in_specs=in_specs,
        out_specs=out_specs,
        compiler_params=_params(("parallel", "parallel"), VMEM_LIMIT),
        name="qkv_proj_rope" if rope else "qkv_proj",
    )(*args)


def _softmax_pv_t(scores, vts, sinks):
    ms = []
    for parts, sink in zip(scores, sinks):
        m = parts[0].max(axis=0, keepdims=True)
        for s in parts[1:]:
            m = jnp.maximum(m, s.max(axis=0, keepdims=True))
        ms.append(m if sink is None else jnp.maximum(m, sink))
    probs, dens = [], []
    for parts, m, sink in zip(scores, ms, sinks):
        ps = [jnp.exp2(s - m) for s in parts]
        den = ps[0].sum(axis=0, keepdims=True)
        for p in ps[1:]:
            den = den + p.sum(axis=0, keepdims=True)
        dens.append(den if sink is None else den + jnp.exp2(sink - m))
        probs.append([p.astype(BF16) for p in ps])
    outs = []
    for ps, vt_parts, den in zip(probs, vts, dens):
        acc = _dot(vt_parts[0], ps[0])
        for vt, p in zip(vt_parts[1:], ps[1:]):
            acc = acc + _dot(vt, p)
        outs.append(acc / den)
    return outs


def _attn_a_kernel(qt_ref, k_ref, vt_ref, kc_ref, vct_ref, sink_ref, mask_ref, o_ref, *, seq, n_kv, group):
    blk = A_BLOCK
    span = 3 * blk
    dh = HEAD_DIM
    i = pl.program_id(1)
    start = pl.multiple_of(jnp.clip((i - 1) * blk, 0, seq - span), blk)
    scores, vts = [], []
    for h in range(n_kv):
        qt = jnp.concatenate([qt_ref[0, 0, h * group + g] for g in range(group)], axis=1)
        kw = k_ref[0, pl.ds(start, span), h * dh:(h + 1) * dh]
        s_loc = _dot(kw, qt) + mask_ref[0]
        s_ctx = _dot(kc_ref[0, :, h * dh:(h + 1) * dh], qt)
        scores.append([s_loc, s_ctx])
        vts.append([vt_ref[0, h, :, pl.ds(start, span)], vct_ref[0, h]])
    ots = _softmax_pv_t(scores, vts, [sink_ref[h] for h in range(n_kv)])
    outs = [ot[:, g * blk:(g + 1) * blk] for ot in ots for g in range(group)]
    o_ref[0] = jnp.concatenate(outs, axis=0).T.astype(BF16)


def _sink_rows(sink, n_kv, group, cols):
    s = (sink.astype(F32) * LOG2E).reshape(n_kv, 1, group, 1)
    return jnp.broadcast_to(s, (n_kv, 1, group, cols)).reshape(n_kv, 1, group * cols)


def _attn_a(qt, k, vt, kc, vct, sink):
    b, _, hq, dh, blk = qt.shape
    s = k.shape[1]
    n_kv = vt.shape[1]
    group = hq // n_kv
    nc = kc.shape[1]
    assert blk == A_BLOCK
    n_blk = s // blk
    assert s % blk == 0 and n_blk >= 3
    off = (np.arange(3) * blk)[:, None, None]
    rel = off + (np.arange(group * blk) % blk)[None, None, :] - np.arange(3 * blk)[None, :, None]
    mask = jnp.asarray(np.where(np.abs(rel) <= A_WINDOW, 0.0, NEG_INF).astype(np.float32))
    return pl.pallas_call(
        functools.partial(_attn_a_kernel, seq=s, n_kv=n_kv, group=group),
        out_shape=jax.ShapeDtypeStruct((b, s, hq * dh), BF16),
        grid=(b, s // blk),
        in_specs=[
            pl.BlockSpec((1, 1, hq, dh, blk), lambda bi, i: (bi, i, 0, 0, 0)),
            pl.BlockSpec((1, s, n_kv * dh), lambda bi, i: (bi, 0, 0)),
            pl.BlockSpec((1, n_kv, dh, s), lambda bi, i: (bi, 0, 0, 0)),
            pl.BlockSpec((1, nc, n_kv * dh), lambda bi, i: (bi, 0, 0)),
            pl.BlockSpec((1, n_kv, dh, nc), lambda bi, i: (bi, 0, 0, 0)),
            pl.BlockSpec((n_kv, 1, group * blk), lambda bi, i: (0, 0, 0)),
            pl.BlockSpec((1, 3 * blk, group * blk),
                         lambda bi, i: (jnp.where(i == 0, 0, jnp.where(i == n_blk - 1, 2, 1)), 0, 0)),
        ],
        out_specs=pl.BlockSpec((1, blk, hq * dh), lambda bi, i: (bi, i, 0)),
        compiler_params=_params(("parallel", "arbitrary"), VMEM_LIMIT),
        name="attn_window",
    )(qt, k, vt, kc, vct, _sink_rows(sink, n_kv, group, blk), mask)


def _attn_ctx_kernel(qt_ref, k_ref, vt_ref, sink_ref, o_ref, *, n_kv, group, use_sink):
    dh = HEAD_DIM
    n_qb = qt_ref.shape[1]
    nq = n_qb * qt_ref.shape[4]
    scores, vts = [], []
    for h in range(n_kv):
        qt = jnp.concatenate([qt_ref[0, qb, h * group + g] for g in range(group) for qb in range(n_qb)], axis=1)
        scores.append([_dot(k_ref[0, :, h * dh:(h + 1) * dh], qt)])
        vts.append([vt_ref[0, h]])
    ots = _softmax_pv_t(scores, vts, [sink_ref[h] if use_sink else None for h in range(n_kv)])
    outs = [ot[:, g * nq:(g + 1) * nq] for ot in ots for g in range(group)]
    o_ref[0] = jnp.concatenate(outs, axis=0).T.astype(BF16)


def _attn_ctx(qt, k, vt, sink):
    b, n_qb, hq, dh, qblk = qt.shape
    n = n_qb * qblk
    n_kv = vt.shape[1]
    group = hq // n_kv
    use_sink = sink is not None
    sink_r = _sink_rows(sink if use_sink else jnp.zeros((hq,), F32), n_kv, group, n)
    return pl.pallas_call(
        functools.partial(_attn_ctx_kernel, n_kv=n_kv, group=group, use_sink=use_sink),
        out_shape=jax.ShapeDtypeStruct((b, n, hq * dh), BF16),
        grid=(b,),
        in_specs=[
            pl.BlockSpec((1, n_qb, hq, dh, qblk), lambda bi: (bi, 0, 0, 0, 0)),
            pl.BlockSpec((1, n, n_kv * dh), lambda bi: (bi, 0, 0)),
            pl.BlockSpec((1, n_kv, dh, n), lambda bi: (bi, 0, 0, 0)),
            pl.BlockSpec((n_kv, 1, group * n), lambda bi: (0, 0, 0)),
        ],
        out_specs=pl.BlockSpec((1, n, hq * dh), lambda bi: (bi, 0, 0)),
        compiler_params=_params(("parallel",)),
        name="attn_ctx",
    )(qt, k, vt, sink_r)


def _na_bias_tables(rpb, n_rows):
    w = GRID_W
    n_blocks = n_rows // NA_QROWS
    half = NA_ROWS // 2
    cfgs, cfg_of = {}, []
    for rb in range(n_blocks):
        r0 = rb * NA_QROWS
        rs = int(np.clip(r0 - half, 0, n_rows - NA_KROWS))
        key = (r0 - rs,) + tuple(int(np.clip(r0 + a - half, 0, n_rows - NA_ROWS)) - rs for a in range(NA_QROWS))
        cfg_of.append(cfgs.setdefault(key, len(cfgs)))
    kc = np.arange(w)[:, None]
    qc = np.arange(w)[None, :]
    wstart = np.clip(qc - NA_COLS // 2, 0, w - NA_COLS)
    col_ok = (kc >= wstart) & (kc < wstart + NA_COLS)
    dcol = np.clip(kc - qc, -(NA_COLS - 1), NA_COLS - 1) + NA_COLS - 1
    onehot = (dcol[None] == np.arange(2 * NA_COLS - 1)[:, None, None]).astype(np.float32)
    tiles = jnp.einsum('hrd,dkq->hrkq', rpb.astype(F32), jnp.asarray(onehot), precision=lax.Precision.HIGHEST)
    tiles = jnp.where(col_ok[None, None], tiles * LOG2E, NEG_INF)
    masked = jnp.full((rpb.shape[0], w, w), NEG_INF, F32)
    tabs = []
    for key in cfgs:
        off, rsr = key[0], key[1:]
        rows = []
        for c in range(NA_KROWS):
            cols = []
            for a in range(NA_QROWS):
                assert 0 <= rsr[a] and rsr[a] + NA_ROWS <= NA_KROWS
                row_ok = rsr[a] <= c < rsr[a] + NA_ROWS
                cols.append(tiles[:, c - off - a + NA_ROWS - 1] if row_ok else masked)
            rows.append(jnp.concatenate(cols, axis=2))
        tabs.append(jnp.concatenate(rows, axis=1))
    return jnp.asarray(np.array(cfg_of, np.int32)), jnp.stack(tabs)


def _attn_b_kernel(cfg_ref, qt_ref, k_ref, vt_ref, kc_ref, vct_ref, bias_ref, o_ref, *, n_rows, heads):
    del cfg_ref
    w = GRID_W
    dh = HEAD_DIM
    rb = pl.program_id(2)
    rs2 = jnp.clip(rb - NA_ROWS // (2 * NA_QROWS), 0, (n_rows - NA_KROWS) // NA_QROWS)
    start = pl.multiple_of(rs2 * (NA_QROWS * w), NA_QROWS * w)
    span = NA_KROWS * w
    scores, vts = [], []
    for h in range(heads):
        qt = qt_ref[0, 0, h]
        kw = k_ref[0, pl.ds(start, span), h * dh:(h + 1) * dh]
        s_loc = _dot(kw, qt) + bias_ref[0, h]
        s_ctx = _dot(kc_ref[0, :, h * dh:(h + 1) * dh], qt)
        scores.append([s_loc, s_ctx])
        vts.append([vt_ref[0, h, :, pl.ds(start, span)], vct_ref[0, h]])
    outs = _softmax_pv_t(scores, vts, [None] * heads)
    o_ref[0] = jnp.concatenate(outs, axis=0).T.astype(BF16)


def _attn_b(qt, k, vt, kc, vct, rpb):
    b, _, hq, dh, qblk = qt.shape
    s = k.shape[1]
    nc = kc.shape[1]
    w = GRID_W
    n_rows = s // w
    assert s % w == 0 and n_rows >= NA_KROWS and n_rows % NA_QROWS == 0 and qblk == NA_QROWS * w
    cfg_of, table = _na_bias_tables(rpb, n_rows)
    hg = NA_HEAD_GROUP
    tq = NA_QROWS * w
    span = NA_KROWS * w
    grid_spec = pltpu.PrefetchScalarGridSpec(
        num_scalar_prefetch=1,
        grid=(b, hq // hg, n_rows // NA_QROWS),
        in_specs=[
            pl.BlockSpec((1, 1, hg, dh, tq), lambda bi, g, r, cfg: (bi, r, g, 0, 0)),
            pl.BlockSpec((1, s, hg * dh), lambda bi, g, r, cfg: (bi, 0, g), pipeline_mode=pl.Buffered(1)),
            pl.BlockSpec((1, hg, dh, s), lambda bi, g, r, cfg: (bi, g, 0, 0), pipeline_mode=pl.Buffered(1)),
            pl.BlockSpec((1, nc, hg * dh), lambda bi, g, r, cfg: (bi, 0, g)),
            pl.BlockSpec((1, hg, dh, nc), lambda bi, g, r, cfg: (bi, g, 0, 0)),
            pl.BlockSpec((1, hg, span, tq), lambda bi, g, r, cfg: (cfg[r], g, 0, 0)),
        ],
        out_specs=pl.BlockSpec((1, tq, hg * dh), lambda bi, g, r, cfg: (bi, r, g)),
    )
    return pl.pallas_call(
        functools.partial(_attn_b_kernel, n_rows=n_rows, heads=hg),
        out_shape=jax.ShapeDtypeStruct((b, s, hq * dh), BF16),
        grid_spec=grid_spec,
        compiler_params=_params(("parallel", "parallel", "arbitrary"), VMEM_LIMIT),
        name="attn_neighbourhood",
    )(cfg_of, qt, k, vt, kc, vct, table)


def _out_kernel(o_ref, wo_ref, x_ref, g1_ref, g_ref, sh_ref, sc_ref, wr_ref, xn_ref, h_ref, aff_ref, *, tiled):
    y = _dot(o_ref[0], wo_ref[...])
    x = _load_tokens(x_ref, o_ref.shape[1]) if tiled else x_ref[0]
    xn = x + g1_ref[0] * y
    _store_tokens(xn_ref, xn)
    h = _rms_mod(xn, g_ref[...], sh_ref[0], sc_ref[0])
    _store_tokens(h_ref, h)
    logits = _dot3(wr_ref[...], h, _NT)
    z = jnp.exp(logits - logits.max(axis=0, keepdims=True))
    aff_ref[0] = z / z.sum(axis=0, keepdims=True)


def _out_proj(o, w_o, x, g1, g, shift, scale, w_router, dims=None):
    tiled = dims is not None
    bx, n, d = dims if tiled else x.shape
    assert d % (8 * LANES) == 0
    da = o.shape[2]
    e = w_router.shape[1]
    tm = min(512, n)
    bm = g1.shape[0]
    mod_map = (lambda b, t: (b, 0, 0)) if bm == bx else (lambda b, t: (0, 0, 0))
    const2 = lambda b, t: (0, 0)
    mod_spec = pl.BlockSpec((1, 1, d), mod_map)
    tiled_shape = jax.ShapeDtypeStruct((bx * n * (d // LANES), LANES), F32)
    return pl.pallas_call(
        functools.partial(_out_kernel, tiled=tiled),
        out_shape=[tiled_shape, tiled_shape, jax.ShapeDtypeStruct((bx, e, n), F32)],
        grid=(bx, n // tm),
        in_specs=[
            pl.BlockSpec((1, tm, da), lambda b, t: (b, t, 0)),
            pl.BlockSpec((da, d), const2),
            _token_spec(tm, d, n // tm, tiled),
            mod_spec,
            pl.BlockSpec((1, d), const2),
            mod_spec,
            mod_spec,
            pl.BlockSpec((e, d), const2),
        ],
        out_specs=[_token_spec(tm, d, n // tm, True),
                   _token_spec(tm, d, n // tm, True),
                   pl.BlockSpec((1, e, tm), lambda b, t: (b, 0, t))],
        compiler_params=_params(("parallel", "parallel"), VMEM_LIMIT),
        name="out_proj_router",
    )(o, w_o.astype(BF16), x, g1, g.reshape(1, d), shift, scale, w_router.T)


def _untile_kernel(x_ref, o_ref):
    o_ref[0] = _load_tokens(x_ref, o_ref.shape[1])


def _untile(x, dims):
    bx, n, d = dims
    tm = min(512, n)
    return pl.pallas_call(
        _untile_kernel,
        out_shape=jax.ShapeDtypeStruct(dims, F32),
        grid=(bx, n // tm),
        in_specs=[_token_spec(tm, d, n // tm, True)],
        out_specs=_token_spec(tm, d, n // tm, False),
        compiler_params=_params(("parallel", "parallel")),
        name="untile_tokens",
    )(x)


ROUTE_ROW_ALIGN = 16
ROUTE_BISECT_STEPS = 30
FF_TILE = 512
FF_TILE_SMALL = 1024
FF_SMALL_ROWS = 256
MOE_TILE_ROWS = 1024
MOE_ROW_CHUNK = 512
_TN = (((0,), (0,)), ((), ()))


def _route_kernel(aff_ref, tri_ref, lmat_ref, idx_ref, gate_ref, *, cap):
    n_exp, n_rows, lanes = aff_ref.shape[1:]
    aff = aff_ref[0]

    def reduce_tokens(x, op):
        return op(op(x, axis=1, keepdims=True), axis=2, keepdims=True)

    def count_ge(t):
        return reduce_tokens(jnp.where(aff >= t, 1.0, 0.0), jnp.sum)

    def max_below(t):
        return reduce_tokens(jnp.where(aff < t, aff, -1.0), jnp.max)

    def bisect(_, lohi):
        lo, hi = lohi
        mid = 0.5 * (lo + hi)
        ok = count_ge(mid) >= cap
        return jnp.where(ok, mid, lo), jnp.where(ok, hi, mid)

    top = reduce_tokens(aff, jnp.max)
    _, hi = lax.fori_loop(0, ROUTE_BISECT_STEPS, bisect, (jnp.zeros_like(top), 2.0 * top + 1e-30))

    def short(t):
        return jnp.max(jnp.where(count_ge(t) < cap, 1.0, 0.0)) > 0.5

    def walk(t):
        return jnp.where(count_ge(t) >= cap, t, max_below(t))

    thr = lax.while_loop(short, walk, max_below(hi))
    above = aff > thr
    equal = aff == thr
    need = cap - reduce_tokens(jnp.where(above, 1.0, 0.0), jnp.sum)

    def prefix(x3):
        x2 = x3.reshape(n_exp * n_rows, lanes)
        inrow = _dot(x2.astype(BF16), tri_ref[...])
        tot = inrow[:, lanes - 1:lanes]
        offs = _dot(lmat_ref[...], jnp.broadcast_to(tot, inrow.shape).astype(BF16))
        return inrow, offs, tot

    eq_f = jnp.where(equal, 1.0, 0.0)
    eq_in, eq_off, _ = prefix(eq_f)
    eq_rank = (eq_in + eq_off).reshape(n_exp, n_rows, lanes) - eq_f
    sel_f = jnp.where(above | (equal & (eq_rank < need)), 1.0, 0.0)
    inrow, offs, tot = prefix(sel_f)
    sel2 = sel_f.reshape(n_exp * n_rows, lanes)
    aff2 = aff.reshape(n_exp * n_rows, lanes)
    a_hi = aff2.astype(BF16)
    r1 = aff2 - a_hi.astype(F32)
    a_mid = r1.astype(BF16)
    a_lo = (r1 - a_mid.astype(F32)).astype(BF16)

    slot = lax.broadcasted_iota(jnp.int32, (1, cap), 1).astype(F32)
    row_id = lax.broadcasted_iota(jnp.int32, (n_rows, 1), 0).astype(F32)
    lane_id = lax.broadcasted_iota(jnp.int32, (lanes, 1), 0).astype(F32)
    for ei in range(n_exp):
        rows = slice(ei * n_rows, (ei + 1) * n_rows)
        off_e = offs[rows, 0:1]
        tot_e = tot[rows]
        in_row = jnp.where((off_e <= slot) & (slot < off_e + tot_e), 1.0, 0.0)
        row_base = (in_row * off_e).sum(axis=0, keepdims=True)
        row_of = (in_row * row_id).sum(axis=0, keepdims=True)
        in_row_b = in_row.astype(BF16)
        pick = lambda v: _dot(v, in_row_b, _TN)
        hit = (pick(inrow[rows].astype(BF16)) == slot + 1.0 - row_base) & (pick(sel2[rows].astype(BF16)) > 0.5)
        hit_f = jnp.where(hit, 1.0, 0.0)
        aff_rows = (pick(a_hi[rows]) + pick(a_mid[rows])) + pick(a_lo[rows])
        lane_of = (hit_f * lane_id).sum(axis=0, keepdims=True)
        idx_ref[0, ei:ei + 1, :] = (row_of * lanes + lane_of).astype(jnp.int32)
        gate_ref[0, ei] = (hit_f * aff_rows).sum(axis=0, keepdims=True).T


def _route(aff, cap):
    bx, e, n = aff.shape
    unit = LANES * ROUTE_ROW_ALIGN
    n_pad = -(-n // unit) * unit
    if n_pad != n:
        aff = jnp.pad(aff, ((0, 0), (0, 0), (0, n_pad - n)), constant_values=-1.0)
    n_rows = n_pad // LANES
    tri = np.triu(np.ones((LANES, LANES), np.float32))
    r = np.arange(e * n_rows)
    lmat = ((r[:, None] // n_rows == r[None, :] // n_rows) & (r[None, :] < r[:, None])).astype(np.float32)
    return pl.pallas_call(
        functools.partial(_route_kernel, cap=cap),
        out_shape=[jax.ShapeDtypeStruct((bx, e, cap), jnp.int32), jax.ShapeDtypeStruct((bx, e, cap, 1), F32)],
        grid=(bx,),
        in_specs=[
            pl.BlockSpec((1, e, n_rows, LANES), lambda b: (b, 0, 0, 0)),
            pl.BlockSpec((LANES, LANES), lambda b: (0, 0)),
            pl.BlockSpec((e * n_rows, e * n_rows), lambda b: (0, 0)),
        ],
        out_specs=[pl.BlockSpec((1, e, cap), lambda b: (b, 0, 0)), pl.BlockSpec((1, e, cap, 1), lambda b: (b, 0, 0, 0))],
        compiler_params=_params(("parallel",)),
        name="moe_route",
    )(aff.reshape(bx, e, n_rows, LANES), jnp.asarray(tri, BF16), jnp.asarray(lmat, BF16))


def _moe_kernel(idx_ref, h_hbm, gate_ref, g2_ref, wg_ref, wu_ref, wd_ref, x_hbm, out_hbm,
                xbuf, abuf, xb, yacc, gsem, asem, ssem, *, rows, cap, chunk, n_tiles, n_ff):
    del x_hbm
    tr = xbuf.shape[1] // rows
    e, p, f = pl.program_id(0), pl.program_id(1), pl.program_id(2)
    n_steps = pl.num_programs(0) * n_tiles
    k = e * n_tiles + p
    slot = lax.rem(k, 2)
    lag = 2 if n_tiles >= 2 else 1

    def row_copy(step, i, src, dst, dslot, sem, gather, priority):
        tok = idx_ref[step * rows + i]
        hbm_row = src.at[pl.ds(pl.multiple_of(tok * tr, tr), tr)]
        buf_row = dst.at[dslot, pl.ds(pl.multiple_of(i * tr, tr), tr)]
        cp = (pltpu.make_async_copy(hbm_row, buf_row, sem.at[dslot]) if gather
              else pltpu.make_async_copy(buf_row, hbm_row, sem.at[dslot]))
        cp.start(priority=priority)

    def row_copies(step, src, dst, dslot, sem, gather):
        def body(i, carry):
            for u in range(2):
                row_copy(step, 2 * i + u, src, dst, dslot, sem, gather, u)
            return carry

        lax.fori_loop(0, rows // 2, body, 0, unroll=4)

    def row_copies_inline(step, lo, count, src, dst, dslot, sem, gather):
        for u in range(count):
            row_copy(step, lo + u, src, dst, dslot, sem, gather, u % 2)

    def wait_rows(buf, bslot, sem):
        pltpu.make_async_copy(buf.at[bslot], buf.at[bslot], sem.at[bslot]).wait()

    na = abuf.shape[0]
    pipelined = na >= 3
    acur = lax.rem(k, na) if pipelined else slot
    anext, aprev = lax.rem(k + 1, na), lax.rem(k + na - 1, na)

    @pl.when(f == 0)
    def _begin():
        @pl.when(k == 0)
        def _():
            row_copies(k, h_hbm, xbuf, slot, gsem, True)
            if pipelined:
                row_copies(k, out_hbm, abuf, acur, asem, True)
                row_copies(k, out_hbm, abuf, aprev, asem, True)
                wait_rows(abuf, aprev, asem)

        if pipelined:
            @pl.when(k == 1)
            def _():
                wait_rows(abuf, na - 1, ssem)

            @pl.when(k >= na - 1)
            def _():
                wait_rows(abuf, anext, ssem)
        else:
            @pl.when(k >= lag)
            def _():
                wait_rows(abuf, lax.rem(k - lag, 2), ssem)

            row_copies(k, out_hbm, abuf, slot, asem, True)
        wait_rows(xbuf, slot, gsem)
        for j in range(tr):
            xb[:, j * LANES:(j + 1) * LANES] = xbuf[slot, pl.ds(j, rows, stride=tr), :].astype(BF16)
        yacc[...] = jnp.zeros_like(yacc)
        if pipelined:
            wait_rows(abuf, acur, asem)

    wg = wg_ref[0, 0].astype(BF16)
    wu = wu_ref[0, 0].astype(BF16)
    wd = wd_ref[0, 0].astype(BF16)
    n_chunks = rows // chunk
    ahead = rows // (n_ff * n_chunks)
    nxt = jnp.minimum(k + 1, n_steps - 1)
    prv = jnp.maximum(k - 1, 0)
    for c in range(n_chunks):
        rs = slice(c * chunk, (c + 1) * chunk)
        lo = (f * n_chunks + c) * ahead
        x = xb[rs]
        a = _dot(x, wg)
        row_copies_inline(nxt, lo, ahead, h_hbm, xbuf, 1 - slot, gsem, True)
        u = _dot(x, wu)
        if pipelined:
            row_copies_inline(nxt, lo, ahead, out_hbm, abuf, anext, asem, True)
        hcat = ((a * jax.nn.sigmoid(a)) * u).astype(BF16)
        yacc[rs] += _dot(hcat, wd)
        if pipelined:
            row_copies_inline(prv, lo, ahead, out_hbm, abuf, aprev, ssem, False)

    @pl.when(f == pl.num_programs(2) - 1)
    def _finish():
        if not pipelined:
            wait_rows(abuf, acur, asem)
        for c in range(n_chunks):
            rs = slice(c * chunk, (c + 1) * chunk)
            smp, off = divmod(c * chunk, cap)
            y = yacc[rs] * (gate_ref[smp, 0, off:off + chunk] * g2_ref[smp])
            for j in range(tr):
                abuf[acur, pl.ds(c * chunk * tr + j, chunk, stride=tr), :] += y[:, j * LANES:(j + 1) * LANES]
        if not pipelined:
            row_copies(k, out_hbm, abuf, slot, ssem, False)

        @pl.when(k == n_steps - 1)
        def _():
            wait_rows(xbuf, 1 - slot, gsem)
            if pipelined:
                wait_rows(abuf, anext, asem)
                row_copies(k, out_hbm, abuf, acur, ssem, False)
                for back in range(na - 1):
                    wait_rows(abuf, lax.rem(k + na - back, na), ssem)
            else:
                if lag == 2:
                    @pl.when(k >= 1)
                    def _():
                        wait_rows(abuf, 1 - slot, ssem)
                wait_rows(abuf, slot, ssem)


def _ec_moe(x, h, aff, g2, layer, w_gate, w_up, w_down, dims):
    bx, n, d = dims
    tr = d // LANES
    e = aff.shape[1]
    ff = w_gate.shape[3]
    cap = EC_CAPACITY * n // e
    idx, gate = _route(aff, cap)
    rows_of = (idx + (jnp.arange(bx, dtype=jnp.int32) * n)[:, None, None])
    rows_of = jnp.transpose(rows_of, (1, 0, 2)).reshape(e * bx * cap)
    g2 = jnp.broadcast_to(g2, (bx, 1, d))
    nb = max(1, min(bx, MOE_TILE_ROWS // cap))
    assert bx % nb == 0
    rows = nb * cap
    n_tiles = bx // nb
    chunk = min(MOE_ROW_CHUNK, cap)
    assert cap % chunk == 0
    tf = min(FF_TILE_SMALL if rows <= FF_SMALL_ROWS else FF_TILE, ff)
    assert rows % ((ff // tf) * (rows // chunk)) == 0
    n_abuf = min(4, n_tiles) if n_tiles >= 3 else 2
    grid_spec = pltpu.PrefetchScalarGridSpec(
        num_scalar_prefetch=1,
        grid=(e, n_tiles, ff // tf),
        in_specs=[
            pl.BlockSpec(memory_space=pl.ANY),
            pl.BlockSpec((nb, 1, cap, 1), lambda ei, p, f, idx: (p, ei, 0, 0)),
            pl.BlockSpec((nb, 1, d), lambda ei, p, f, idx: (p, 0, 0)),
            pl.BlockSpec((1, 1, d, tf), lambda ei, p, f, idx: (layer, ei, 0, f)),
            pl.BlockSpec((1, 1, d, tf), lambda ei, p, f, idx: (layer, ei, 0, f)),
            pl.BlockSpec((1, 1, tf, d), lambda ei, p, f, idx: (layer, ei, f, 0)),
            pl.BlockSpec(memory_space=pl.ANY),
        ],
        out_specs=pl.BlockSpec(memory_space=pl.ANY),
        scratch_shapes=[
            pltpu.VMEM((2, rows * tr, LANES), F32),
            pltpu.VMEM((n_abuf, rows * tr, LANES), F32),
            pltpu.VMEM((rows, d), BF16),
            pltpu.VMEM((rows, d), F32),
            pltpu.SemaphoreType.DMA((2,)),
            pltpu.SemaphoreType.DMA((n_abuf,)),
            pltpu.SemaphoreType.DMA((n_abuf,)),
        ],
    )
    return pl.pallas_call(
        functools.partial(_moe_kernel, rows=rows, cap=cap, chunk=chunk, n_tiles=n_tiles, n_ff=ff // tf),
        out_shape=jax.ShapeDtypeStruct(x.shape, F32),
        grid_spec=grid_spec,
        input_output_aliases={7: 0},
        compiler_params=_params(("arbitrary", "arbitrary", "arbitrary"), VMEM_LIMIT),
        name="moe_ffn",
    )(rows_of, h, gate, g2, w_gate, w_up, w_down, x)


def kernel(x, c, ctx, c_ctx, w_mod, b_mod, norm_mix, norm_ffn, a_w_qkv, a_q_gain, a_k_gain, a_sink, a_w_o,
           b_w_qkv, b_q_gain, b_k_gain, b_rpb, b_w_o, moe_router, moe_w_gate, moe_w_up, moe_w_down):
    b, s, d = x.shape
    depth = w_mod.shape[0]
    pad = (-(b + 1)) % 8
    cvec = jnp.concatenate([c, c_ctx[None], jnp.zeros((pad, d), F32)], axis=0)
    mods = _modulation(cvec, w_mod, b_mod)
    rope_tabs = _rope_tables(s)
    xl, xc = x, ctx
    for i in range(depth):
        last = i == depth - 1
        til_l = dict(dims=x.shape) if i > 0 else {}
        til_c = dict(dims=ctx.shape) if i > 0 else {}
        ml =[m.reshape(b, 1, d) for m in jnp.split(mods[i, :b], 6, axis=-1)]
        mc = [m.reshape(1, 1, d) for m in jnp.split(mods[i, b:b + 1], 6, axis=-1)]
        sh1_l, sc1_l, g1_l, sh2_l, sc2_l, g2_l = ml
        sh1_c, sc1_c, g1_c, sh2_c, sc2_c, g2_c = mc
        j = i // N_MIXERS
        if i % N_MIXERS == 0:
            ql, kl, vl = _project(xl, norm_mix[i], sh1_l, sc1_l, a_w_qkv[j], a_q_gain[j], a_k_gain[j],
                                  A_HEADS, A_KV_HEADS, rope_tabs, **til_l)
            qc, kc, vc = _project(xc, norm_mix[i], sh1_c, sc1_c, a_w_qkv[j], a_q_gain[j], a_k_gain[j],
                                  A_HEADS, A_KV_HEADS, **til_c)
            ol = _attn_a(ql, kl, vl, kc, vc, a_sink[j])
            oc = None if last else _attn_ctx(qc, kc, vc, a_sink[j])
            w_o = a_w_o[j]
        else:
            ql, kl, vl = _project(xl, norm_mix[i], sh1_l, sc1_l, b_w_qkv[j], b_q_gain[j], b_k_gain[j],
                                  B_HEADS, B_HEADS, **til_l)
            qc, kc, vc = _project(xc, norm_mix[i], sh1_c, sc1_c, b_w_qkv[j], b_q_gain[j], b_k_gain[j],
                                  B_HEADS, B_HEADS, **til_c)
            ol = _attn_b(ql, kl, vl, kc, vc, b_rpb[j])
            oc = None if last else _attn_ctx(qc, kc, vc, None)
            w_o = b_w_o[j]
        xl, hl, aff_l = _out_proj(ol, w_o, xl, g1_l, norm_ffn[i], sh2_l, sc2_l, moe_router[i], **til_l)
        xl = _ec_moe(xl, hl, aff_l, g2_l, i, moe_w_gate, moe_w_up, moe_w_down, x.shape)
        if not last:
            xc, hc, aff_c = _out_proj(oc, w_o, xc, g1_c, norm_ffn[i], sh2_c, sc2_c, moe_router[i], **til_c)
            xc = _ec_moe(xc, hc, aff_c, g2_c, i, moe_w_gate, moe_w_up, moe_w_down, ctx.shape)
    return _untile(xl, x.shape)
```

```python
import functools

import numpy as np
import jax
import jax.numpy as jnp
from jax import lax
from jax.experimental import pallas as pl
from jax.experimental.pallas import tpu as pltpu

F32 = jnp.float32
BF16 = jnp.bfloat16

GRID_W = 64
HEAD_DIM = 64
A_HEADS = 16
A_KV_HEADS = 4
A_WINDOW = 128
A_BLOCK = 128
Q_BLOCK = 128
B_HEADS = 16
NA_ROWS = 8
NA_COLS = 16
N_EXPERTS = 16
EC_CAPACITY = 2
ROPE_BASE = 10000.0
EPS = 1e-6
NEG_INF = -1e30
ATTN_SCALE = HEAD_DIM ** -0.5
LOG2E = 1.4426950408889634
N_MIXERS = 2

NA_QROWS = 2
NA_KROWS = NA_ROWS + NA_QROWS
NA_HEAD_GROUP = 16

LANES = 128
TOKEN_TILE = 1024
VMEM_LIMIT = 56 * 1024 * 1024

_NT = (((1,), (1,)), ((), ()))
_NN = (((1,), (0,)), ((), ()))


def _dot(a, b, dims=_NN):
    return lax.dot_general(a, b, dims, preferred_element_type=F32)


def _split(a):
    hi = a.astype(BF16)
    lo = (a - hi.astype(F32)).astype(BF16)
    return hi, lo


def _dot3(a, b, dims=_NN):
    ah, al = _split(a)
    bh, bl = _split(b)
    return _dot(ah, bh, dims) + (_dot(ah, bl, dims) + _dot(al, bh, dims))


def _params(sem, vmem=None):
    return pltpu.CompilerParams(dimension_semantics=sem, vmem_limit_bytes=vmem)


def _mod_kernel(c_ref, w_ref, b_ref, o_ref):
    c = c_ref[...]
    a = c * jax.nn.sigmoid(c)
    o_ref[0] = _dot3(a, w_ref[0]) + b_ref[0]


def _modulation(cvec, w_mod, b_mod):
    depth, d, n = w_mod.shape
    rows = cvec.shape[0]
    tn = 1024
    return pl.pallas_call(
        _mod_kernel,
        out_shape=jax.ShapeDtypeStruct((depth, rows, n), F32),
        grid=(depth, n // tn),
        in_specs=[
            pl.BlockSpec((rows, d), lambda i, j: (0, 0)),
            pl.BlockSpec((1, d, tn), lambda i, j: (i, 0, j)),
            pl.BlockSpec((1, 1, tn), lambda i, j: (i, 0, j)),
        ],
        out_specs=pl.BlockSpec((1, rows, tn), lambda i, j: (i, 0, j)),
        compiler_params=_params(("parallel", "parallel")),
        name="modulation",
    )(cvec, w_mod, b_mod.reshape(depth, 1, n))


def _rms_mod(x, g, shift, scale):
    y = x * lax.rsqrt(jnp.mean(x * x, axis=-1, keepdims=True) + EPS)
    return (y * g) * (1.0 + scale) + shift


def _load_tokens(ref, tm):
    chunks = ref.shape[0] // tm
    return jnp.concatenate([ref[pl.ds(j, tm, stride=chunks), :] for j in range(chunks)], axis=1)


def _store_tokens(ref, val):
    tm, d = val.shape
    chunks = d // LANES
    for j in range(chunks):
        ref[pl.ds(j, tm, stride=chunks), :] = val[:, j * LANES:(j + 1) * LANES]


def _token_spec(tm, d, n_tiles, tiled):
    if tiled:
        return pl.BlockSpec((tm * (d // LANES), LANES), lambda b, t: (b * n_tiles + t, 0))
    return pl.BlockSpec((1, tm, d), lambda b, t: (b, t, 0))


def _proj_kernel(*refs, n_q, n_kv, rope, tiled):
    if rope:
        x_ref, g_ref, sh_ref, sc_ref, wt_ref, cq_ref, ck_ref, sq_ref, sk_ref, qt_ref, k_ref, vt_ref = refs
    else:
        x_ref, g_ref, sh_ref, sc_ref, wt_ref, cq_ref, ck_ref, qt_ref, k_ref, vt_ref = refs
        sq_ref = sk_ref = None
    dh = HEAD_DIM
    x = _load_tokens(x_ref, k_ref.shape[1]) if tiled else x_ref[0]
    h = _rms_mod(x, g_ref[...], sh_ref[0], sc_ref[0]).astype(BF16)
    res = _dot(wt_ref[...], h, _NT)

    def head(j, c_ref, s_ref):
        t = res[j * dh:(j + 1) * dh]
        tn = t * lax.rsqrt(jnp.mean(t * t, axis=0, keepdims=True) + EPS)
        out = tn * c_ref[...]
        if rope:
            q4 = dh // 4
            partner = jnp.concatenate([tn[q4:2 * q4], tn[:q4], tn[3 * q4:], tn[2 * q4:3 * q4]], axis=0)
            out = out + partner * s_ref[...]
        return out

    for j in range(n_q):
        qh = head(j, cq_ref, sq_ref).astype(BF16)
        for qb in range(qt_ref.shape[1]):
            qt_ref[0, qb, j] = qh[:, qb * Q_BLOCK:(qb + 1) * Q_BLOCK]
    ks =[head(n_q + j, ck_ref, sk_ref) for j in range(n_kv)]
    k_ref[0] = jnp.concatenate(ks, axis=0).T.astype(BF16)
    v0 = (n_q + n_kv) * dh
    for j in range(n_kv):
        vt_ref[0, j] = res[v0 + j * dh:v0 + (j + 1) * dh].astype(BF16)


def _rope_perm():
    d = np.arange(HEAD_DIM)
    return np.where((d % 32) < 16, d + 16, d - 16)


def _rope_tables(n_tok):
    pos = jnp.arange(n_tok)
    row = (pos // GRID_W).astype(F32)
    col = (pos % GRID_W).astype(F32)
    n_freq = HEAD_DIM // 4
    inv = ROPE_BASE ** (-jnp.arange(n_freq, dtype=F32) / n_freq)
    ang_r = row[:, None] * inv
    ang_c = col[:, None] * inv
    cos_t = jnp.concatenate([jnp.cos(ang_r), jnp.cos(ang_r), jnp.cos(ang_c), jnp.cos(ang_c)], axis=-1)
    sin_t = jnp.concatenate([-jnp.sin(ang_r), jnp.sin(ang_r), -jnp.sin(ang_c), jnp.sin(ang_c)], axis=-1)
    return cos_t.T, sin_t.T


def _project(x, g, shift, scale, w, q_gain, k_gain, n_q, n_kv, rope_tabs=None, dims=None):
    tiled = dims is not None
    bx, n, d = dims if tiled else x.shape
    dh = HEAD_DIM
    rope = rope_tabs is not None
    tm = min(TOKEN_TILE, n)
    qg = (q_gain.astype(F32) * (ATTN_SCALE * LOG2E)).reshape(dh, 1)
    kg = k_gain.astype(F32).reshape(dh, 1)
    wt = w.T.astype(BF16)
    n_cols = wt.shape[0]
    bm = shift.shape[0]
    mod_map = (lambda b, t: (b, 0, 0)) if bm == bx else (lambda b, t: (0, 0, 0))
    const2 = lambda b, t: (0, 0)
    tok_map = lambda b, t: (0, t)
    in_specs = [
        _token_spec(tm, d, n // tm, tiled),
        pl.BlockSpec((1, d), const2),
        pl.BlockSpec((1, 1, d), mod_map),
        pl.BlockSpec((1, 1, d), mod_map),
        pl.BlockSpec((n_cols, d), const2),
    ]
    args = [x, g.reshape(1, d), shift, scale, wt]
    if rope:
        cos_t, sin_t = rope_tabs
        perm = _rope_perm()
        args += [qg * cos_t, kg * cos_t, qg[perm] * sin_t, kg[perm] * sin_t]
        in_specs += [pl.BlockSpec((dh, tm), tok_map)] * 4
    else:
        args += [jnp.broadcast_to(qg, (dh, tm)), jnp.broadcast_to(kg, (dh, tm))]
        in_specs += [pl.BlockSpec((dh, tm), const2)] * 2
    assert tm % Q_BLOCK == 0
    out_shape = [jax.ShapeDtypeStruct((bx, n // Q_BLOCK, n_q, dh, Q_BLOCK), BF16),
                 jax.ShapeDtypeStruct((bx, n, n_kv * dh), BF16),
                 jax.ShapeDtypeStruct((bx, n_kv, dh, n), BF16)]
    out_specs = [pl.BlockSpec((1, tm // Q_BLOCK, n_q, dh, Q_BLOCK), lambda b, t: (b, t, 0, 0, 0)),
                 pl.BlockSpec((1, tm, n_kv * dh), lambda b, t: (b, t, 0)),
                 pl.BlockSpec((1, n_kv, dh, tm), lambda b, t: (b, 0, 0, t))]
    return pl.pallas_call(
        functools.partial(_proj_kernel, n_q=n_q, n_kv=n_kv, rope=rope, tiled=tiled),
        out_shape=out_shape,
        grid=(bx, n // tm),
        in_specs=in_specs,
        out_specs=out_specs,
        compiler_params=_params(("parallel", "parallel"), VMEM_LIMIT),
        name="qkv_proj_rope" if rope else "qkv_proj",
    )(*args)


def _softmax_pv_t(scores, vts, sinks):
    ms = []
    for parts, sink in zip(scores, sinks):
        m = parts[0].max(axis=0, keepdims=True)
        for s in parts[1:]:
            m = jnp.maximum(m, s.max(axis=0, keepdims=True))
        ms.append(m if sink is None else jnp.maximum(m, sink))
    probs, dens = [], []
    for parts, m, sink in zip(scores, ms, sinks):
        ps = [jnp.exp2(s - m) for s in parts]
        den = ps[0].sum(axis=0, keepdims=True)
        for p in ps[1:]:
            den = den + p.sum(axis=0, keepdims=True)
        dens.append(den if sink is None else den + jnp.exp2(sink - m))
        probs.append([p.astype(BF16) for p in ps])
    outs = []
    for ps, vt_parts, den in zip(probs, vts, dens):
        acc = _dot(vt_parts[0], ps[0])
        for vt, p in zip(vt_parts[1:], ps[1:]):
            acc = acc + _dot(vt, p)
        outs.append(acc / den)
    return outs


def _attn_a_kernel(qt_ref, k_ref, vt_ref, kc_ref, vct_ref, sink_ref, mask_ref, o_ref, *, seq, n_kv, group):
    blk = A_BLOCK
    span = 3 * blk
    dh = HEAD_DIM
    i = pl.program_id(1)
    start = pl.multiple_of(jnp.clip((i - 1) * blk, 0, seq - span), blk)
    scores, vts = [], []
    for h in range(n_kv):
        qt = jnp.concatenate([qt_ref[0, 0, h * group + g] for g in range(group)], axis=1)
        kw = k_ref[0, pl.ds(start, span), h * dh:(h + 1) * dh]
        s_loc = _dot(kw, qt) + mask_ref[0]
        s_ctx = _dot(kc_ref[0, :, h * dh:(h + 1) * dh], qt)
        scores.append([s_loc, s_ctx])
        vts.append([vt_ref[0, h, :, pl.ds(start, span)], vct_ref[0, h]])
    ots = _softmax_pv_t(scores, vts, [sink_ref[h] for h in range(n_kv)])
    outs = [ot[:, g * blk:(g + 1) * blk] for ot in ots for g in range(group)]
    o_ref[0] = jnp.concatenate(outs, axis=0).T.astype(BF16)


def _sink_rows(sink, n_kv, group, cols):
    s = (sink.astype(F32) * LOG2E).reshape(n_kv, 1, group, 1)
    return jnp.broadcast_to(s, (n_kv, 1, group, cols)).reshape(n_kv, 1, group * cols)


def _attn_a(qt, k, vt, kc, vct, sink):
    b, _, hq, dh, blk = qt.shape
    s = k.shape[1]
    n_kv = vt.shape[1]
    group = hq // n_kv
    nc = kc.shape[1]
    assert blk == A_BLOCK
    n_blk = s // blk
    assert s % blk == 0 and n_blk >= 3
    off = (np.arange(3) * blk)[:, None, None]
    rel = off + (np.arange(group * blk) % blk)[None, None, :] - np.arange(3 * blk)[None, :, None]
    mask = jnp.asarray(np.where(np.abs(rel) <= A_WINDOW, 0.0, NEG_INF).astype(np.float32))
    return pl.pallas_call(
        functools.partial(_attn_a_kernel, seq=s, n_kv=n_kv, group=group),
        out_shape=jax.ShapeDtypeStruct((b, s, hq * dh), BF16),
        grid=(b, s // blk),
        in_specs=[
            pl.BlockSpec((1, 1, hq, dh, blk), lambda bi, i: (bi, i, 0, 0, 0)),
            pl.BlockSpec((1, s, n_kv * dh), lambda bi, i: (bi, 0, 0)),
            pl.BlockSpec((1, n_kv, dh, s), lambda bi, i: (bi, 0, 0, 0)),
            pl.BlockSpec((1, nc, n_kv * dh), lambda bi, i: (bi, 0, 0)),
            pl.BlockSpec((1, n_kv, dh, nc), lambda bi, i: (bi, 0, 0, 0)),
            pl.BlockSpec((n_kv, 1, group * blk), lambda bi, i: (0, 0, 0)),
            pl.BlockSpec((1, 3 * blk, group * blk),
                         lambda bi, i: (jnp.where(i == 0, 0, jnp.where(i == n_blk - 1, 2, 1)), 0, 0)),
        ],
        out_specs=pl.BlockSpec((1, blk, hq * dh), lambda bi, i: (bi, i, 0)),
        compiler_params=_params(("parallel", "arbitrary"), VMEM_LIMIT),
        name="attn_window",
    )(qt, k, vt, kc, vct, _sink_rows(sink, n_kv, group, blk), mask)


def _attn_ctx_kernel(qt_ref, k_ref, vt_ref, sink_ref, o_ref, *, n_kv, group, use_sink):
    dh = HEAD_DIM
    n_qb = qt_ref.shape[1]
    nq = n_qb * qt_ref.shape[4]
    scores, vts = [], []
    for h in range(n_kv):
        qt = jnp.concatenate([qt_ref[0, qb, h * group + g] for g in range(group) for qb in range(n_qb)], axis=1)
        scores.append([_dot(k_ref[0, :, h * dh:(h + 1) * dh], qt)])
        vts.append([vt_ref[0, h]])
    ots = _softmax_pv_t(scores, vts, [sink_ref[h] if use_sink else None for h in range(n_kv)])
    outs = [ot[:, g * nq:(g + 1) * nq] for ot in ots for g in range(group)]
    o_ref[0] = jnp.concatenate(outs, axis=0).T.astype(BF16)


def _attn_ctx(qt, k, vt, sink):
    b, n_qb, hq, dh, qblk = qt.shape
    n = n_qb * qblk
    n_kv = vt.shape[1]
    group = hq // n_kv
    use_sink = sink is not None
    sink_r = _sink_rows(sink if use_sink else jnp.zeros((hq,), F32), n_kv, group, n)
    return pl.pallas_call(
        functools.partial(_attn_ctx_kernel, n_kv=n_kv, group=group, use_sink=use_sink),
        out_shape=jax.ShapeDtypeStruct((b, n, hq * dh), BF16),
        grid=(b,),
        in_specs=[
            pl.BlockSpec((1, n_qb, hq, dh, qblk), lambda bi: (bi, 0, 0, 0, 0)),
            pl.BlockSpec((1, n, n_kv * dh), lambda bi: (bi, 0, 0)),
            pl.BlockSpec((1, n_kv, dh, n), lambda bi: (bi, 0, 0, 0)),
            pl.BlockSpec((n_kv, 1, group * n), lambda bi: (0, 0, 0)),
        ],
        out_specs=pl.BlockSpec((1, n, hq * dh), lambda bi: (bi, 0, 0)),
        compiler_params=_params(("parallel",)),
        name="attn_ctx",
    )(qt, k, vt, sink_r)


def _na_bias_tables(rpb, n_rows):
    w = GRID_W
    n_blocks = n_rows // NA_QROWS
    half = NA_ROWS // 2
    cfgs, cfg_of = {}, []
    for rb in range(n_blocks):
        r0 = rb * NA_QROWS
        rs = int(np.clip(r0 - half, 0, n_rows - NA_KROWS))
        key = (r0 - rs,) + tuple(int(np.clip(r0 + a - half, 0, n_rows - NA_ROWS)) - rs for a in range(NA_QROWS))
        cfg_of.append(cfgs.setdefault(key, len(cfgs)))
    kc = np.arange(w)[:, None]
    qc = np.arange(w)[None, :]
    wstart = np.clip(qc - NA_COLS // 2, 0, w - NA_COLS)
    col_ok = (kc >= wstart) & (kc < wstart + NA_COLS)
    dcol = np.clip(kc - qc, -(NA_COLS - 1), NA_COLS - 1) + NA_COLS - 1
    onehot = (dcol[None] == np.arange(2 * NA_COLS - 1)[:, None, None]).astype(np.float32)
    tiles = jnp.einsum('hrd,dkq->hrkq', rpb.astype(F32), jnp.asarray(onehot), precision=lax.Precision.HIGHEST)
    tiles = jnp.where(col_ok[None, None], tiles * LOG2E, NEG_INF)
    masked = jnp.full((rpb.shape[0], w, w), NEG_INF, F32)
    tabs = []
    for key in cfgs:
        off, rsr = key[0], key[1:]
        rows = []
        for c in range(NA_KROWS):
            cols = []
            for a in range(NA_QROWS):
                assert 0 <= rsr[a] and rsr[a] + NA_ROWS <= NA_KROWS
                row_ok = rsr[a] <= c < rsr[a] + NA_ROWS
                cols.append(tiles[:, c - off - a + NA_ROWS - 1] if row_ok else masked)
            rows.append(jnp.concatenate(cols, axis=2))
        tabs.append(jnp.concatenate(rows, axis=1))
    return jnp.asarray(np.array(cfg_of, np.int32)), jnp.stack(tabs)


def _attn_b_kernel(cfg_ref, qt_ref, k_ref, vt_ref, kc_ref, vct_ref, bias_ref, o_ref, *, n_rows, heads):
    del cfg_ref
    w = GRID_W
    dh = HEAD_DIM
    rb = pl.program_id(2)
    rs2 = jnp.clip(rb - NA_ROWS // (2 * NA_QROWS), 0, (n_rows - NA_KROWS) // NA_QROWS)
    start = pl.multiple_of(rs2 * (NA_QROWS * w), NA_QROWS * w)
    span = NA_KROWS * w
    scores, vts = [], []
    for h in range(heads):
        qt = qt_ref[0, 0, h]
        kw = k_ref[0, pl.ds(start, span), h * dh:(h + 1) * dh]
        s_loc = _dot(kw, qt) + bias_ref[0, h]
        s_ctx = _dot(kc_ref[0, :, h * dh:(h + 1) * dh], qt)
        scores.append([s_loc, s_ctx])
        vts.append([vt_ref[0, h, :, pl.ds(start, span)], vct_ref[0, h]])
    outs = _softmax_pv_t(scores, vts, [None] * heads)
    o_ref[0] = jnp.concatenate(outs, axis=0).T.astype(BF16)


def _attn_b(qt, k, vt, kc, vct, rpb):
    b, _, hq, dh, qblk = qt.shape
    s = k.shape[1]
    nc = kc.shape[1]
    w = GRID_W
    n_rows = s // w
    assert s % w == 0 and n_rows >= NA_KROWS and n_rows % NA_QROWS == 0 and qblk == NA_QROWS * w
    cfg_of, table = _na_bias_tables(rpb, n_rows)
    hg = NA_HEAD_GROUP
    tq = NA_QROWS * w
    span = NA_KROWS * w
    grid_spec = pltpu.PrefetchScalarGridSpec(
        num_scalar_prefetch=1,
        grid=(b, hq // hg, n_rows // NA_QROWS),
        in_specs=[
            pl.BlockSpec((1, 1, hg, dh, tq), lambda bi, g, r, cfg: (bi, r, g, 0, 0)),
            pl.BlockSpec((1, s, hg * dh), lambda bi, g, r, cfg: (bi, 0, g), pipeline_mode=pl.Buffered(1)),
            pl.BlockSpec((1, hg, dh, s), lambda bi, g, r, cfg: (bi, g, 0, 0), pipeline_mode=pl.Buffered(1)),
            pl.BlockSpec((1, nc, hg * dh), lambda bi, g, r, cfg: (bi, 0, g)),
            pl.BlockSpec((1, hg, dh, nc), lambda bi, g, r, cfg: (bi, g, 0, 0)),
            pl.BlockSpec((1, hg, span, tq), lambda bi, g, r, cfg: (cfg[r], g, 0, 0)),
        ],
        out_specs=pl.BlockSpec((1, tq, hg * dh), lambda bi, g, r, cfg: (bi, r, g)),
    )
    return pl.pallas_call(
        functools.partial(_attn_b_kernel, n_rows=n_rows, heads=hg),
        out_shape=jax.ShapeDtypeStruct((b, s, hq * dh), BF16),
        grid_spec=grid_spec,
        compiler_params=_params(("parallel", "parallel", "arbitrary"), VMEM_LIMIT),
        name="attn_neighbourhood",
    )(cfg_of, qt, k, vt, kc, vct, table)


def _out_kernel(o_ref, wo_ref, x_ref, g1_ref, g_ref, sh_ref, sc_ref, wr_ref, xn_ref, h_ref, aff_ref, *, tiled):
    y = _dot(o_ref[0], wo_ref[...])
    x = _load_tokens(x_ref, o_ref.shape[1]) if tiled else x_ref[0]
    xn = x + g1_ref[0] * y
    _store_tokens(xn_ref, xn)
    h = _rms_mod(xn, g_ref[...], sh_ref[0], sc_ref[0])
    _store_tokens(h_ref, h)
    logits = _dot3(wr_ref[...], h, _NT)
    z = jnp.exp(logits - logits.max(axis=0, keepdims=True))
    aff_ref[0] = z / z.sum(axis=0, keepdims=True)


def _out_proj(o, w_o, x, g1, g, shift, scale, w_router, dims=None):
    tiled = dims is not None
    bx, n, d = dims if tiled else x.shape
    assert d % (8 * LANES) == 0
    da = o.shape[2]
    e = w_router.shape[1]
    tm = min(TOKEN_TILE, n)
    bm = g1.shape[0]
    mod_map = (lambda b, t: (b, 0, 0)) if bm == bx else (lambda b, t: (0, 0, 0))
    const2 = lambda b, t: (0, 0)
    mod_spec = pl.BlockSpec((1, 1, d), mod_map)
    tiled_shape = jax.ShapeDtypeStruct((bx * n * (d // LANES), LANES), F32)
    return pl.pallas_call(
        functools.partial(_out_kernel, tiled=tiled),
        out_shape=[tiled_shape, tiled_shape, jax.ShapeDtypeStruct((bx, e, n), F32)],
        grid=(bx, n // tm),
        in_specs=[
            pl.BlockSpec((1, tm, da), lambda b, t: (b, t, 0)),
            pl.BlockSpec((da, d), const2),
            _token_spec(tm, d, n // tm, tiled),
            mod_spec,
            pl.BlockSpec((1, d), const2),
            mod_spec,
            mod_spec,
            pl.BlockSpec((e, d), const2),
        ],
        out_specs=[_token_spec(tm, d, n // tm, True),
                   _token_spec(tm, d, n // tm, True),
                   pl.BlockSpec((1, e, tm), lambda b, t: (b, 0, t))],
        compiler_params=_params(("parallel", "parallel"), VMEM_LIMIT),
        name="out_proj_router",
    )(o, w_o.astype(BF16), x, g1, g.reshape(1, d), shift, scale, w_router.T)


def _untile_kernel(x_ref, o_ref):
    o_ref[0] = _load_tokens(x_ref, o_ref.shape[1])


def _untile(x, dims):
    bx, n, d = dims
    tm = min(TOKEN_TILE, n)
    return pl.pallas_call(
        _untile_kernel,
        out_shape=jax.ShapeDtypeStruct(dims, F32),
        grid=(bx, n // tm),
        in_specs=[_token_spec(tm, d, n // tm, True)],
        out_specs=_token_spec(tm, d, n // tm, False),
        compiler_params=_params(("parallel", "parallel")),
        name="untile_tokens",
    )(x)


ROUTE_ROW_ALIGN = 16
ROUTE_BISECT_STEPS = 30
FF_TILE = 512
FF_TILE_SMALL = 1024
FF_SMALL_ROWS = 256
MOE_TILE_ROWS = 1024
MOE_ROW_CHUNK = 512
_TN = (((0,), (0,)), ((), ()))


def _route_kernel(aff_ref, tri_ref, lmat_ref, idx_ref, gate_ref, *, cap):
    n_exp, n_rows, lanes = aff_ref.shape[1:]
    aff = aff_ref[0]

    def reduce_tokens(x, op):
        return op(op(x, axis=1, keepdims=True), axis=2, keepdims=True)

    def count_ge(t):
        return reduce_tokens(jnp.where(aff >= t, 1.0, 0.0), jnp.sum)

    def max_below(t):
        return reduce_tokens(jnp.where(aff < t, aff, -1.0), jnp.max)

    def bisect(_, lohi):
        lo, hi = lohi
        mid = 0.5 * (lo + hi)
        ok = count_ge(mid) >= cap
        return jnp.where(ok, mid, lo), jnp.where(ok, hi, mid)

    top = reduce_tokens(aff, jnp.max)
    _, hi = lax.fori_loop(0, ROUTE_BISECT_STEPS, bisect, (jnp.zeros_like(top), 2.0 * top + 1e-30))

    def short(t):
        return jnp.max(jnp.where(count_ge(t) < cap, 1.0, 0.0)) > 0.5

    def walk(t):
        return jnp.where(count_ge(t) >= cap, t, max_below(t))

    thr = lax.while_loop(short, walk, max_below(hi))
    above = aff > thr
    equal = aff == thr
    need = cap - reduce_tokens(jnp.where(above, 1.0, 0.0), jnp.sum)

    def prefix(x3):
        x2 = x3.reshape(n_exp * n_rows, lanes)
        inrow = _dot(x2.astype(BF16), tri_ref[...])
        tot = inrow[:, lanes - 1:lanes]
        offs = _dot(lmat_ref[...], jnp.broadcast_to(tot, inrow.shape).astype(BF16))
        return inrow, offs, tot

    eq_f = jnp.where(equal, 1.0, 0.0)
    eq_in, eq_off, _ = prefix(eq_f)
    eq_rank = (eq_in + eq_off).reshape(n_exp, n_rows, lanes) - eq_f
    sel_f = jnp.where(above | (equal & (eq_rank < need)), 1.0, 0.0)
    inrow, offs, tot = prefix(sel_f)
    sel2 = sel_f.reshape(n_exp * n_rows, lanes)
    aff2 = aff.reshape(n_exp * n_rows, lanes)
    a_hi = aff2.astype(BF16)
    r1 = aff2 - a_hi.astype(F32)
    a_mid = r1.astype(BF16)
    a_lo = (r1 - a_mid.astype(F32)).astype(BF16)

    slot = lax.broadcasted_iota(jnp.int32, (1, cap), 1).astype(F32)
    row_id = lax.broadcasted_iota(jnp.int32, (n_rows, 1), 0).astype(F32)
    lane_id = lax.broadcasted_iota(jnp.int32, (lanes, 1), 0).astype(F32)
    for ei in range(n_exp):
        rows = slice(ei * n_rows, (ei + 1) * n_rows)
        off_e = offs[rows, 0:1]
        tot_e = tot[rows]
        in_row = jnp.where((off_e <= slot) & (slot < off_e + tot_e), 1.0, 0.0)
        row_base = (in_row * off_e).sum(axis=0, keepdims=True)
        row_of = (in_row * row_id).sum(axis=0, keepdims=True)
        in_row_b = in_row.astype(BF16)
        pick = lambda v: _dot(v, in_row_b, _TN)
        hit = (pick(inrow[rows].astype(BF16)) == slot + 1.0 - row_base) & (pick(sel2[rows].astype(BF16)) > 0.5)
        hit_f = jnp.where(hit, 1.0, 0.0)
        aff_rows = (pick(a_hi[rows]) + pick(a_mid[rows])) + pick(a_lo[rows])
        lane_of = (hit_f * lane_id).sum(axis=0, keepdims=True)
        idx_ref[0, ei:ei + 1, :] = (row_of * lanes + lane_of).astype(jnp.int32)
        gate_ref[0, ei] = (hit_f * aff_rows).sum(axis=0, keepdims=True).T


def _route(aff, cap):
    bx, e, n = aff.shape
    unit = LANES * ROUTE_ROW_ALIGN
    n_pad = -(-n // unit) * unit
    if n_pad != n:
        aff = jnp.pad(aff, ((0, 0), (0, 0), (0, n_pad - n)), constant_values=-1.0)
    n_rows = n_pad // LANES
    tri = np.triu(np.ones((LANES, LANES), np.float32))
    r = np.arange(e * n_rows)
    lmat = ((r[:, None] // n_rows == r[None, :] // n_rows) & (r[None, :] < r[:, None])).astype(np.float32)
    return pl.pallas_call(
        functools.partial(_route_kernel, cap=cap),
        out_shape=[jax.ShapeDtypeStruct((bx, e, cap), jnp.int32), jax.ShapeDtypeStruct((bx, e, cap, 1), F32)],
        grid=(bx,),
        in_specs=[
            pl.BlockSpec((1, e, n_rows, LANES), lambda b: (b, 0, 0, 0)),
            pl.BlockSpec((LANES, LANES), lambda b: (0, 0)),
            pl.BlockSpec((e * n_rows, e * n_rows), lambda b: (0, 0)),
        ],
        out_specs=[pl.BlockSpec((1, e, cap), lambda b: (b, 0, 0)), pl.BlockSpec((1, e, cap, 1), lambda b: (b, 0, 0, 0))],
        compiler_params=_params(("parallel",)),
        name="moe_route",
    )(aff.reshape(bx, e, n_rows, LANES), jnp.asarray(tri, BF16), jnp.asarray(lmat, BF16))


def _moe_kernel(idx_ref, h_hbm, gate_ref, g2_ref, wg_ref, wu_ref, wd_ref, x_hbm, out_hbm,
                xbuf, abuf, xb, yacc, gsem, asem, ssem, *, rows, cap, chunk, n_tiles, n_ff):
    del x_hbm
    tr = xbuf.shape[1] // rows
    e, p, f = pl.program_id(0), pl.program_id(1), pl.program_id(2)
    n_steps = pl.num_programs(0) * n_tiles
    k = e * n_tiles + p
    slot = lax.rem(k, 2)
    lag = 2 if n_tiles >= 2 else 1

    def row_copy(step, i, src, dst, dslot, sem, gather, priority):
        tok = idx_ref[step * rows + i]
        hbm_row = src.at[pl.ds(pl.multiple_of(tok * tr, tr), tr)]
        buf_row = dst.at[dslot, pl.ds(pl.multiple_of(i * tr, tr), tr)]
        cp = (pltpu.make_async_copy(hbm_row, buf_row, sem.at[dslot]) if gather
              else pltpu.make_async_copy(buf_row, hbm_row, sem.at[dslot]))
        cp.start(priority=priority)

    def row_copies(step, src, dst, dslot, sem, gather):
        def body(i, carry):
            for u in range(2):
                row_copy(step, 2 * i + u, src, dst, dslot, sem, gather, u)
            return carry

        lax.fori_loop(0, rows // 2, body, 0, unroll=4)

    def row_copies_inline(step, lo, count, src, dst, dslot, sem, gather):
        for u in range(count):
            row_copy(step, lo + u, src, dst, dslot, sem, gather, u % 2)

    def wait_rows(buf, bslot, sem):
        pltpu.make_async_copy(buf.at[bslot], buf.at[bslot], sem.at[bslot]).wait()

    na = abuf.shape[0]
    pipelined = na >= 3
    acur = lax.rem(k, na) if pipelined else slot
    anext, aprev = lax.rem(k + 1, na), lax.rem(k + na - 1, na)

    @pl.when(f == 0)
    def _begin():
        @pl.when(k == 0)
        def _():
            row_copies(k, h_hbm, xbuf, slot, gsem, True)
            if pipelined:
                row_copies(k, out_hbm, abuf, acur, asem, True)
                row_copies(k, out_hbm, abuf, aprev, asem, True)
                wait_rows(abuf, aprev, asem)

        if pipelined:
            @pl.when(k == 1)
            def _():
                wait_rows(abuf, na - 1, ssem)

            @pl.when(k >= na - 1)
            def _():
                wait_rows(abuf, anext, ssem)
        else:
            @pl.when(k >= lag)
            def _():
                wait_rows(abuf, lax.rem(k - lag, 2), ssem)

            row_copies(k, out_hbm, abuf, slot, asem, True)
        wait_rows(xbuf, slot, gsem)
        for j in range(tr):
            xb[:, j * LANES:(j + 1) * LANES] = xbuf[slot, pl.ds(j, rows, stride=tr), :].astype(BF16)
        yacc[...] = jnp.zeros_like(yacc)
        if pipelined:
            wait_rows(abuf, acur, asem)

    wg = wg_ref[0, 0].astype(BF16)
    wu = wu_ref[0, 0].astype(BF16)
    wd = wd_ref[0, 0].astype(BF16)
    n_chunks = rows // chunk
    ahead = rows // (n_ff * n_chunks)
    nxt = jnp.minimum(k + 1, n_steps - 1)
    prv = jnp.maximum(k - 1, 0)
    for c in range(n_chunks):
        rs = slice(c * chunk, (c + 1) * chunk)
        lo = (f * n_chunks + c) * ahead
        x = xb[rs]
        a = _dot(x, wg)
        row_copies_inline(nxt, lo, ahead, h_hbm, xbuf, 1 - slot, gsem, True)
        u = _dot(x, wu)
        if pipelined:
            row_copies_inline(nxt, lo, ahead, out_hbm, abuf, anext, asem, True)
        hcat = ((a * jax.nn.sigmoid(a)) * u).astype(BF16)
        yacc[rs] += _dot(hcat, wd)
        if pipelined:
            row_copies_inline(prv, lo, ahead, out_hbm, abuf, aprev, ssem, False)

    @pl.when(f == pl.num_programs(2) - 1)
    def _finish():
        if not pipelined:
            wait_rows(abuf, acur, asem)
        for c in range(n_chunks):
            rs = slice(c * chunk, (c + 1) * chunk)
            smp, off = divmod(c * chunk, cap)
            y = yacc[rs] * (gate_ref[smp, 0, off:off + chunk] * g2_ref[smp])
            for j in range(tr):
                abuf[acur, pl.ds(c * chunk * tr + j, chunk, stride=tr), :] += y[:, j * LANES:(j + 1) * LANES]
        if not pipelined:
            row_copies(k, out_hbm, abuf, slot, ssem, False)

        @pl.when(k == n_steps - 1)
        def _():
            wait_rows(xbuf, 1 - slot, gsem)
            if pipelined:
                wait_rows(abuf, anext, asem)
                row_copies(k, out_hbm, abuf, acur, ssem, False)
                for back in range(na - 1):
                    wait_rows(abuf, lax.rem(k + na - back, na), ssem)
            else:
                if lag == 2:
                    @pl.when(k >= 1)
                    def _():
                        wait_rows(abuf, 1 - slot, ssem)
                wait_rows(abuf, slot, ssem)


def _ec_moe(x, h, aff, g2, layer, w_gate, w_up, w_down, dims):
    bx, n, d = dims
    tr = d // LANES
    e = aff.shape[1]
    ff = w_gate.shape[3]
    cap = EC_CAPACITY * n // e
    idx, gate = _route(aff, cap)
    rows_of = (idx + (jnp.arange(bx, dtype=jnp.int32) * n)[:, None, None])
    rows_of = jnp.transpose(rows_of, (1, 0, 2)).reshape(e * bx * cap)
    g2 = jnp.broadcast_to(g2, (bx, 1, d))
    nb = max(1, min(bx, MOE_TILE_ROWS // cap))
    assert bx % nb == 0
    rows = nb * cap
    n_tiles = bx // nb
    chunk = min(MOE_ROW_CHUNK, cap)
    assert cap % chunk == 0
    tf = min(FF_TILE_SMALL if rows <= FF_SMALL_ROWS else FF_TILE, ff)
    assert rows % ((ff // tf) * (rows // chunk)) == 0
    n_abuf = min(4, n_tiles) if n_tiles >= 3 else 2
    grid_spec = pltpu.PrefetchScalarGridSpec(
        num_scalar_prefetch=1,
        grid=(e, n_tiles, ff // tf),
        in_specs=[
            pl.BlockSpec(memory_space=pl.ANY),
            pl.BlockSpec((nb, 1, cap, 1), lambda ei, p, f, idx: (p, ei, 0, 0)),
            pl.BlockSpec((nb, 1, d), lambda ei, p, f, idx: (p, 0, 0)),
            pl.BlockSpec((1, 1, d, tf), lambda ei, p, f, idx: (layer, ei, 0, f)),
            pl.BlockSpec((1, 1, d, tf), lambda ei, p, f, idx: (layer, ei, 0, f)),
            pl.BlockSpec((1, 1, tf, d), lambda ei, p, f, idx: (layer, ei, f, 0)),
            pl.BlockSpec(memory_space=pl.ANY),
        ],
        out_specs=pl.BlockSpec(memory_space=pl.ANY),
        scratch_shapes=[
            pltpu.VMEM((2, rows * tr, LANES), F32),
            pltpu.VMEM((n_abuf, rows * tr, LANES), F32),
            pltpu.VMEM((rows, d), BF16),
            pltpu.VMEM((rows, d), F32),
            pltpu.SemaphoreType.DMA((2,)),
            pltpu.SemaphoreType.DMA((n_abuf,)),
            pltpu.SemaphoreType.DMA((n_abuf,)),
        ],
    )
    return pl.pallas_call(
        functools.partial(_moe_kernel, rows=rows, cap=cap, chunk=chunk, n_tiles=n_tiles, n_ff=ff // tf),
        out_shape=jax.ShapeDtypeStruct(x.shape, F32),
        grid_spec=grid_spec,
        input_output_aliases={7: 0},
        compiler_params=_params(("arbitrary", "arbitrary", "arbitrary"), VMEM_LIMIT),
        name="moe_ffn",
    )(rows_of, h, gate, g2, w_gate, w_up, w_down, x)


def kernel(x, c, ctx, c_ctx, w_mod, b_mod, norm_mix, norm_ffn, a_w_qkv, a_q_gain, a_k_gain, a_sink, a_w_o,
           b_w_qkv, b_q_gain, b_k_gain, b_rpb, b_w_o, moe_router, moe_w_gate, moe_w_up, moe_w_down):
    b, s, d = x.shape
    depth = w_mod.shape[0]
    pad = (-(b + 1)) % 8
    cvec = jnp.concatenate([c, c_ctx[None], jnp.zeros((pad, d), F32)], axis=0)
    mods = _modulation(cvec, w_mod, b_mod)
    rope_tabs = _rope_tables(s)
    xl, xc = x, ctx
    for i in range(depth):
        last = i == depth - 1
        til_l = dict(dims=x.shape) if i > 0 else {}
        til_c = dict(dims=ctx.shape) if i > 0 else {}
        ml =[m.reshape(b, 1, d) for m in jnp.split(mods[i, :b], 6, axis=-1)]
        mc = [m.reshape(1, 1, d) for m in jnp.split(mods[i, b:b + 1], 6, axis=-1)]
        sh1_l, sc1_l, g1_l, sh2_l, sc2_l, g2_l = ml
        sh1_c, sc1_c, g1_c, sh2_c, sc2_c, g2_c = mc
        j = i // N_MIXERS
        if i % N_MIXERS == 0:
            ql, kl, vl = _project(xl, norm_mix[i], sh1_l, sc1_l, a_w_qkv[j], a_q_gain[j], a_k_gain[j],
                                  A_HEADS, A_KV_HEADS, rope_tabs, **til_l)
            qc, kc, vc = _project(xc, norm_mix[i], sh1_c, sc1_c, a_w_qkv[j], a_q_gain[j], a_k_gain[j],
                                  A_HEADS, A_KV_HEADS, **til_c)
            ol = _attn_a(ql, kl, vl, kc, vc, a_sink[j])
            oc = None if last else _attn_ctx(qc, kc, vc, a_sink[j])
            w_o = a_w_o[j]
        else:
            ql, kl, vl = _project(xl, norm_mix[i], sh1_l, sc1_l, b_w_qkv[j], b_q_gain[j], b_k_gain[j],
                                  B_HEADS, B_HEADS, **til_l)
            qc, kc, vc = _project(xc, norm_mix[i], sh1_c, sc1_c, b_w_qkv[j], b_q_gain[j], b_k_gain[j],
                                  B_HEADS, B_HEADS, **til_c)
            ol = _attn_b(ql, kl, vl, kc, vc, b_rpb[j])
            oc = None if last else _attn_ctx(qc, kc, vc, None)
            w_o = b_w_o[j]
        xl, hl, aff_l = _out_proj(ol, w_o, xl, g1_l, norm_ffn[i], sh2_l, sc2_l, moe_router[i], **til_l)
        xl = _ec_moe(xl, hl, aff_l, g2_l, i, moe_w_gate, moe_w_up, moe_w_down, x.shape)
        if not last:
            xc, hc, aff_c = _out_proj(oc, w_o, xc, g1_c, norm_ffn[i], sh2_c, sc2_c, moe_router[i], **til_c)
            xc = _ec_moe(xc, hc, aff_c, g2_c, i, moe_w_gate, moe_w_up, moe_w_down, ctx.shape)
    return _untile(xl, x.shape)
```

```python
import functools

import numpy as np
import jax
import jax.numpy as jnp
from jax import lax
from jax.experimental import pallas as pl
from jax.experimental.pallas import tpu as pltpu

F32 = jnp.float32
BF16 = jnp.bfloat16

GRID_W = 64
HEAD_DIM = 64
A_HEADS = 16
A_KV_HEADS = 4
A_WINDOW = 128
A_BLOCK = 128
Q_BLOCK = 128
B_HEADS = 16
NA_ROWS = 8
NA_COLS = 16
N_EXPERTS = 16
EC_CAPACITY = 2
ROPE_BASE = 10000.0
EPS = 1e-6
NEG_INF = -1e30
ATTN_SCALE = HEAD_DIM ** -0.5
LOG2E = 1.4426950408889634
N_MIXERS = 2

NA_QROWS = 2
NA_KROWS = NA_ROWS + NA_QROWS
NA_HEAD_GROUP = 16

LANES = 128
TOKEN_TILE = 1024
VMEM_LIMIT = 56 * 1024 * 1024

_NT = (((1,), (1,)), ((), ()))
_NN = (((1,), (0,)), ((), ()))


def _dot(a, b, dims=_NN):
    return lax.dot_general(a, b, dims, preferred_element_type=F32)


def _split(a):
    hi = a.astype(BF16)
    lo = (a - hi.astype(F32)).astype(BF16)
    return hi, lo


def _dot3(a, b, dims=_NN):
    ah, al = _split(a)
    bh, bl = _split(b)
    return _dot(ah, bh, dims) + (_dot(ah, bl, dims) + _dot(al, bh, dims))


def _params(sem, vmem=None):
    return pltpu.CompilerParams(dimension_semantics=sem, vmem_limit_bytes=vmem)


def _mod_kernel(c_ref, w_ref, b_ref, o_ref):
    c = c_ref[...]
    a = c * jax.nn.sigmoid(c)
    o_ref[0] = _dot3(a, w_ref[0]) + b_ref[0]


def _modulation(cvec, w_mod, b_mod):
    depth, d, n = w_mod.shape
    rows = cvec.shape[0]
    tn = 1024
    return pl.pallas_call(
        _mod_kernel,
        out_shape=jax.ShapeDtypeStruct((depth, rows, n), F32),
        grid=(depth, n // tn),
        in_specs=[
            pl.BlockSpec((rows, d), lambda i, j: (0, 0)),
            pl.BlockSpec((1, d, tn), lambda i, j: (i, 0, j)),
            pl.BlockSpec((1, 1, tn), lambda i, j: (i, 0, j)),
        ],
        out_specs=pl.BlockSpec((1, rows, tn), lambda i, j: (i, 0, j)),
        compiler_params=_params(("parallel", "parallel")),
        name="modulation",
    )(cvec, w_mod, b_mod.reshape(depth, 1, n))


def _rms_mod(x, g, shift, scale):
    y = x * lax.rsqrt(jnp.mean(x * x, axis=-1, keepdims=True) + EPS)
    return (y * g) * (1.0 + scale) + shift


def _load_tokens(ref, tm):
    chunks = ref.shape[0] // tm
    return jnp.concatenate([ref[pl.ds(j, tm, stride=chunks), :] for j in range(chunks)], axis=1)


def _store_tokens(ref, val):
    tm, d = val.shape
    chunks = d // LANES
    for j in range(chunks):
        ref[pl.ds(j, tm, stride=chunks), :] = val[:, j * LANES:(j + 1) * LANES]


def _token_spec(tm, d, n_tiles, tiled):
    if tiled:
        return pl.BlockSpec((tm * (d // LANES), LANES), lambda b, t: (b * n_tiles + t, 0))
    return pl.BlockSpec((1, tm, d), lambda b, t: (b, t, 0))


def _proj_kernel(*refs, n_q, n_kv, rope, tiled):
    if rope:
        x_ref, g_ref, sh_ref, sc_ref, wt_ref, cq_ref, ck_ref, sq_ref, sk_ref, qt_ref, k_ref, vt_ref = refs
    else:
        x_ref, g_ref, sh_ref, sc_ref, wt_ref, cq_ref, ck_ref, qt_ref, k_ref, vt_ref = refs
        sq_ref = sk_ref = None
    dh = HEAD_DIM
    x = _load_tokens(x_ref, k_ref.shape[1]) if tiled else x_ref[0]
    h = _rms_mod(x, g_ref[...], sh_ref[0], sc_ref[0]).astype(BF16)
    res = _dot(wt_ref[...], h, _NT)

    def head(j, c_ref, s_ref):
        t = res[j * dh:(j + 1) * dh]
        tn = t * lax.rsqrt(jnp.mean(t * t, axis=0, keepdims=True) + EPS)
        out = tn * c_ref[...]
        if rope:
            q4 = dh // 4
            partner = jnp.concatenate([tn[q4:2 * q4], tn[:q4], tn[3 * q4:], tn[2 * q4:3 * q4]], axis=0)
            out = out + partner * s_ref[...]
        return out

    for j in range(n_q):
        qh = head(j, cq_ref, sq_ref).astype(BF16)
        for qb in range(qt_ref.shape[1]):
            qt_ref[0, qb, j] = qh[:, qb * Q_BLOCK:(qb + 1) * Q_BLOCK]
    ks =[head(n_q + j, ck_ref, sk_ref) for j in range(n_kv)]
    k_ref[0] = jnp.concatenate(ks, axis=0).T.astype(BF16)
    v0 = (n_q + n_kv) * dh
    for j in range(n_kv):
        vt_ref[0, j] = res[v0 + j * dh:v0 + (j + 1) * dh].astype(BF16)


def _rope_perm():
    d = np.arange(HEAD_DIM)
    return np.where((d % 32) < 16, d + 16, d - 16)


def _rope_tables(n_tok):
    pos = jnp.arange(n_tok)
    row = (pos // GRID_W).astype(F32)
    col = (pos % GRID_W).astype(F32)
    n_freq = HEAD_DIM // 4
    inv = ROPE_BASE ** (-jnp.arange(n_freq, dtype=F32) / n_freq)
    ang_r = row[:, None] * inv
    ang_c = col[:, None] * inv
    cos_t = jnp.concatenate([jnp.cos(ang_r), jnp.cos(ang_r), jnp.cos(ang_c), jnp.cos(ang_c)], axis=-1)
    sin_t = jnp.concatenate([-jnp.sin(ang_r), jnp.sin(ang_r), -jnp.sin(ang_c), jnp.sin(ang_c)], axis=-1)
    return cos_t.T, sin_t.T


def _project(x, g, shift, scale, w, q_gain, k_gain, n_q, n_kv, rope_tabs=None, dims=None):
    tiled = dims is not None
    bx, n, d = dims if tiled else x.shape
    dh = HEAD_DIM
    rope = rope_tabs is not None
    tm = min(TOKEN_TILE, n)
    qg = (q_gain.astype(F32) * (ATTN_SCALE * LOG2E)).reshape(dh, 1)
    kg = k_gain.astype(F32).reshape(dh, 1)
    wt = w.T.astype(BF16)
    n_cols = wt.shape[0]
    bm = shift.shape[0]
    mod_map = (lambda b, t: (b, 0, 0)) if bm == bx else (lambda b, t: (0, 0, 0))
    const2 = lambda b, t: (0, 0)
    tok_map = lambda b, t: (0, t)
    in_specs = [
        _token_spec(tm, d, n // tm, tiled),
        pl.BlockSpec((1, d), const2),
        pl.BlockSpec((1, 1, d), mod_map),
        pl.BlockSpec((1, 1, d), mod_map),
        pl.BlockSpec((n_cols, d), const2),
    ]
    args = [x, g.reshape(1, d), shift, scale, wt]
    if rope:
        cos_t, sin_t = rope_tabs
        perm = _rope_perm()
        args += [qg * cos_t, kg * cos_t, qg[perm] * sin_t, kg[perm] * sin_t]
        in_specs += [pl.BlockSpec((dh, tm), tok_map)] * 4
    else:
        args += [jnp.broadcast_to(qg, (dh, tm)), jnp.broadcast_to(kg, (dh, tm))]
        in_specs += [pl.BlockSpec((dh, tm), const2)] * 2
    assert tm % Q_BLOCK == 0
    out_shape = [jax.ShapeDtypeStruct((bx, n // Q_BLOCK, n_q, dh, Q_BLOCK), BF16),
                 jax.ShapeDtypeStruct((bx, n, n_kv * dh), BF16),
                 jax.ShapeDtypeStruct((bx, n_kv, dh, n), BF16)]
    out_specs = [pl.BlockSpec((1, tm // Q_BLOCK, n_q, dh, Q_BLOCK), lambda b, t: (b, t, 0, 0, 0)),
                 pl.BlockSpec((1, tm, n_kv * dh), lambda b, t: (b, t, 0)),
                 pl.BlockSpec((1, n_kv, dh, tm), lambda b, t: (b, 0, 0, t))]
    return pl.pallas_call(
        functools.partial(_proj_kernel, n_q=n_q, n_kv=n_kv, rope=rope, tiled=tiled),
        out_shape=out_shape,
        grid=(bx, n // tm),
        in_specs=in_specs,
        out_specs=out_specs,
        compiler_params=_params(("parallel", "parallel"), VMEM_LIMIT),
        name="qkv_proj_rope" if rope else "qkv_proj",
    )(*args)


def _softmax_pv_t(scores, vts, sinks):
    ms = []
    for parts, sink in zip(scores, sinks):
        m = parts[0].max(axis=0, keepdims=True)
        for s in parts[1:]:
            m = jnp.maximum(m, s.max(axis=0, keepdims=True))
        ms.append(m if sink is None else jnp.maximum(m, sink))
    probs, dens = [], []
    for parts, m, sink in zip(scores, ms, sinks):
        ps = [jnp.exp2(s - m) for s in parts]
        den = ps[0].sum(axis=0, keepdims=True)
        for p in ps[1:]:
            den = den + p.sum(axis=0, keepdims=True)
        dens.append(den if sink is None else den + jnp.exp2(sink - m))
        probs.append([p.astype(BF16) for p in ps])
    outs = []
    for ps, vt_parts, den in zip(probs, vts, dens):
        acc = _dot(vt_parts[0], ps[0])
        for vt, p in zip(vt_parts[1:], ps[1:]):
            acc = acc + _dot(vt, p)
        outs.append(acc / den)
    return outs


def _attn_a_kernel(qt_ref, k_ref, vt_ref, kc_ref, vct_ref, sink_ref, mask_ref, o_ref, *, seq, n_kv, group):
    blk = A_BLOCK
    span = 3 * blk
    dh = HEAD_DIM
    i = pl.program_id(1)
    start = pl.multiple_of(jnp.clip((i - 1) * blk, 0, seq - span), blk)
    scores, vts = [], []
    for h in range(n_kv):
        qt = jnp.concatenate([qt_ref[0, 0, h * group + g] for g in range(group)], axis=1)
        kw = k_ref[0, pl.ds(start, span), h * dh:(h + 1) * dh]
        s_loc = _dot(kw, qt) + mask_ref[0]
        s_ctx = _dot(kc_ref[0, :, h * dh:(h + 1) * dh], qt)
        scores.append([s_loc, s_ctx])
        vts.append([vt_ref[0, h, :, pl.ds(start, span)], vct_ref[0, h]])
    ots = _softmax_pv_t(scores, vts, [sink_ref[h] for h in range(n_kv)])
    outs = [ot[:, g * blk:(g + 1) * blk] for ot in ots for g in range(group)]
    o_ref[0] = jnp.concatenate(outs, axis=0).T.astype(BF16)


def _sink_rows(sink, n_kv, group, cols):
    s = (sink.astype(F32) * LOG2E).reshape(n_kv, 1, group, 1)
    return jnp.broadcast_to(s, (n_kv, 1, group, cols)).reshape(n_kv, 1, group * cols)


def _attn_a(qt, k, vt, kc, vct, sink):
    b, _, hq, dh, blk = qt.shape
    s = k.shape[1]
    n_kv = vt.shape[1]
    group = hq // n_kv
    nc = kc.shape[1]
    assert blk == A_BLOCK
    n_blk = s // blk
    assert s % blk == 0 and n_blk >= 3
    off = (np.arange(3) * blk)[:, None, None]
    rel = off + (np.arange(group * blk) % blk)[None, None, :] - np.arange(3 * blk)[None, :, None]
    mask = jnp.asarray(np.where(np.abs(rel) <= A_WINDOW, 0.0, NEG_INF).astype(np.float32))
    return pl.pallas_call(
        functools.partial(_attn_a_kernel, seq=s, n_kv=n_kv, group=group),
        out_shape=jax.ShapeDtypeStruct((b, s, hq * dh), BF16),
        grid=(b, s // blk),
        in_specs=[
            pl.BlockSpec((1, 1, hq, dh, blk), lambda bi, i: (bi, i, 0, 0, 0)),
            pl.BlockSpec((1, s, n_kv * dh), lambda bi, i: (bi, 0, 0)),
            pl.BlockSpec((1, n_kv, dh, s), lambda bi, i: (bi, 0, 0, 0)),
            pl.BlockSpec((1, nc, n_kv * dh), lambda bi, i: (bi, 0, 0)),
            pl.BlockSpec((1, n_kv, dh, nc), lambda bi, i: (bi, 0, 0, 0)),
            pl.BlockSpec((n_kv, 1, group * blk), lambda bi, i: (0, 0, 0)),
            pl.BlockSpec((1, 3 * blk, group * blk),
                         lambda bi, i: (jnp.where(i == 0, 0, jnp.where(i == n_blk - 1, 2, 1)), 0, 0)),
        ],
        out_specs=pl.BlockSpec((1, blk, hq * dh), lambda bi, i: (bi, i, 0)),
        compiler_params=_params(("parallel", "arbitrary"), VMEM_LIMIT),
        name="attn_window",
    )(qt, k, vt, kc, vct, _sink_rows(sink, n_kv, group, blk), mask)


def _attn_ctx_kernel(qt_ref, k_ref, vt_ref, sink_ref, o_ref, *, n_kv, group, use_sink):
    dh = HEAD_DIM
    n_qb = qt_ref.shape[1]
    nq = n_qb * qt_ref.shape[4]
    scores, vts = [], []
    for h in range(n_kv):
        qt = jnp.concatenate([qt_ref[0, qb, h * group + g] for g in range(group) for qb in range(n_qb)], axis=1)
        scores.append([_dot(k_ref[0, :, h * dh:(h + 1) * dh], qt)])
        vts.append([vt_ref[0, h]])
    ots = _softmax_pv_t(scores, vts, [sink_ref[h] if use_sink else None for h in range(n_kv)])
    outs = [ot[:, g * nq:(g + 1) * nq] for ot in ots for g in range(group)]
    o_ref[0] = jnp.concatenate(outs, axis=0).T.astype(BF16)


def _attn_ctx(qt, k, vt, sink):
    b, n_qb, hq, dh, qblk = qt.shape
    n = n_qb * qblk
    n_kv = vt.shape[1]
    group = hq // n_kv
    use_sink = sink is not None
    sink_r = _sink_rows(sink if use_sink else jnp.zeros((hq,), F32), n_kv, group, n)
    return pl.pallas_call(
        functools.partial(_attn_ctx_kernel, n_kv=n_kv, group=group, use_sink=use_sink),
        out_shape=jax.ShapeDtypeStruct((b, n, hq * dh), BF16),
        grid=(b,),
        in_specs=[
            pl.BlockSpec((1, n_qb, hq, dh, qblk), lambda bi: (bi, 0, 0, 0, 0)),
            pl.BlockSpec((1, n, n_kv * dh), lambda bi: (bi, 0, 0)),
            pl.BlockSpec((1, n_kv, dh, n), lambda bi: (bi, 0, 0, 0)),
            pl.BlockSpec((n_kv, 1, group * n), lambda bi: (0, 0, 0)),
        ],
        out_specs=pl.BlockSpec((1, n, hq * dh), lambda bi: (bi, 0, 0)),
        compiler_params=_params(("parallel",)),
        name="attn_ctx",
    )(qt, k, vt, sink_r)


def _na_bias_tables(rpb, n_rows):
    w = GRID_W
    n_blocks = n_rows // NA_QROWS
    half = NA_ROWS // 2
    cfgs, cfg_of = {}, []
    for rb in range(n_blocks):
        r0 = rb * NA_QROWS
        rs = int(np.clip(r0 - half, 0, n_rows - NA_KROWS))
        key = (r0 - rs,) + tuple(int(np.clip(r0 + a - half, 0, n_rows - NA_ROWS)) - rs for a in range(NA_QROWS))
        cfg_of.append(cfgs.setdefault(key, len(cfgs)))
    kc = np.arange(w)[:, None]
    qc = np.arange(w)[None, :]
    wstart = np.clip(qc - NA_COLS // 2, 0, w - NA_COLS)
    col_ok = (kc >= wstart) & (kc < wstart + NA_COLS)
    dcol = np.clip(kc - qc, -(NA_COLS - 1), NA_COLS - 1) + NA_COLS - 1
    onehot = (dcol[None] == np.arange(2 * NA_COLS - 1)[:, None, None]).astype(np.float32)
    tiles = jnp.einsum('hrd,dkq->hrkq', rpb.astype(F32), jnp.asarray(onehot), precision=lax.Precision.HIGHEST)
    tiles = jnp.where(col_ok[None, None], tiles * LOG2E, NEG_INF)
    masked = jnp.full((rpb.shape[0], w, w), NEG_INF, F32)
    tabs = []
    for key in cfgs:
        off, rsr = key[0], key[1:]
        rows = []
        for c in range(NA_KROWS):
            cols = []
            for a in range(NA_QROWS):
                assert 0 <= rsr[a] and rsr[a] + NA_ROWS <= NA_KROWS
                row_ok = rsr[a] <= c < rsr[a] + NA_ROWS
                cols.append(tiles[:, c - off - a + NA_ROWS - 1] if row_ok else masked)
            rows.append(jnp.concatenate(cols, axis=2))
        tabs.append(jnp.concatenate(rows, axis=1))
    return jnp.asarray(np.array(cfg_of, np.int32)), jnp.stack(tabs)


def _attn_b_kernel(cfg_ref, qt_ref, k_ref, vt_ref, kc_ref, vct_ref, bias_ref, o_ref, *, n_rows, heads):
    del cfg_ref
    w = GRID_W
    dh = HEAD_DIM
    rb = pl.program_id(2)
    rs2 = jnp.clip(rb - NA_ROWS // (2 * NA_QROWS), 0, (n_rows - NA_KROWS) // NA_QROWS)
    start = pl.multiple_of(rs2 * (NA_QROWS * w), NA_QROWS * w)
    span = NA_KROWS * w
    scores, vts = [], []
    for h in range(heads):
        qt = qt_ref[0, 0, h]
        kw = k_ref[0, pl.ds(start, span), h * dh:(h + 1) * dh]
        s_loc = _dot(kw, qt) + bias_ref[0, h]
        s_ctx = _dot(kc_ref[0, :, h * dh:(h + 1) * dh], qt)
        scores.append([s_loc, s_ctx])
        vts.append([vt_ref[0, h, :, pl.ds(start, span)], vct_ref[0, h]])
    outs = _softmax_pv_t(scores, vts, [None] * heads)
    o_ref[0] = jnp.concatenate(outs, axis=0).T.astype(BF16)


def _attn_b(qt, k, vt, kc, vct, rpb):
    b, _, hq, dh, qblk = qt.shape
    s = k.shape[1]
    nc = kc.shape[1]
    w = GRID_W
    n_rows = s // w
    assert s % w == 0 and n_rows >= NA_KROWS and n_rows % NA_QROWS == 0 and qblk == NA_QROWS * w
    cfg_of, table = _na_bias_tables(rpb, n_rows)
    hg = NA_HEAD_GROUP
    tq = NA_QROWS * w
    span = NA_KROWS * w
    grid_spec = pltpu.PrefetchScalarGridSpec(
        num_scalar_prefetch=1,
        grid=(b, hq // hg, n_rows // NA_QROWS),
        in_specs=[
            pl.BlockSpec((1, 1, hg, dh, tq), lambda bi, g, r, cfg: (bi, r, g, 0, 0)),
            pl.BlockSpec((1, s, hg * dh), lambda bi, g, r, cfg: (bi, 0, g), pipeline_mode=pl.Buffered(1)),
            pl.BlockSpec((1, hg, dh, s), lambda bi, g, r, cfg: (bi, g, 0, 0), pipeline_mode=pl.Buffered(1)),
            pl.BlockSpec((1, nc, hg * dh), lambda bi, g, r, cfg: (bi, 0, g)),
            pl.BlockSpec((1, hg, dh, nc), lambda bi, g, r, cfg: (bi, g, 0, 0)),
            pl.BlockSpec((1, hg, span, tq), lambda bi, g, r, cfg: (cfg[r], g, 0, 0)),
        ],
        out_specs=pl.BlockSpec((1, tq, hg * dh), lambda bi, g, r, cfg: (bi, r, g)),
    )
    return pl.pallas_call(
        functools.partial(_attn_b_kernel, n_rows=n_rows, heads=hg),
        out_shape=jax.ShapeDtypeStruct((b, s, hq * dh), BF16),
        grid_spec=grid_spec,
        compiler_params=_params(("parallel", "parallel", "arbitrary"), VMEM_LIMIT),
        name="attn_neighbourhood",
    )(cfg_of, qt, k, vt, kc, vct, table)


def _out_kernel(o_ref, wo_ref, x_ref, g1_ref, g_ref, sh_ref, sc_ref, wr_ref, xn_ref, h_ref, aff_ref, *, tiled):
    y = _dot(o_ref[0], wo_ref[...])
    x = _load_tokens(x_ref, o_ref.shape[1]) if tiled else x_ref[0]
    xn = x + g1_ref[0] * y
    _store_tokens(xn_ref, xn)
    h = _rms_mod(xn, g_ref[...], sh_ref[0], sc_ref[0])
    _store_tokens(h_ref, h)
    logits = _dot3(wr_ref[...], h, _NT)
    z = jnp.exp(logits - logits.max(axis=0, keepdims=True))
    aff_ref[0] = z / z.sum(axis=0, keepdims=True)


def _out_proj(o, w_o, x, g1, g, shift, scale, w_router, dims=None):
    tiled = dims is not None
    bx, n, d = dims if tiled else x.shape
    assert d % (8 * LANES) == 0
    da = o.shape[2]
    e = w_router.shape[1]
    tm = min(TOKEN_TILE, n)
    bm = g1.shape[0]
    mod_map = (lambda b, t: (b, 0, 0)) if bm == bx else (lambda b, t: (0, 0, 0))
    const2 = lambda b, t: (0, 0)
    mod_spec = pl.BlockSpec((1, 1, d), mod_map)
    tiled_shape = jax.ShapeDtypeStruct((bx * n * (d // LANES), LANES), F32)
    return pl.pallas_call(
        functools.partial(_out_kernel, tiled=tiled),
        out_shape=[tiled_shape, tiled_shape, jax.ShapeDtypeStruct((bx, e, n), F32)],
        grid=(bx, n // tm),
        in_specs=[
            pl.BlockSpec((1, tm, da), lambda b, t: (b, t, 0)),
            pl.BlockSpec((da, d), const2),
            _token_spec(tm, d, n // tm, tiled),
            mod_spec,
            pl.BlockSpec((1, d), const2),
            mod_spec,
            mod_spec,
            pl.BlockSpec((e, d), const2),
        ],
        out_specs=[_token_spec(tm, d, n // tm, True),
                   _token_spec(tm, d, n // tm, True),
                   pl.BlockSpec((1, e, tm), lambda b, t: (b, 0, t))],
        compiler_params=_params(("parallel", "parallel"), VMEM_LIMIT),
        name="out_proj_router",
    )(o, w_o.astype(BF16), x, g1, g.reshape(1, d), shift, scale, w_router.T)


def _untile_kernel(x_ref, o_ref):
    o_ref[0] = _load_tokens(x_ref, o_ref.shape[1])


def _untile(x, dims):
    bx, n, d = dims
    tm = min(TOKEN_TILE, n)
    return pl.pallas_call(
        _untile_kernel,
        out_shape=jax.ShapeDtypeStruct(dims, F32),
        grid=(bx, n // tm),
        in_specs=[_token_spec(tm, d, n // tm, True)],
        out_specs=_token_spec(tm, d, n // tm, False),
        compiler_params=_params(("parallel", "parallel")),
        name="untile_tokens",
    )(x)


ROUTE_ROW_ALIGN = 16
ROUTE_BISECT_STEPS = 30
WEIGHT_RING = 3
FF_TILE = 512
FF_TILE_SMALL = 1024
FF_SMALL_ROWS = 256
MOE_TILE_ROWS = 1024
MOE_ROW_CHUNK = 512
_TN = (((0,), (0,)), ((), ()))


def _route_kernel(aff_ref, tri_ref, lmat_ref, idx_ref, gate_ref, *, cap):
    n_exp, n_rows, lanes = aff_ref.shape[1:]
    aff = aff_ref[0]

    def reduce_tokens(x, op):
        return op(op(x, axis=1, keepdims=True), axis=2, keepdims=True)

    def count_ge(t):
        return reduce_tokens(jnp.where(aff >= t, 1.0, 0.0), jnp.sum)

    def max_below(t):
        return reduce_tokens(jnp.where(aff < t, aff, -1.0), jnp.max)

    def bisect(_, lohi):
        lo, hi = lohi
        mid = 0.5 * (lo + hi)
        ok = count_ge(mid) >= cap
        return jnp.where(ok, mid, lo), jnp.where(ok, hi, mid)

    top = reduce_tokens(aff, jnp.max)
    _, hi = lax.fori_loop(0, ROUTE_BISECT_STEPS, bisect, (jnp.zeros_like(top), 2.0 * top + 1e-30))

    def short(t):
        return jnp.max(jnp.where(count_ge(t) < cap, 1.0, 0.0)) > 0.5

    def walk(t):
        return jnp.where(count_ge(t) >= cap, t, max_below(t))

    thr = lax.while_loop(short, walk, max_below(hi))
    above = aff > thr
    equal = aff == thr
    need = cap - reduce_tokens(jnp.where(above, 1.0, 0.0), jnp.sum)

    def prefix(x3):
        x2 = x3.reshape(n_exp * n_rows, lanes)
        inrow = _dot(x2.astype(BF16), tri_ref[...])
        tot = inrow[:, lanes - 1:lanes]
        offs = _dot(lmat_ref[...], jnp.broadcast_to(tot, inrow.shape).astype(BF16))
        return inrow, offs, tot

    eq_f = jnp.where(equal, 1.0, 0.0)
    eq_in, eq_off, _ = prefix(eq_f)
    eq_rank = (eq_in + eq_off).reshape(n_exp, n_rows, lanes) - eq_f
    sel_f = jnp.where(above | (equal & (eq_rank < need)), 1.0, 0.0)
    inrow, offs, tot = prefix(sel_f)
    sel2 = sel_f.reshape(n_exp * n_rows, lanes)
    aff2 = aff.reshape(n_exp * n_rows, lanes)
    a_hi = aff2.astype(BF16)
    r1 = aff2 - a_hi.astype(F32)
    a_mid = r1.astype(BF16)
    a_lo = (r1 - a_mid.astype(F32)).astype(BF16)

    slot = lax.broadcasted_iota(jnp.int32, (1, cap), 1).astype(F32)
    row_id = lax.broadcasted_iota(jnp.int32, (n_rows, 1), 0).astype(F32)
    lane_id = lax.broadcasted_iota(jnp.int32, (lanes, 1), 0).astype(F32)
    for ei in range(n_exp):
        rows = slice(ei * n_rows, (ei + 1) * n_rows)
        off_e = offs[rows, 0:1]
        tot_e = tot[rows]
        in_row = jnp.where((off_e <= slot) & (slot < off_e + tot_e), 1.0, 0.0)
        row_base = (in_row * off_e).sum(axis=0, keepdims=True)
        row_of = (in_row * row_id).sum(axis=0, keepdims=True)
        in_row_b = in_row.astype(BF16)
        pick = lambda v: _dot(v, in_row_b, _TN)
        hit = (pick(inrow[rows].astype(BF16)) == slot + 1.0 - row_base) & (pick(sel2[rows].astype(BF16)) > 0.5)
        hit_f = jnp.where(hit, 1.0, 0.0)
        aff_rows = (pick(a_hi[rows]) + pick(a_mid[rows])) + pick(a_lo[rows])
        lane_of = (hit_f * lane_id).sum(axis=0, keepdims=True)
        idx_ref[0, ei:ei + 1, :] = (row_of * lanes + lane_of).astype(jnp.int32)
        gate_ref[0, ei] = (hit_f * aff_rows).sum(axis=0, keepdims=True).T


def _route(aff, cap):
    bx, e, n = aff.shape
    unit = LANES * ROUTE_ROW_ALIGN
    n_pad = -(-n // unit) * unit
    if n_pad != n:
        aff = jnp.pad(aff, ((0, 0), (0, 0), (0, n_pad - n)), constant_values=-1.0)
    n_rows = n_pad // LANES
    tri = np.triu(np.ones((LANES, LANES), np.float32))
    r = np.arange(e * n_rows)
    lmat = ((r[:, None] // n_rows == r[None, :] // n_rows) & (r[None, :] < r[:, None])).astype(np.float32)
    return pl.pallas_call(
        functools.partial(_route_kernel, cap=cap),
        out_shape=[jax.ShapeDtypeStruct((bx, e, cap), jnp.int32), jax.ShapeDtypeStruct((bx, e, cap, 1), F32)],
        grid=(bx,),
        in_specs=[
            pl.BlockSpec((1, e, n_rows, LANES), lambda b: (b, 0, 0, 0)),
            pl.BlockSpec((LANES, LANES), lambda b: (0, 0)),
            pl.BlockSpec((e * n_rows, e * n_rows), lambda b: (0, 0)),
        ],
        out_specs=[pl.BlockSpec((1, e, cap), lambda b: (b, 0, 0)), pl.BlockSpec((1, e, cap, 1), lambda b: (b, 0, 0, 0))],
        compiler_params=_params(("parallel",)),
        name="moe_route",
    )(aff.reshape(bx, e, n_rows, LANES), jnp.asarray(tri, BF16), jnp.asarray(lmat, BF16))


def _moe_kernel(idx_ref, h_hbm, gate_ref, g2_ref, wg_hbm, wu_hbm, wd_hbm, x_hbm, out_hbm,
                xbuf, abuf, xb, yacc, wgb, wub, wdb, gsem, asem, ssem, wsem,
                *, rows, cap, chunk, n_tiles, n_ff, layer):
    del x_hbm
    tr = xbuf.shape[1] // rows
    e, p, f = pl.program_id(0), pl.program_id(1), pl.program_id(2)
    n_steps = pl.num_programs(0) * n_tiles
    k = e * n_tiles + p
    slot = lax.rem(k, 2)
    lag = 2 if n_tiles >= 2 else 1

    def row_copy(step, i, src, dst, dslot, sem, gather, priority):
        tok = idx_ref[step * rows + i]
        hbm_row = src.at[pl.ds(pl.multiple_of(tok * tr, tr), tr)]
        buf_row = dst.at[dslot, pl.ds(pl.multiple_of(i * tr, tr), tr)]
        cp = (pltpu.make_async_copy(hbm_row, buf_row, sem.at[dslot]) if gather
              else pltpu.make_async_copy(buf_row, hbm_row, sem.at[dslot]))
        cp.start(priority=priority)

    def row_copies(step, src, dst, dslot, sem, gather):
        def body(i, carry):
            for u in range(2):
                row_copy(step, 2 * i + u, src, dst, dslot, sem, gather, u)
            return carry

        lax.fori_loop(0, rows // 2, body, 0, unroll=4)

    def row_copies_inline(step, lo, count, src, dst, dslot, sem, gather):
        for u in range(count):
            row_copy(step, lo + u, src, dst, dslot, sem, gather, u % 2)

    def wait_rows(buf, bslot, sem):
        pltpu.make_async_copy(buf.at[bslot], buf.at[bslot], sem.at[bslot]).wait()

    na = abuf.shape[0]
    pipelined = na >= 3
    acur = lax.rem(k, na) if pipelined else slot
    anext, aprev = lax.rem(k + 1, na), lax.rem(k + na - 1, na)

    @pl.when(f == 0)
    def _begin():
        @pl.when(k == 0)
        def _():
            row_copies(k, h_hbm, xbuf, slot, gsem, True)
            if pipelined:
                row_copies(k, out_hbm, abuf, acur, asem, True)
                row_copies(k, out_hbm, abuf, aprev, asem, True)
                wait_rows(abuf, aprev, asem)

        if pipelined:
            @pl.when(k == 1)
            def _():
                wait_rows(abuf, na - 1, ssem)

            @pl.when(k >= na - 1)
            def _():
                wait_rows(abuf, anext, ssem)
        else:
            @pl.when(k >= lag)
            def _():
                wait_rows(abuf, lax.rem(k - lag, 2), ssem)

            row_copies(k, out_hbm, abuf, slot, asem, True)
        wait_rows(xbuf, slot, gsem)
        for j in range(tr):
            xb[:, j * LANES:(j + 1) * LANES] = xbuf[slot, pl.ds(j, rows, stride=tr), :].astype(BF16)
        yacc[...] = jnp.zeros_like(yacc)
        if pipelined:
            wait_rows(abuf, acur, asem)

    tf_w = wgb.shape[2]
    g = k * n_ff + f
    n_grid = n_steps * n_ff

    def weight_copies(gg, ws):
        ge = lax.div(gg, n_tiles * n_ff)
        cols = pl.ds(pl.multiple_of(lax.rem(gg, n_ff) * tf_w, tf_w), tf_w)
        return (pltpu.make_async_copy(wg_hbm.at[layer, ge, :, cols], wgb.at[ws], wsem.at[ws]),
                pltpu.make_async_copy(wu_hbm.at[layer, ge, :, cols], wub.at[ws], wsem.at[ws]),
                pltpu.make_async_copy(wd_hbm.at[layer, ge, cols, :], wdb.at[ws], wsem.at[ws]))

    @pl.when(g == 0)
    def _():
        for g0 in range(min(2, WEIGHT_RING - 1)):
            for cp in weight_copies(g0, g0):
                cp.start()

    @pl.when(g + 2 < n_grid)
    def _():
        for cp in weight_copies(g + 2, lax.rem(g + 2, WEIGHT_RING)):
            cp.start()

    wslot = lax.rem(g, WEIGHT_RING)
    for cp in weight_copies(g, wslot):
        cp.wait()
    wg = wgb[wslot].astype(BF16)
    wu = wub[wslot].astype(BF16)
    wd = wdb[wslot].astype(BF16)
    n_chunks = rows // chunk
    ahead = rows // (n_ff * n_chunks)
    nxt = jnp.minimum(k + 1, n_steps - 1)
    prv = jnp.maximum(k - 1, 0)
    for c in range(n_chunks):
        rs = slice(c * chunk, (c + 1) * chunk)
        lo = (f * n_chunks + c) * ahead
        x = xb[rs]
        a = _dot(x, wg)
        row_copies_inline(nxt, lo, ahead, h_hbm, xbuf, 1 - slot, gsem, True)
        u = _dot(x, wu)
        if pipelined:
            row_copies_inline(nxt, lo, ahead, out_hbm, abuf, anext, asem, True)
        hcat = ((a * jax.nn.sigmoid(a)) * u).astype(BF16)
        yacc[rs] += _dot(hcat, wd)
        if pipelined:
            row_copies_inline(prv, lo, ahead, out_hbm, abuf, aprev, ssem, False)

    @pl.when(f == pl.num_programs(2) - 1)
    def _finish():
        if not pipelined:
            wait_rows(abuf, acur, asem)
        for c in range(n_chunks):
            rs = slice(c * chunk, (c + 1) * chunk)
            smp, off = divmod(c * chunk, cap)
            y = yacc[rs] * (gate_ref[smp, 0, off:off + chunk] * g2_ref[smp])
            for j in range(tr):
                abuf[acur, pl.ds(c * chunk * tr + j, chunk, stride=tr), :] += y[:, j * LANES:(j + 1) * LANES]
        if not pipelined:
            row_copies(k, out_hbm, abuf, slot, ssem, False)

        @pl.when(k == n_steps - 1)
        def _():
            wait_rows(xbuf, 1 - slot, gsem)
            if pipelined:
                wait_rows(abuf, anext, asem)
                row_copies(k, out_hbm, abuf, acur, ssem, False)
                for back in range(na - 1):
                    wait_rows(abuf, lax.rem(k + na - back, na), ssem)
            else:
                if lag == 2:
                    @pl.when(k >= 1)
                    def _():
                        wait_rows(abuf, 1 - slot, ssem)
                wait_rows(abuf, slot, ssem)


def _ec_moe(x, h, aff, g2, layer, w_gate, w_up, w_down, dims):
    bx, n, d = dims
    tr = d // LANES
    e = aff.shape[1]
    ff = w_gate.shape[3]
    cap = EC_CAPACITY * n // e
    idx, gate = _route(aff, cap)
    rows_of = (idx + (jnp.arange(bx, dtype=jnp.int32) * n)[:, None, None])
    rows_of = jnp.transpose(rows_of, (1, 0, 2)).reshape(e * bx * cap)
    g2 = jnp.broadcast_to(g2, (bx, 1, d))
    nb = max(1, min(bx, MOE_TILE_ROWS // cap))
    assert bx % nb == 0
    rows = nb * cap
    n_tiles = bx // nb
    chunk = min(MOE_ROW_CHUNK, cap)
    assert cap % chunk == 0
    tf = min(FF_TILE_SMALL if rows <= FF_SMALL_ROWS else FF_TILE, ff)
    assert rows % ((ff // tf) * (rows // chunk)) == 0
    n_abuf = min(4, n_tiles) if n_tiles >= 3 else 2
    grid_spec = pltpu.PrefetchScalarGridSpec(
        num_scalar_prefetch=1,
        grid=(e, n_tiles, ff // tf),
        in_specs=[
            pl.BlockSpec(memory_space=pl.ANY),
            pl.BlockSpec((nb, 1, cap, 1), lambda ei, p, f, idx: (p, ei, 0, 0)),
            pl.BlockSpec((nb, 1, d), lambda ei, p, f, idx: (p, 0, 0)),
            pl.BlockSpec(memory_space=pl.ANY),
            pl.BlockSpec(memory_space=pl.ANY),
            pl.BlockSpec(memory_space=pl.ANY),
            pl.BlockSpec(memory_space=pl.ANY),
        ],
        out_specs=pl.BlockSpec(memory_space=pl.ANY),
        scratch_shapes=[
            pltpu.VMEM((2, rows * tr, LANES), F32),
            pltpu.VMEM((n_abuf, rows * tr, LANES), F32),
            pltpu.VMEM((rows, d), BF16),
            pltpu.VMEM((rows, d), F32),
            pltpu.VMEM((WEIGHT_RING, d, tf), F32),
            pltpu.VMEM((WEIGHT_RING, d, tf), F32),
            pltpu.VMEM((WEIGHT_RING, tf, d), F32),
            pltpu.SemaphoreType.DMA((2,)),
            pltpu.SemaphoreType.DMA((n_abuf,)),
            pltpu.SemaphoreType.DMA((n_abuf,)),
            pltpu.SemaphoreType.DMA((WEIGHT_RING,)),
        ],
    )
    return pl.pallas_call(
        functools.partial(_moe_kernel, rows=rows, cap=cap, chunk=chunk, n_tiles=n_tiles, n_ff=ff // tf, layer=layer),
        out_shape=jax.ShapeDtypeStruct(x.shape, F32),
        grid_spec=grid_spec,
        input_output_aliases={7: 0},
        compiler_params=_params(("arbitrary", "arbitrary", "arbitrary"), VMEM_LIMIT),
        name="moe_ffn",
    )(rows_of, h, gate, g2, w_gate, w_up, w_down, x)


def kernel(x, c, ctx, c_ctx, w_mod, b_mod, norm_mix, norm_ffn, a_w_qkv, a_q_gain, a_k_gain, a_sink, a_w_o,
           b_w_qkv, b_q_gain, b_k_gain, b_rpb, b_w_o, moe_router, moe_w_gate, moe_w_up, moe_w_down):
    b, s, d = x.shape
    depth = w_mod.shape[0]
    pad = (-(b + 1)) % 8
    cvec = jnp.concatenate([c, c_ctx[None], jnp.zeros((pad, d), F32)], axis=0)
    mods = _modulation(cvec, w_mod, b_mod)
    rope_tabs = _rope_tables(s)
    xl, xc = x, ctx
    for i in range(depth):
        last = i == depth - 1
        til_l = dict(dims=x.shape) if i > 0 else {}
        til_c = dict(dims=ctx.shape) if i > 0 else {}
        ml =[m.reshape(b, 1, d) for m in jnp.split(mods[i, :b], 6, axis=-1)]
        mc = [m.reshape(1, 1, d) for m in jnp.split(mods[i, b:b + 1], 6, axis=-1)]
        sh1_l, sc1_l, g1_l, sh2_l, sc2_l, g2_l = ml
        sh1_c, sc1_c, g1_c, sh2_c, sc2_c, g2_c = mc
        j = i // N_MIXERS
        if i % N_MIXERS == 0:
            ql, kl, vl = _project(xl, norm_mix[i], sh1_l, sc1_l, a_w_qkv[j], a_q_gain[j], a_k_gain[j],
                                  A_HEADS, A_KV_HEADS, rope_tabs, **til_l)
            qc, kc, vc = _project(xc, norm_mix[i], sh1_c, sc1_c, a_w_qkv[j], a_q_gain[j], a_k_gain[j],
                                  A_HEADS, A_KV_HEADS, **til_c)
            ol = _attn_a(ql, kl, vl, kc, vc, a_sink[j])
            oc = None if last else _attn_ctx(qc, kc, vc, a_sink[j])
            w_o = a_w_o[j]
        else:
            ql, kl, vl = _project(xl, norm_mix[i], sh1_l, sc1_l, b_w_qkv[j], b_q_gain[j], b_k_gain[j],
                                  B_HEADS, B_HEADS, **til_l)
            qc, kc, vc = _project(xc, norm_mix[i], sh1_c, sc1_c, b_w_qkv[j], b_q_gain[j], b_k_gain[j],
                                  B_HEADS, B_HEADS, **til_c)
            ol = _attn_b(ql, kl, vl, kc, vc, b_rpb[j])
            oc = None if last else _attn_ctx(qc, kc, vc, None)
            w_o = b_w_o[j]
        xl, hl, aff_l = _out_proj(ol, w_o, xl, g1_l, norm_ffn[i], sh2_l, sc2_l, moe_router[i], **til_l)
        xl = _ec_moe(xl, hl, aff_l, g2_l, i, moe_w_gate, moe_w_up, moe_w_down, x.shape)
        if not last:
            xc, hc, aff_c = _out_proj(oc, w_o, xc, g1_c, norm_ffn[i], sh2_c, sc2_c, moe_router[i], **til_c)
            xc = _ec_moe(xc, hc, aff_c, g2_c, i, moe_w_gate, moe_w_up, moe_w_down, ctx.shape)
    return _untile(xl, x.shape)
```
